```python
import math
import jax, jax.numpy as jnp
from jax import lax
import numpy as np

D_MODEL = 1024
BATCH = 8
SEQ = 2048
DEPTH = 4
DEC_BATCH = 32
DEC_SEQ = 4
PAST_LEN = 8192
PAGE_SIZE = 128

N_MIXERS = 3
N_POOL = len([i for i in range(DEPTH) if i % N_MIXERS == 0])
N_DIFF = len([i for i in range(DEPTH) if i % N_MIXERS == 1])
N_SSD = len([i for i in range(DEPTH) if i % N_MIXERS == 2])

POOL_WINDOWS = (2, 4, 8, 16)
POOL_GROUPS = len(POOL_WINDOWS)
POOL_GD = D_MODEL // POOL_GROUPS
POOL_BUF = max(POOL_WINDOWS) - 1

DIFF_DH = 64
DIFF_VD = 2 * DIFF_DH
DIFF_HEADS = D_MODEL // DIFF_VD
Q_BLOCK = 128
ROPE_THETA = 10000.0

SSD_DI = 2 * D_MODEL
SSD_P = 64
SSD_H = SSD_DI // SSD_P
SSD_G = 4
SSD_N = 128
SSD_CONV = 4
SSD_CONV_DIM = SSD_DI + 2 * SSD_G * SSD_N
SSD_IN = SSD_DI + SSD_CONV_DIM + SSD_H
SSD_CHUNK = 128

N_EXPERTS = 16
N_EGROUPS = 4
EXP_PER_GROUP = N_EXPERTS // N_EGROUPS
TOP_K = 2
D_FF_EXPERT = D_MODEL // 2

PLE_DIM = 256
DN_ALPHA = (2 * DEPTH) ** 0.25
DN_BETA = (8 * DEPTH) ** -0.25
NORM_EPS = 1e-5

kernel_name = "hybrid_pool_diffattn_ssd_moe_step"


def layer_norm(x, g, b):
    xf = x.astype(jnp.float32)
    mu = jnp.mean(xf, axis=-1, keepdims=True)
    var = jnp.mean(jnp.square(xf - mu), axis=-1, keepdims=True)
    return ((xf - mu) * lax.rsqrt(var + NORM_EPS) * g + b).astype(x.dtype)


def group_rms_norm(x, g, n_groups):
    shp = x.shape
    xg = x.reshape(shp[:-1] + (n_groups, shp[-1] // n_groups))
    xg = xg * lax.rsqrt(jnp.mean(jnp.square(xg), axis=-1, keepdims=True) + NORM_EPS)
    return xg.reshape(shp) * g


def rope(x, pos):
    d = x.shape[-1]
    half = d // 2
    inv = ROPE_THETA ** (-jnp.arange(half, dtype=jnp.float32) * 2.0 / d)
    ang = pos.astype(jnp.float32)[:, None] * inv[None, :]
    bshape = (1, pos.shape[0]) + (1,) * (x.ndim - 3) + (half,)
    cos = jnp.cos(ang).reshape(bshape)
    sin = jnp.sin(ang).reshape(bshape)
    xf = x.astype(jnp.float32)
    x1, x2 = xf[..., :half], xf[..., half:]
    return jnp.concatenate([x1 * cos - x2 * sin, x2 * cos + x1 * sin], axis=-1).astype(x.dtype)


def pool_mixer(x, prefix, pos, w_in, w_grp, scale, w_out):
    b, l, _ = x.shape
    u = x @ w_in
    ext = jnp.concatenate([prefix.astype(u.dtype), u], axis=1)
    cs = jnp.pad(jnp.cumsum(ext.astype(jnp.float32), axis=1), ((0, 0), (1, 0), (0, 0)))
    end = cs[:, POOL_BUF + 1:]
    uf = u.astype(jnp.float32)
    groups = []
    for gi, w in enumerate(POOL_WINDOWS):
        c = slice(gi * POOL_GD, (gi + 1) * POOL_GD)
        start = cs[:, POOL_BUF + 1 - w: POOL_BUF + 1 - w + l, c]
        cnt = jnp.minimum(w, pos + 1).astype(jnp.float32)[None, :, None]
        groups.append((end[..., c] - start) / cnt - uf[..., c])
    pooled = jnp.stack(groups, axis=2).astype(x.dtype)
    mixed = jnp.einsum('blgc,gce->blge', pooled, w_grp).reshape(b, l, D_MODEL) * scale
    return mixed @ w_out, ext[:, -POOL_BUF:]


def diff_qkv(x, w_in, pos):
    b, l, _ = x.shape
    qkv = x @ w_in
    q = rope(qkv[..., :D_MODEL].reshape(b, l, DIFF_HEADS, 2, DIFF_DH), pos)
    k = rope(qkv[..., D_MODEL:2 * D_MODEL].reshape(b, l, DIFF_HEADS, 2, DIFF_DH), pos)
    v = qkv[..., 2 * D_MODEL:].reshape(b, l, DIFF_HEADS, DIFF_VD)
    return q, k, v


def diff_lambda(lq, lk, lam_init):
    dots = jnp.sum(lq.astype(jnp.float32) * lk.astype(jnp.float32), axis=-1)
    return jnp.exp(dots[0]) - jnp.exp(dots[1]) + lam_init


def diff_combine(s, lam):
    p = jax.nn.softmax(s, axis=-1)
    return p[:, :, 0] - lam * p[:, :, 1]


def diff_attn_prompt(q, k, v, lam):
    b, s = q.shape[:2]
    nb = s // Q_BLOCK
    qb = q.reshape(b, nb, Q_BLOCK, DIFF_HEADS, 2, DIFF_DH).swapaxes(0, 1)
    kpos = jnp.arange(s)
    qpos = kpos.reshape(nb, Q_BLOCK)
    scale = DIFF_DH ** -0.5

    def block(args):
        qblk, qp = args
        sc = jnp.einsum('bqhcd,bkhcd->bhcqk', qblk, k).astype(jnp.float32) * scale
        sc = jnp.where(kpos[None, :] <= qp[:, None], sc, -jnp.inf)
        a = diff_combine(sc, lam)
        return jnp.einsum('bhqk,bkhe->bqhe', a.astype(v.dtype), v)

    o = lax.map(block, (qb, qpos))
    return o.swapaxes(0, 1).reshape(b, s, DIFF_HEADS, DIFF_VD)


def diff_attn_sample(q, k, v, k_past, v_past, lam):
    t = q.shape[1]
    n_past = k_past.shape[1]
    scale = DIFF_DH ** -0.5
    s_past = jnp.einsum('bqhcd,bkhcd->bhcqk', q, k_past).astype(jnp.float32) * scale
    s_new = jnp.einsum('bqhcd,bkhcd->bhcqk', q, k).astype(jnp.float32) * scale
    s_new = jnp.where(jnp.tril(jnp.ones((t, t), bool)), s_new, -jnp.inf)
    a = diff_combine(jnp.concatenate([s_past, s_new], axis=-1), lam).astype(v.dtype)
    return (jnp.einsum('bhqk,bkhe->bqhe', a[..., :n_past], v_past)
            + jnp.einsum('bhqk,bkhe->bqhe', a[..., n_past:], v))


def diff_out(o, norm_g, lam_init, w_out):
    b, l = o.shape[:2]
    of = o.astype(jnp.float32)
    of = of * lax.rsqrt(jnp.mean(jnp.square(of), axis=-1, keepdims=True) + NORM_EPS) * norm_g * (1.0 - lam_init)
    return of.reshape(b, l, D_MODEL).astype(o.dtype) @ w_out


def causal_conv(u, prefix, w, bias):
    l = u.shape[1]
    ext = jnp.concatenate([prefix.astype(u.dtype), u], axis=1)
    y = sum(ext[:, kk:kk + l] * w[kk] for kk in range(SSD_CONV)) + bias
    return y, ext[:, -(SSD_CONV - 1):]


def ssd_scan(x, dt, a, bm, cm, h0, chunk):
    b, l, h, p = x.shape
    g, n = bm.shape[2], bm.shape[3]
    hg = h // g
    nc = l // chunk
    x = x.astype(jnp.float32).reshape(b, nc, chunk, g, hg, p)
    dt = dt.reshape(b, nc, chunk, g, hg)
    bm = bm.astype(jnp.float32).reshape(b, nc, chunk, g, n)
    cm = cm.astype(jnp.float32).reshape(b, nc, chunk, g, n)
    acum = jnp.cumsum(dt * a.reshape(g, hg), axis=2)
    seg = acum[:, :, :, None] - acum[:, :, None, :]
    causal = jnp.tril(jnp.ones((chunk, chunk), bool))[:, :, None, None]
    decay = jnp.exp(jnp.where(causal, seg, -jnp.inf))
    cb = jnp.einsum('bcign,bcjgn->bcijg', cm, bm)
    wts = cb[..., None] * decay * dt[:, :, None]
    y_diag = jnp.einsum('bcijgh,bcjghp->bcighp', wts, x)
    dte = jnp.exp(acum[:, :, -1:] - acum) * dt
    states = jnp.einsum('bcjgn,bcjgh,bcjghp->bcghpn', bm, dte, x)
    chunk_decay = jnp.exp(acum[:, :, -1])

    def step(hc, inp):
        st, dc = inp
        return dc[..., None, None] * hc + st, hc

    h_last, h_prev = lax.scan(step, h0.astype(jnp.float32).reshape(b, g, hg, p, n),
                              (states.swapaxes(0, 1), chunk_decay.swapaxes(0, 1)))
    h_prev = h_prev.swapaxes(0, 1)
    y_off = jnp.einsum('bcign,bcghpn,bcigh->bcighp', cm, h_prev, jnp.exp(acum))
    return (y_diag + y_off).reshape(b, l, h, p), h_last.reshape(b, h, p, n)


def ssd_mixer(x, conv_prefix, h0, w_in, conv_w, conv_b, dt_bias, a_log, d_skip, norm_g, w_out):
    b, l, _ = x.shape
    zxbcdt = x @ w_in
    z = zxbcdt[..., :SSD_DI]
    xbc = zxbcdt[..., SSD_DI:SSD_DI + SSD_CONV_DIM]
    dt_raw = zxbcdt[..., SSD_DI + SSD_CONV_DIM:]
    xbc, conv_new = causal_conv(xbc, conv_prefix, conv_w, conv_b)
    xbc = jax.nn.silu(xbc)
    xs = xbc[..., :SSD_DI].reshape(b, l, SSD_H, SSD_P)
    bm = xbc[..., SSD_DI:SSD_DI + SSD_G * SSD_N].reshape(b, l, SSD_G, SSD_N)
    cm = xbc[..., SSD_DI + SSD_G * SSD_N:].reshape(b, l, SSD_G, SSD_N)
    dt = jax.nn.softplus(dt_raw.astype(jnp.float32) + dt_bias.astype(jnp.float32))
    a = -jnp.exp(a_log.astype(jnp.float32))
    chunk = SSD_CHUNK if l % SSD_CHUNK == 0 else l
    y, h_new = ssd_scan(xs, dt, a, bm, cm, h0, chunk)
    y = y + xs.astype(jnp.float32) * d_skip.astype(jnp.float32)[:, None]
    y = y.reshape(b, l, SSD_DI) * jax.nn.silu(z.astype(jnp.float32))
    y = group_rms_norm(y, norm_g, SSD_G)
    return y.astype(x.dtype) @ w_out, h_new, conv_new


def moe_ffn(x, w_router, b_router, w_gate, w_up, w_down):
    b, l, dm = x.shape
    t = x.reshape(-1, dm)
    probs = jax.nn.softmax((t @ w_router).astype(jnp.float32), axis=-1)
    sel = (probs + b_router.astype(jnp.float32)).reshape(-1, N_EGROUPS, EXP_PER_GROUP)
    grp_score = lax.top_k(sel, TOP_K)[0].sum(-1)
    grp = jnp.argmax(grp_score, axis=-1)
    in_grp = jnp.einsum('tge,tg->te', sel, jax.nn.one_hot(grp, N_EGROUPS, dtype=jnp.float32))
    _, loc = lax.top_k(in_grp, TOP_K)
    eid = grp[:, None] * EXP_PER_GROUP + loc
    gw = jnp.take_along_axis(probs, eid, axis=-1)
    gw = gw / jnp.sum(gw, axis=-1, keepdims=True)
    gate = jnp.einsum('tk,tke->te', gw, jax.nn.one_hot(eid, N_EXPERTS, dtype=jnp.float32))
    h = jax.nn.silu(jnp.einsum('td,edf->tef', t, w_gate)) * jnp.einsum('td,edf->tef', t, w_up)
    y = jnp.einsum('tef,efd->td', h * gate[:, :, None].astype(h.dtype), w_down)
    return y.reshape(b, l, dm)


def setup_inputs(seed: int = 0) -> dict:
    key = jax.random.key(seed)
    ks = iter(jax.random.split(key, 64))

    def nrm(shape, s=1.0):
        return jax.random.normal(next(ks), shape, jnp.float32) * s

    n_pages = PAST_LEN // PAGE_SIZE
    n_used = DEC_BATCH * n_pages
    n_phys = n_used + n_used // 4
    page_table = jax.random.permutation(next(ks), n_phys)[:n_used].reshape(DEC_BATCH, n_pages).astype(jnp.int32)
    dt = jnp.exp(jax.random.uniform(next(ks), (N_SSD, SSD_H)) * (math.log(0.1) - math.log(0.001)) + math.log(0.001))
    dt_bias = dt + jnp.log(-jnp.expm1(-dt))
    a_log = jnp.log(jax.random.uniform(next(ks), (N_SSD, SSD_H), minval=1.0, maxval=16.0))
    dm = D_MODEL ** -0.5
    return {
        "x_prompt": nrm((BATCH, SEQ, D_MODEL)),
        "x_sample": nrm((DEC_BATCH, DEC_SEQ, D_MODEL)),
        "p_prompt": nrm((DEPTH, BATCH, SEQ, PLE_DIM)),
        "p_sample": nrm((DEPTH, DEC_BATCH, DEC_SEQ, PLE_DIM)),
        "state_pool": nrm((N_POOL, DEC_BATCH, POOL_BUF, D_MODEL)),
        "cache_k": nrm((N_DIFF, n_phys, PAGE_SIZE, DIFF_HEADS, 2, DIFF_DH)),
        "cache_v": nrm((N_DIFF, n_phys, PAGE_SIZE, DIFF_HEADS, DIFF_VD)),
        "page_table": page_table,
        "state_ssm": nrm((N_SSD, DEC_BATCH, SSD_H, SSD_P, SSD_N), 0.1),
        "state_conv": nrm((N_SSD, DEC_BATCH, SSD_CONV - 1, SSD_CONV_DIM)),
        "ln_g": 1.0 + nrm((DEPTH, 2, D_MODEL), 0.1),
        "ln_b": nrm((DEPTH, 2, D_MODEL), 0.1),
        "w_pool_in": nrm((N_POOL, D_MODEL, D_MODEL), dm),
        "w_pool_grp": nrm((N_POOL, POOL_GROUPS, POOL_GD, POOL_GD), POOL_GD ** -0.5),
        "pool_scale": 1.0 + nrm((N_POOL, D_MODEL), 0.1),
        "w_pool_out": nrm((N_POOL, D_MODEL, D_MODEL), dm * DN_BETA),
        "w_diff_in": nrm((N_DIFF, D_MODEL, 3 * D_MODEL), dm),
        "diff_lambda_q": nrm((N_DIFF, 2, DIFF_DH), 0.1),
        "diff_lambda_k": nrm((N_DIFF, 2, DIFF_DH), 0.1),
        "diff_norm_g": 1.0 + nrm((N_DIFF, DIFF_VD), 0.1),
        "w_diff_out": nrm((N_DIFF, D_MODEL, D_MODEL), dm * DN_BETA),
        "w_ssd_in": nrm((N_SSD, D_MODEL, SSD_IN), dm),
        "ssd_conv_w": nrm((N_SSD, SSD_CONV, SSD_CONV_DIM), 0.5),
        "ssd_conv_b": nrm((N_SSD, SSD_CONV_DIM), 0.1),
        "ssd_dt_bias": dt_bias,
        "ssd_a_log": a_log,
        "ssd_d": 1.0 + nrm((N_SSD, SSD_H), 0.1),
        "ssd_norm_g": 1.0 + nrm((N_SSD, SSD_DI), 0.1),
        "w_ssd_out": nrm((N_SSD, SSD_DI, D_MODEL), SSD_DI ** -0.5 * DN_BETA),
        "w_router": nrm((D_MODEL, N_EXPERTS), dm),
        "b_router": nrm((N_EXPERTS,), 0.01),
        "w_exp_gate": nrm((DEPTH, N_EXPERTS, D_MODEL, D_FF_EXPERT), dm),
        "w_exp_up": nrm((DEPTH, N_EXPERTS, D_MODEL, D_FF_EXPERT), dm),
        "w_exp_down": nrm((DEPTH, N_EXPERTS, D_FF_EXPERT, D_MODEL), D_FF_EXPERT ** -0.5 * DN_BETA),
        "w_ple": nrm((DEPTH, PLE_DIM, D_MODEL), PLE_DIM ** -0.5),
        "w_ple_gate": nrm((DEPTH, D_MODEL, D_MODEL), dm),
        "b_ple_gate": nrm((DEPTH, D_MODEL), 0.1),
    }


def reference(x_prompt, x_sample, p_prompt, p_sample, state_pool, cache_k, cache_v, page_table, state_ssm, state_conv,
              ln_g, ln_b, w_pool_in, w_pool_grp, pool_scale, w_pool_out, w_diff_in, diff_lambda_q, diff_lambda_k,
              diff_norm_g, w_diff_out, w_ssd_in, ssd_conv_w, ssd_conv_b, ssd_dt_bias, ssd_a_log, ssd_d, ssd_norm_g,
              w_ssd_out, w_router, b_router, w_exp_gate, w_exp_up, w_exp_down, w_ple, w_ple_gate, b_ple_gate):
    bp, seq = x_prompt.shape[:2]
    bs, tdec = x_sample.shape[:2]
    past = page_table.shape[1] * cache_k.shape[2]
    pos_p = jnp.arange(seq, dtype=jnp.int32)
    pos_s = past + jnp.arange(tdec, dtype=jnp.int32)

    def finish(x, mix, i, p):
        x = layer_norm(DN_ALPHA * x + mix, ln_g[i, 0], ln_b[i, 0])
        ffn = moe_ffn(x, w_router, b_router, w_exp_gate[i], w_exp_up[i], w_exp_down[i])
        x = layer_norm(DN_ALPHA * x + ffn, ln_g[i, 1], ln_b[i, 1])
        return x + jax.nn.sigmoid(x @ w_ple_gate[i] + b_ple_gate[i]) * (p @ w_ple[i])

    xp, xs = x_prompt, x_sample
    pool_p, pool_s, k_p, v_p, k_s, v_s, ssm_p, conv_p, ssm_s, conv_s = ([] for _ in range(10))
    for i in range(DEPTH):
        kind, j = i % N_MIXERS, i // N_MIXERS
        if kind == 0:
            wts = (w_pool_in[j], w_pool_grp[j], pool_scale[j], w_pool_out[j])
            mp, sp = pool_mixer(xp, jnp.zeros((bp, POOL_BUF, D_MODEL), xp.dtype), pos_p, *wts)
            ms, ss = pool_mixer(xs, state_pool[j], pos_s, *wts)
            pool_p.append(sp)
            pool_s.append(ss)
        elif kind == 1:
            lam_init = 0.8 - 0.6 * math.exp(-0.3 * i)
            lam = diff_lambda(diff_lambda_q[j], diff_lambda_k[j], lam_init)
            q, k, v = diff_qkv(xp, w_diff_in[j], pos_p)
            mp = diff_out(diff_attn_prompt(q, k, v, lam), diff_norm_g[j], lam_init, w_diff_out[j])
            k_p.append(k)
            v_p.append(v)
            q, k, v = diff_qkv(xs, w_diff_in[j], pos_s)
            k_past = cache_k[j][page_table].reshape(bs, past, DIFF_HEADS, 2, DIFF_DH)
            v_past = cache_v[j][page_table].reshape(bs, past, DIFF_HEADS, DIFF_VD)
            ms = diff_out(diff_attn_sample(q, k, v, k_past, v_past, lam), diff_norm_g[j], lam_init, w_diff_out[j])
            k_s.append(k)
            v_s.append(v)
        else:
            wts = (w_ssd_in[j], ssd_conv_w[j], ssd_conv_b[j], ssd_dt_bias[j], ssd_a_log[j], ssd_d[j],
                   ssd_norm_g[j], w_ssd_out[j])
            mp, hp, cp = ssd_mixer(xp, jnp.zeros((bp, SSD_CONV - 1, SSD_CONV_DIM), xp.dtype),
                                   jnp.zeros((bp, SSD_H, SSD_P, SSD_N), jnp.float32), *wts)
            ms, hs, cs = ssd_mixer(xs, state_conv[j], state_ssm[j], *wts)
            ssm_p.append(hp.astype(xp.dtype))
            conv_p.append(cp)
            ssm_s.append(hs.astype(xs.dtype))
            conv_s.append(cs)
        xp = finish(xp, mp, i, p_prompt[i])
        xs = finish(xs, ms, i, p_sample[i])

    return (xp, xs, jnp.stack(pool_p), jnp.stack(pool_s), jnp.stack(k_p), jnp.stack(v_p), jnp.stack(k_s),
            jnp.stack(v_s), jnp.stack(ssm_p), jnp.stack(conv_p), jnp.stack(ssm_s), jnp.stack(conv_s))
```

```python
import functools
import math

import jax
import jax.numpy as jnp
from jax import lax
from jax.experimental import pallas as pl
from jax.experimental.pallas import tpu as pltpu

F32 = jnp.float32
BF16 = jnp.bfloat16

D_MODEL = 1024
DEPTH = 4
N_MIXERS = 3
PAGE_SIZE = 128

POOL_WINDOWS = (2, 4, 8, 16)
POOL_GD = D_MODEL // len(POOL_WINDOWS)
POOL_BUF = max(POOL_WINDOWS) - 1

DIFF_DH = 64
DIFF_VD = 2 * DIFF_DH
DIFF_HEADS = D_MODEL // DIFF_VD
ROPE_THETA = 10000.0

SSD_DI = 2 * D_MODEL
SSD_P = 64
SSD_H = SSD_DI // SSD_P
SSD_G = 4
SSD_N = 128
SSD_CONV = 4
SSD_CONV_DIM = SSD_DI + 2 * SSD_G * SSD_N
SSD_CHUNK = 128

N_EXPERTS = 16
N_EGROUPS = 4
EXP_PER_GROUP = N_EXPERTS // N_EGROUPS
D_FF_EXPERT = D_MODEL // 2

PLE_DIM = 256
DN_ALPHA = (2 * DEPTH) ** 0.25
NORM_EPS = 1e-5

LANES = 128
SUBLANES = 8
VMEM_LIMIT = 56 * 1024 * 1024


def _params(*sem):
    return pltpu.CompilerParams(dimension_semantics=sem, vmem_limit_bytes=VMEM_LIMIT)


def _bdot(a, b):
    return jnp.dot(a.astype(BF16), b.astype(BF16), preferred_element_type=F32)


def _bdot_nt(a, b):
    return lax.dot_general(a.astype(BF16), b.astype(BF16), (((1,), (1,)), ((), ())),
                           preferred_element_type=F32)


def _split3(x):
    hi = x.astype(BF16)
    r = x - hi.astype(F32)
    mid = r.astype(BF16)
    lo = (r - mid.astype(F32)).astype(BF16)
    return hi, mid, lo


def _dot_exact_rhs(x, m):
    hi, mid, lo = _split3(x)
    return (jnp.dot(hi, m, preferred_element_type=F32) + jnp.dot(mid, m, preferred_element_type=F32)
            + jnp.dot(lo, m, preferred_element_type=F32))


def _dot_exact_lhs(m, x):
    hi, mid, lo = _split3(x)
    return (jnp.dot(m, hi, preferred_element_type=F32) + jnp.dot(m, mid, preferred_element_type=F32)
            + jnp.dot(m, lo, preferred_element_type=F32))


def _ln(v, g, b):
    mu = jnp.mean(v, axis=-1, keepdims=True)
    c = v - mu
    var = jnp.mean(c * c, axis=-1, keepdims=True)
    return c * lax.rsqrt(var + NORM_EPS) * g + b


def _silu(v):
    return v * jax.nn.sigmoid(v)


def _softplus(v):
    return jnp.maximum(v, 0.0) + jnp.log(1.0 + jnp.exp(-jnp.abs(v)))


def _pool_kernel(x_ref, pre_ref, win_ref, wgrp_ref, scale_ref, wout_ref, g_ref, b_ref,
                 y_ref, newp_ref, ext_ref, *, tm, stride, pos0, n_tiles):
    i = pl.program_id(1)
    pre_rows = (POOL_BUF + 1) * stride

    @pl.when(i == 0)
    def _():
        ext_ref[0:pre_rows, :] = pre_ref[0]

    @pl.when(i > 0)
    def _():
        ext_ref[0:pre_rows, :] = ext_ref[tm:tm + pre_rows, :]

    x = x_ref[0]
    u = _bdot(x, win_ref[...])
    ext_ref[pre_rows:pre_rows + tm, :] = u
    row = lax.broadcasted_iota(jnp.int32, (tm, 1), 0)
    pos = pos0 + (i * tm + row) // stride
    mixed = []
    for gi, w in enumerate(POOL_WINDOWS):
        c0 = gi * POOL_GD
        ug = u[:, c0:c0 + POOL_GD]
        s = ug
        for k in range(1, w):
            r0 = pre_rows - k * stride
            s = s + ext_ref[r0:r0 + tm, c0:c0 + POOL_GD]
        cnt = jnp.minimum(w, pos + 1).astype(F32)
        pooled = s / cnt - ug
        mixed.append(_bdot(pooled, wgrp_ref[gi]))
    mixed = jnp.concatenate(mixed, axis=1) * scale_ref[...]
    mix = _bdot(mixed, wout_ref[...])
    y_ref[0] = _ln(DN_ALPHA * x + mix, g_ref[...], b_ref[...])

    @pl.when(i == n_tiles - 1)
    def _():
        newp_ref[0] = ext_ref[tm + stride:tm + pre_rows, :]


def _pool_layer(x, prefix, w_in, w_grp, scale, w_out, g, b, *, tm, stride, pos0):
    bsz, rows, d = x.shape
    n_tiles = rows // tm
    pre_rows = (POOL_BUF + 1) * stride
    kern = functools.partial(_pool_kernel, tm=tm, stride=stride, pos0=pos0, n_tiles=n_tiles)
    full = lambda *shape: pl.BlockSpec(shape, lambda bi, i: (0,) * len(shape))
    return pl.pallas_call(
        kern,
        grid=(bsz, n_tiles),
        in_specs=[
            pl.BlockSpec((1, tm, d), lambda bi, i: (bi, i, 0)),
            pl.BlockSpec((1, pre_rows, d), lambda bi, i: (bi, 0, 0)),
            full(d, d), full(len(POOL_WINDOWS), POOL_GD, POOL_GD), full(1, d), full(d, d),
            full(1, d), full(1, d),
        ],
        out_specs=[
            pl.BlockSpec((1, tm, d), lambda bi, i: (bi, i, 0)),
            pl.BlockSpec((1, POOL_BUF * stride, d), lambda bi, i: (bi, 0, 0)),
        ],
        out_shape=[jax.ShapeDtypeStruct((bsz, rows, d), F32),
                   jax.ShapeDtypeStruct((bsz, POOL_BUF * stride, d), F32)],
        scratch_shapes=[pltpu.VMEM((pre_rows + tm, d), F32)],
        compiler_params=_params("arbitrary", "arbitrary"),
        name="pool_layer",
    )(x, prefix, w_in, w_grp, scale, w_out, g, b)


def _mm_kernel(x_ref, w_ref, o_ref):
    o_ref[...] = _bdot(x_ref[...], w_ref[...]).astype(o_ref.dtype)


def _matmul(x, w, *, tm, tn, out_dtype=F32):
    t, k = x.shape
    n = w.shape[1]
    return pl.pallas_call(
        _mm_kernel,
        grid=(n // tn, t // tm),
        in_specs=[pl.BlockSpec((tm, k), lambda j, i: (i, 0)),
                  pl.BlockSpec((k, tn), lambda j, i: (0, j))],
        out_specs=pl.BlockSpec((tm, tn), lambda j, i: (i, j)),
        out_shape=jax.ShapeDtypeStruct((t, n), out_dtype),
        compiler_params=_params("arbitrary", "arbitrary"),
        name="matmul",
    )(x, w)


def _mm_res_ln_kernel(a_ref, w_ref, x_ref, g_ref, b_ref, o_ref):
    mix = _bdot(a_ref[...], w_ref[...])
    o_ref[...] = _ln(DN_ALPHA * x_ref[...] + mix, g_ref[...], b_ref[...])


def _mm_res_ln(a, w, x, g, b, *, tm):
    t, k = a.shape
    d = w.shape[1]
    return pl.pallas_call(
        _mm_res_ln_kernel,
        grid=(t // tm,),
        in_specs=[pl.BlockSpec((tm, k), lambda i: (i, 0)),
                  pl.BlockSpec((k, d), lambda i: (0, 0)),
                  pl.BlockSpec((tm, d), lambda i: (i, 0)),
                  pl.BlockSpec((1, d), lambda i: (0, 0)),
                  pl.BlockSpec((1, d), lambda i: (0, 0))],
        out_specs=pl.BlockSpec((tm, d), lambda i: (i, 0)),
        out_shape=jax.ShapeDtypeStruct((t, d), F32),
        compiler_params=_params("arbitrary"),
        name="mm_res_ln",
    )(a, w, x, g, b)


def _qkv_kernel(x_ref, w_ref, cos_ref, sin_ref, q_ref, k_ref, v_ref, *, tm):
    x = x_ref[0].astype(BF16)
    cos = cos_ref[...]
    sin = sin_ref[...]
    lane = lax.broadcasted_iota(jnp.int32, (tm, LANES), 1)
    first_half = (lane % DIFF_DH) < (DIFF_DH // 2)

    def rope(y):
        outs = []
        for c in range(D_MODEL // LANES):
            yc = y[:, c * LANES:(c + 1) * LANES]
            sw = jnp.where(first_half, pltpu.roll(yc, LANES - DIFF_DH // 2, 1),
                           pltpu.roll(yc, DIFF_DH // 2, 1))
            outs.append(yc * cos + sw * sin)
        return jnp.concatenate(outs, axis=1)

    q_ref[0] = rope(jnp.dot(x, w_ref[:, 0:D_MODEL], preferred_element_type=F32))
    k_ref[0] = rope(jnp.dot(x, w_ref[:, D_MODEL:2 * D_MODEL], preferred_element_type=F32))
    v_ref[0] = jnp.dot(x, w_ref[:, 2 * D_MODEL:3 * D_MODEL], preferred_element_type=F32)


def _rope_tables(pos):
    half = DIFF_DH // 2
    inv = ROPE_THETA ** (-jnp.arange(half, dtype=F32) * 2.0 / DIFF_DH)
    ang = pos.astype(F32)[:, None] * inv[None, :]
    cos, sin = jnp.cos(ang), jnp.sin(ang)
    reps = LANES // DIFF_DH
    cos_t = jnp.tile(jnp.concatenate([cos, cos], axis=1), (1, reps))
    sin_t = jnp.tile(jnp.concatenate([-sin, sin], axis=1), (1, reps))
    return cos_t, sin_t


def _qkv_rope(x, w, cos_t, sin_t, *, tm):
    bsz, rows, d = x.shape
    kern = functools.partial(_qkv_kernel, tm=tm)
    spec = pl.BlockSpec((1, tm, d), lambda bi, i: (bi, i, 0))
    tab = pl.BlockSpec((tm, LANES), lambda bi, i: (i, 0))
    shp = jax.ShapeDtypeStruct((bsz, rows, d), F32)
    return pl.pallas_call(
        kern,
        grid=(bsz, rows // tm),
        in_specs=[spec, pl.BlockSpec((d, 3 * d), lambda bi, i: (0, 0)), tab, tab],
        out_specs=[spec, spec, spec],
        out_shape=[shp, shp, shp],
        compiler_params=_params("arbitrary", "arbitrary"),
        name="qkv_rope",
    )(x, w, cos_t, sin_t)


def _diff_lambda(lq_ref, lk_ref, lam_init):
    dots = jnp.sum(lq_ref[...] * lk_ref[...], axis=-1, keepdims=True)
    e = jnp.exp(dots)
    return e[0:1, :] - e[1:2, :] + lam_init


def _head_norm(o, ng, lam_init):
    ms = jnp.mean(o * o, axis=-1, keepdims=True)
    return o * lax.rsqrt(ms + NORM_EPS) * ng * (1.0 - lam_init)


def _attn_kernel(q_ref, k_ref, v_ref, lq_ref, lk_ref, ng_ref, o_ref, *, tq, lam_init):
    qi = pl.program_id(2)
    scale = DIFF_DH ** -0.5
    q = q_ref[0]
    q1 = q[:, :DIFF_DH].astype(BF16)
    q2 = q[:, DIFF_DH:].astype(BF16)
    rowpos = qi * tq + lax.broadcasted_iota(jnp.int32, (tq, tq), 0)
    col = lax.broadcasted_iota(jnp.int32, (tq, tq), 1)

    def update(qc, kc, v, mask, m, l, a):
        s = _bdot_nt(qc, kc) * scale
        s = jnp.where(mask, s, -jnp.inf)
        mn = jnp.maximum(m, jnp.max(s, axis=1, keepdims=True))
        alpha = jnp.exp(m - mn)
        p = jnp.exp(s - mn)
        l = alpha * l + jnp.sum(p, axis=1, keepdims=True)
        a = alpha * a + jnp.dot(p.astype(BF16), v, preferred_element_type=F32)
        return mn, l, a

    def body(kb, carry):
        m1, l1, a1, m2, l2, a2 = carry
        start = pl.multiple_of(kb * tq, tq)
        k = k_ref[0, pl.ds(start, tq), :]
        v = v_ref[0, pl.ds(start, tq), :].astype(BF16)
        mask = (kb * tq + col) <= rowpos
        m1, l1, a1 = update(q1, k[:, :DIFF_DH], v, mask, m1, l1, a1)
        m2, l2, a2 = update(q2, k[:, DIFF_DH:], v, mask, m2, l2, a2)
        return m1, l1, a1, m2, l2, a2

    neg = jnp.full((tq, 1), -jnp.inf, F32)
    zero = jnp.zeros((tq, 1), F32)
    zacc = jnp.zeros((tq, DIFF_VD), F32)
    m1, l1, a1, m2, l2, a2 = lax.fori_loop(0, qi + 1, body, (neg, zero, zacc, neg, zero, zacc))
    lam = _diff_lambda(lq_ref, lk_ref, lam_init)
    o = a1 / l1 - lam * (a2 / l2)
    o_ref[0] = _head_norm(o, ng_ref[...], lam_init).astype(o_ref.dtype)


def _diff_attn_prompt(q, k, v, lq, lk, ng, lam_init, *, tq):
    bsz, seq, d = q.shape
    kern = functools.partial(_attn_kernel, tq=tq, lam_init=lam_init)
    small = lambda *shape: pl.BlockSpec(shape, lambda bi, h, i: (0,) * len(shape))
    return pl.pallas_call(
        kern,
        grid=(bsz, DIFF_HEADS, seq // tq),
        in_specs=[pl.BlockSpec((1, tq, DIFF_VD), lambda bi, h, i: (bi, i, h)),
                  pl.BlockSpec((1, seq, DIFF_VD), lambda bi, h, i: (bi, 0, h)),
                  pl.BlockSpec((1, seq, DIFF_VD), lambda bi, h, i: (bi, 0, h)),
                  small(2, DIFF_DH), small(2, DIFF_DH), small(1, DIFF_VD)],
        out_specs=pl.BlockSpec((1, tq, DIFF_VD), lambda bi, h, i: (bi, i, h)),
        out_shape=jax.ShapeDtypeStruct((bsz, seq, d), BF16),
        compiler_params=_params("arbitrary", "arbitrary", "arbitrary"),
        name="diff_attn_prompt",
    )(q, k, v, lq, lk, ng)


DEC_TPAD = SUBLANES
DEC_ROWS = DIFF_HEADS * 2 * DEC_TPAD


def _decode_kernel(pt_ref, q_ref, *refs, pages_per_step, n_steps, lam_init):
    g = pages_per_step
    k_refs = refs[:g]
    v_refs = refs[g:2 * g]
    knew_ref, vnew_ref, lq_ref, lk_ref, ng_ref, o_ref, m_ref, l_ref, acc_ref = refs[2 * g:]
    s = pl.program_id(1)
    scale = DIFF_DH ** -0.5

    @pl.when(s == 0)
    def _():
        m_ref[...] = jnp.full(m_ref.shape, -jnp.inf, F32)
        l_ref[...] = jnp.zeros(l_ref.shape, F32)
        acc_ref[...] = jnp.zeros(acc_ref.shape, F32)

    q = q_ref[0].astype(BF16)

    def block(k, v, mask):
        sc = _bdot_nt(q, k) * scale
        if mask is not None:
            sc = jnp.where(mask, sc, -jnp.inf)
        m_old = m_ref[...]
        mn = jnp.maximum(m_old, jnp.max(sc, axis=1, keepdims=True))
        alpha = jnp.exp(m_old - mn)
        p = jnp.exp(sc - mn)
        l_ref[...] = alpha * l_ref[...] + jnp.sum(p, axis=1, keepdims=True)
        acc_ref[...] = alpha * acc_ref[...] + _bdot(p, v)
        m_ref[...] = mn

    for gi in range(g):
        block(k_refs[gi][0], v_refs[gi][0], None)

    @pl.when(s == n_steps - 1)
    def _():
        t_row = lax.broadcasted_iota(jnp.int32, (DEC_ROWS, PAGE_SIZE), 0) % DEC_TPAD
        key = lax.broadcasted_iota(jnp.int32, (DEC_ROWS, PAGE_SIZE), 1)
        block(knew_ref[0], vnew_ref[0], key <= t_row)
        lam = _diff_lambda(lq_ref, lk_ref, lam_init)
        ng = ng_ref[...]
        for h in range(DIFF_HEADS):
            r1 = h * 2 * DEC_TPAD
            r2 = r1 + DEC_TPAD
            cols = slice(h * DIFF_VD, (h + 1) * DIFF_VD)
            a1 = acc_ref[r1:r1 + DEC_TPAD, cols] / l_ref[r1:r1 + DEC_TPAD, :]
            a2 = acc_ref[r2:r2 + DEC_TPAD, cols] / l_ref[r2:r2 + DEC_TPAD, :]
            o_ref[0, :, cols] = _head_norm(a1 - lam * a2, ng, lam_init).astype(o_ref.dtype)


def _diff_attn_sample(qbd, cache_k, cache_v, page_table, knew, vnew, lq, lk, ng, lam_init,
                      *, pages_per_step):
    bsz = qbd.shape[0]
    n_pages = page_table.shape[1]
    g = pages_per_step
    n_steps = n_pages // g
    kern = functools.partial(_decode_kernel, pages_per_step=g, n_steps=n_steps, lam_init=lam_init)

    def page_spec(gi):
        return pl.BlockSpec((1, PAGE_SIZE, D_MODEL), lambda b, s, pt: (pt[b, s * g + gi], 0, 0))

    per_b = lambda rows: pl.BlockSpec((1, rows, D_MODEL), lambda b, s, pt: (b, 0, 0))
    small = lambda *shape: pl.BlockSpec(shape, lambda b, s, pt: (0,) * len(shape))
    grid_spec = pltpu.PrefetchScalarGridSpec(
        num_scalar_prefetch=1,
        grid=(bsz, n_steps),
        in_specs=([per_b(DEC_ROWS)] + [page_spec(gi) for gi in range(g)] + [page_spec(gi) for gi in range(g)]
                  + [per_b(PAGE_SIZE), per_b(PAGE_SIZE), small(2, DIFF_DH), small(2, DIFF_DH), small(1, DIFF_VD)]),
        out_specs=per_b(DEC_TPAD),
        scratch_shapes=[pltpu.VMEM((DEC_ROWS, 1), F32), pltpu.VMEM((DEC_ROWS, 1), F32),
                        pltpu.VMEM((DEC_ROWS, D_MODEL), F32)],
    )
    return pl.pallas_call(
        kern,
        grid_spec=grid_spec,
        out_shape=jax.ShapeDtypeStruct((bsz, DEC_TPAD, D_MODEL), BF16),
        compiler_params=_params("arbitrary", "arbitrary"),
        name="diff_attn_sample",
    )(page_table, qbd, *([cache_k] * g), *([cache_v] * g), knew, vnew, lq, lk, ng)


CONV_PAD = SUBLANES


def _ssd_kernel(*refs, q, n_chunks, valid_len, has_h0):
    if has_h0:
        (z_ref, xbc_ref, dt_ref, dtt_ref, cpre_ref, cw_ref, cb_ref, dtb_row_ref, dtb_col_ref,
         alog_row_ref, alog_col_ref, dexp_ref, ng_ref, expand_ref, h0_ref,
         y_ref, hl_ref, s_ref, cext_ref) = refs
    else:
        (z_ref, xbc_ref, dt_ref, dtt_ref, cpre_ref, cw_ref, cb_ref, dtb_row_ref, dtb_col_ref,
         alog_row_ref, alog_col_ref, dexp_ref, ng_ref, expand_ref,
         y_ref, hl_ref, s_ref, cext_ref) = refs
        h0_ref = None
    c = pl.program_id(1)

    @pl.when(c == 0)
    def _():
        if has_h0:
            s_ref[...] = h0_ref[0]
        else:
            s_ref[...] = jnp.zeros(s_ref.shape, F32)
        cext_ref[0:CONV_PAD, :] = cpre_ref[0]

    @pl.when(c > 0)
    def _():
        cext_ref[0:CONV_PAD, :] = cext_ref[q:q + CONV_PAD, :]

    cext_ref[CONV_PAD:CONV_PAD + q, :] = xbc_ref[0]
    xbc = cb_ref[...]
    for kk in range(SSD_CONV):
        r0 = CONV_PAD - (SSD_CONV - 1) + kk
        xbc = xbc + cext_ref[r0:r0 + q, :] * cw_ref[kk:kk + 1, :]
    xbc = _silu(xbc)
    xs = xbc[:, :SSD_DI]
    bm = xbc[:, SSD_DI:SSD_DI + SSD_G * SSD_N]
    cm = xbc[:, SSD_DI + SSD_G * SSD_N:]

    dtc = _softplus(dt_ref[0] + dtb_row_ref[...])
    dtr = _softplus(dtt_ref[0] + dtb_col_ref[...])
    if valid_len is not None:
        t_col = c * q + lax.broadcasted_iota(jnp.int32, (q, LANES), 0)
        t_row = c * q + lax.broadcasted_iota(jnp.int32, (SSD_H, q), 1)
        dtc = jnp.where(t_col < valid_len, dtc, 0.0)
        dtr = jnp.where(t_row < valid_len, dtr, 0.0)
    ii = lax.broadcasted_iota(jnp.int32, (q, q), 0)
    jj = lax.broadcasted_iota(jnp.int32, (q, q), 1)
    causal = jj <= ii
    tri = jnp.where(causal, 1.0, 0.0).astype(BF16)
    tri_t = jnp.where(ii <= jj, 1.0, 0.0).astype(BF16)
    acum_c = _dot_exact_lhs(tri, dtc * (-jnp.exp(alog_row_ref[...])))
    acum_r = _dot_exact_rhs(dtr * (-jnp.exp(alog_col_ref[...])), tri_t)
    total = acum_c[q - 1:q, :]
    expand = expand_ref[...]
    dte_x = _dot_exact_rhs(jnp.exp(total - acum_c) * dtc, expand)
    eac_x = _dot_exact_rhs(jnp.exp(acum_c), expand)
    cdec_x = _dot_exact_rhs(jnp.broadcast_to(jnp.exp(total), (SUBLANES, LANES)), expand)[0:1, :]

    gw = SSD_DI // SSD_G
    hg = SSD_H // SSD_G
    lane = lax.broadcasted_iota(jnp.int32, (q, 2 * SSD_P), 1)
    ys = []
    for g in range(SSD_G):
        gc = slice(g * gw, (g + 1) * gw)
        bg = bm[:, g * SSD_N:(g + 1) * SSD_N]
        cg = cm[:, g * SSD_N:(g + 1) * SSD_N]
        cbm = _bdot_nt(cg, bg)
        xg = xs[:, gc]
        s_old = s_ref[:, gc]
        y_g = _bdot(cg, s_old) * eac_x[:, gc]
        s_ref[:, gc] = s_old * cdec_x[:, gc] + _bdot(bg.T, xg * dte_x[:, gc])
        pairs = []
        for hp in range(hg // 2):
            wts = []
            for hh in range(2):
                h = g * hg + 2 * hp + hh
                seg = acum_c[:, h:h + 1] - acum_r[h:h + 1, :]
                dec = jnp.exp(jnp.where(causal, seg, -jnp.inf))
                wts.append((cbm * dec * dtr[h:h + 1, :]).astype(BF16))
            xp = xg[:, hp * 2 * SSD_P:(hp + 1) * 2 * SSD_P]
            xbd = jnp.concatenate([jnp.where(lane < SSD_P, xp, 0.0), jnp.where(lane >= SSD_P, xp, 0.0)],
                                  axis=0).astype(BF16)
            pairs.append(jnp.dot(jnp.concatenate(wts, axis=1), xbd, preferred_element_type=F32))
        y_g = y_g + jnp.concatenate(pairs, axis=1) + xg * dexp_ref[:, gc]
        zg = z_ref[0, :, gc]
        y_g = y_g * _silu(zg)
        y_g = y_g * lax.rsqrt(jnp.mean(y_g * y_g, axis=-1, keepdims=True) + NORM_EPS)
        ys.append(y_g)
    y_ref[0] = (jnp.concatenate(ys, axis=1) * ng_ref[...]).astype(y_ref.dtype)

    @pl.when(c == n_chunks - 1)
    def _():
        hl_ref[0] = s_ref[...]


def _ssd_scan(z, xbc, dt, dtt, cpre, conv_w, conv_b, dtb_row, dtb_col, alog_row, alog_col, d_exp, ng,
              expand, h0t, *, valid_len):
    bsz, seq, _ = z.shape
    q = SSD_CHUNK
    n_chunks = seq // q
    has_h0 = h0t is not None
    kern = functools.partial(_ssd_kernel, q=q, n_chunks=n_chunks, valid_len=valid_len, has_h0=has_h0)
    full = lambda *shape: pl.BlockSpec(shape, lambda b, c: (0,) * len(shape))
    in_specs = [
        pl.BlockSpec((1, q, SSD_DI), lambda b, c: (b, c, 0)),
        pl.BlockSpec((1, q, SSD_CONV_DIM), lambda b, c: (b, c, 0)),
        pl.BlockSpec((1, q, LANES), lambda b, c: (b, c, 0)),
        pl.BlockSpec((1, SSD_H, q), lambda b, c: (b, 0, c)),
        pl.BlockSpec((1, CONV_PAD, SSD_CONV_DIM), lambda b, c: (b, 0, 0)),
        full(SSD_CONV, SSD_CONV_DIM), full(1, SSD_CONV_DIM),
        full(1, LANES), full(SSD_H, 1), full(1, LANES), full(SSD_H, 1),
        full(1, SSD_DI), full(1, SSD_DI), full(LANES, SSD_DI),
    ]
    args = [z, xbc, dt, dtt, cpre, conv_w, conv_b, dtb_row, dtb_col, alog_row, alog_col, d_exp, ng, expand]
    if has_h0:
        in_specs.append(pl.BlockSpec((1, SSD_N, SSD_DI), lambda b, c: (b, 0, 0)))
        args.append(h0t)
    return pl.pallas_call(
        kern,
        grid=(bsz, n_chunks),
        in_specs=in_specs,
        out_specs=[pl.BlockSpec((1, q, SSD_DI), lambda b, c: (b, c, 0)),
                   pl.BlockSpec((1, SSD_N, SSD_DI), lambda b, c: (b, 0, 0))],
        out_shape=[jax.ShapeDtypeStruct((bsz, seq, SSD_DI), BF16),
                   jax.ShapeDtypeStruct((bsz, SSD_N, SSD_DI), F32)],
        scratch_shapes=[pltpu.VMEM((SSD_N, SSD_DI), F32), pltpu.VMEM((CONV_PAD + q, SSD_CONV_DIM), F32)],
        compiler_params=_params("arbitrary", "arbitrary"),
        name="ssd_scan",
    )(*args)


def _first_max(vals):
    m = vals[0]
    for v in vals[1:]:
        m = jnp.maximum(m, v)
    flags = []
    taken = jnp.zeros_like(m)
    for v in vals:
        f = jnp.where(v == m, 1.0 - taken, 0.0)
        taken = taken + f
        flags.append(f)
    return m, flags


def _router_kernel(x_ref, wh_ref, wl_ref, b_ref, o_ref):
    x = x_ref[...]
    xh = x.astype(BF16)
    xl = (x - xh.astype(F32)).astype(BF16)
    logits = (jnp.dot(xh, wh_ref[...], preferred_element_type=F32)
              + jnp.dot(xh, wl_ref[...], preferred_element_type=F32)
              + jnp.dot(xl, wh_ref[...], preferred_element_type=F32))
    lt = logits.T
    rows = [lt[e:e + 1, :] for e in range(N_EXPERTS)]
    mx = rows[0]
    for r in rows[1:]:
        mx = jnp.maximum(mx, r)
    ex = [jnp.exp(r - mx) for r in rows]
    den = ex[0]
    for r in ex[1:]:
        den = den + r
    probs = [r / den for r in ex]
    sel = [probs[e] + b_ref[e:e + 1, :] for e in range(N_EXPERTS)]
    scores, chosen = [], []
    for g in range(N_EGROUPS):
        v = sel[g * EXP_PER_GROUP:(g + 1) * EXP_PER_GROUP]
        m1, f1 = _first_max(v)
        rest = [jnp.where(f > 0.0, -jnp.inf, vi) for f, vi in zip(f1, v)]
        m2, f2 = _first_max(rest)
        scores.append(m1 + m2)
        chosen.extend([a + b for a, b in zip(f1, f2)])
    _, gflag = _first_max(scores)
    gates = [probs[e] * chosen[e] * gflag[e // EXP_PER_GROUP] for e in range(N_EXPERTS)]
    tot = gates[0]
    for r in gates[1:]:
        tot = tot + r
    for e in range(N_EXPERTS):
        o_ref[e:e + 1, :] = gates[e] / tot


def _router(x, wr_hi, wr_lo, b_col, *, tm):
    t, d = x.shape
    return pl.pallas_call(
        _router_kernel,
        grid=(t // tm,),
        in_specs=[pl.BlockSpec((tm, d), lambda i: (i, 0)),
                  pl.BlockSpec((d, LANES), lambda i: (0, 0)),
                  pl.BlockSpec((d, LANES), lambda i: (0, 0)),
                  pl.BlockSpec((N_EXPERTS, 1), lambda i: (0, 0))],
        out_specs=pl.BlockSpec((N_EXPERTS, tm), lambda i: (0, i)),
        out_shape=jax.ShapeDtypeStruct((N_EXPERTS, t), F32),
        compiler_params=_params("arbitrary"),
        name="router",
    )(x, wr_hi, wr_lo, b_col)


def _moe_kernel(x_ref, gate_ref, wg_ref, wu_ref, wd_ref, g_ref, b_ref, o_ref, xbf_ref, acc_ref):
    e = pl.program_id(1)

    @pl.when(e == 0)
    def _():
        xbf_ref[...] = x_ref[...].astype(BF16)
        acc_ref[...] = jnp.zeros(acc_ref.shape, F32)

    xb = xbf_ref[...]
    gate = gate_ref[...]
    lane = lax.broadcasted_iota(jnp.int32, gate.shape, 1)
    gcol = jnp.sum(jnp.where(lane == e, gate, 0.0), axis=1, keepdims=True)
    hgate = jnp.dot(xb, wg_ref[0].astype(BF16), preferred_element_type=F32)
    hup = jnp.dot(xb, wu_ref[0].astype(BF16), preferred_element_type=F32)
    h = _silu(hgate) * hup * gcol
    acc_ref[...] += _bdot(h, wd_ref[0])

    @pl.when(e == N_EXPERTS - 1)
    def _():
        o_ref[...] = _ln(DN_ALPHA * x_ref[...] + acc_ref[...], g_ref[...], b_ref[...])


def _moe(x, gate, w_gate, w_up, w_down, g, b, *, tm):
    t, d = x.shape
    f = w_gate.shape[2]
    return pl.pallas_call(
        _moe_kernel,
        grid=(t // tm, N_EXPERTS),
        in_specs=[pl.BlockSpec((tm, d), lambda i, e: (i, 0)),
                  pl.BlockSpec((tm, N_EXPERTS), lambda i, e: (i, 0)),
                  pl.BlockSpec((1, d, f), lambda i, e: (e, 0, 0)),
                  pl.BlockSpec((1, d, f), lambda i, e: (e, 0, 0)),
                  pl.BlockSpec((1, f, d), lambda i, e: (e, 0, 0)),
                  pl.BlockSpec((1, d), lambda i, e: (0, 0)),
                  pl.BlockSpec((1, d), lambda i, e: (0, 0))],
        out_specs=pl.BlockSpec((tm, d), lambda i, e: (i, 0)),
        out_shape=jax.ShapeDtypeStruct((t, d), F32),
        scratch_shapes=[pltpu.VMEM((tm, d), BF16), pltpu.VMEM((tm, d), F32)],
        compiler_params=_params("arbitrary", "arbitrary"),
        name="moe_experts",
    )(x, gate, w_gate, w_up, w_down, g, b)


def _ple_kernel(x_ref, p_ref, wg_ref, bg_ref, wp_ref, o_ref):
    x = x_ref[...]
    gate = jax.nn.sigmoid(_bdot(x, wg_ref[...]) + bg_ref[...])
    o_ref[...] = x + gate * _bdot(p_ref[...], wp_ref[...])


def _ple(x, p, w_gate, b_gate, w_p, *, tm):
    t, d = x.shape
    pd = p.shape[1]
    return pl.pallas_call(
        _ple_kernel,
        grid=(t // tm,),
        in_specs=[pl.BlockSpec((tm, d), lambda i: (i, 0)),
                  pl.BlockSpec((tm, pd), lambda i: (i, 0)),
                  pl.BlockSpec((d, d), lambda i: (0, 0)),
                  pl.BlockSpec((1, d), lambda i: (0, 0)),
                  pl.BlockSpec((pd, d), lambda i: (0, 0))],
        out_specs=pl.BlockSpec((tm, d), lambda i: (i, 0)),
        out_shape=jax.ShapeDtypeStruct((t, d), F32),
        compiler_params=_params("arbitrary"),
        name="ple_gate",
    )(x, p, w_gate, b_gate, w_p)


def _row_tile(rows, want):
    tm = min(rows, want)
    assert rows % tm == 0
    return tm


POOL_TM = 512
MM_TM = 512
QKV_TM = 512
ATTN_TQ = 256
MOE_TM = 1024
ROUTER_TM = 1024
PLE_TM = 512
DECODE_PAGES_PER_STEP = 4


def kernel(x_prompt, x_sample, p_prompt, p_sample, state_pool, cache_k, cache_v, page_table, state_ssm, state_conv,
           ln_g, ln_b, w_pool_in, w_pool_grp, pool_scale, w_pool_out, w_diff_in, diff_lambda_q, diff_lambda_k,
           diff_norm_g, w_diff_out, w_ssd_in, ssd_conv_w, ssd_conv_b, ssd_dt_bias, ssd_a_log, ssd_d, ssd_norm_g,
           w_ssd_out, w_router, b_router, w_exp_gate, w_exp_up, w_exp_down, w_ple, w_ple_gate, b_ple_gate):
    bp, seq, d = x_prompt.shape
    bs, tdec, _ = x_sample.shape
    n_pages = page_table.shape[1]
    past = n_pages * cache_k.shape[2]
    tp = bp * seq
    ts = bs * tdec

    def to_tm(a):
        return jnp.swapaxes(a, 0, 1).reshape((ts,) + a.shape[2:])

    def from_tm(a):
        return jnp.swapaxes(a.reshape((tdec, bs) + a.shape[1:]), 0, 1)

    row = lambda v: v.reshape(1, -1)

    wr = jnp.pad(w_router, ((0, 0), (0, LANES - N_EXPERTS)))
    wr_hi = wr.astype(BF16)
    wr_lo = (wr - wr_hi.astype(F32)).astype(BF16)
    br_col = b_router.reshape(N_EXPERTS, 1)

    def finish(x1, i, p, tm_cap):
        t = x1.shape[0]
        gate = _router(x1, wr_hi, wr_lo, br_col, tm=_row_tile(t, min(ROUTER_TM, tm_cap))).T
        x2 = _moe(x1, gate, w_exp_gate[i], w_exp_up[i], w_exp_down[i], row(ln_g[i, 1]), row(ln_b[i, 1]),
                  tm=_row_tile(t, min(MOE_TM, tm_cap)))
        return _ple(x2, p, w_ple_gate[i].astype(BF16), row(b_ple_gate[i]), w_ple[i].astype(BF16),
                    tm=_row_tile(t, min(PLE_TM, tm_cap)))

    xp = x_prompt.reshape(tp, d)
    xs = to_tm(x_sample)
    pos_p = jnp.arange(seq, dtype=jnp.int32)
    pos_s = jnp.repeat(past + jnp.arange(tdec, dtype=jnp.int32), bs)
    outs = {k: [] for k in ("pool_p", "pool_s", "k_p", "v_p", "k_s", "v_s", "ssm_p", "conv_p", "ssm_s", "conv_s")}

    for i in range(DEPTH):
        kind, j = i % N_MIXERS, i // N_MIXERS
        g1, b1 = row(ln_g[i, 0]), row(ln_b[i, 0])
        if kind == 0:
            wts = (w_pool_in[j].astype(BF16), w_pool_grp[j].astype(BF16), row(pool_scale[j]),
                   w_pool_out[j].astype(BF16), g1, b1)
            x1p, newp = _pool_layer(xp.reshape(bp, seq, d), jnp.zeros((bp, POOL_BUF + 1, d), F32), *wts,
                                    tm=_row_tile(seq, POOL_TM), stride=1, pos0=0)
            pre_s = jnp.swapaxes(state_pool[j], 0, 1).reshape(1, POOL_BUF * bs, d)
            pre_s = jnp.pad(pre_s, ((0, 0), (bs, 0), (0, 0)))
            x1s, news = _pool_layer(xs.reshape(1, ts, d), pre_s, *wts, tm=ts, stride=bs, pos0=past)
            outs["pool_p"].append(newp)
            outs["pool_s"].append(jnp.swapaxes(news.reshape(POOL_BUF, bs, d), 0, 1))
            x1p = x1p.reshape(tp, d)
            x1s = x1s.reshape(ts, d)
        elif kind == 1:
            lam_init = 0.8 - 0.6 * math.exp(-0.3 * i)
            w_in = w_diff_in[j].astype(BF16)
            w_out = w_diff_out[j].astype(BF16)
            lq, lk, ng = diff_lambda_q[j], diff_lambda_k[j], row(diff_norm_g[j])
            cos_p, sin_p = _rope_tables(pos_p)
            q, k, v = _qkv_rope(xp.reshape(bp, seq, d), w_in, cos_p, sin_p, tm=_row_tile(seq, QKV_TM))
            o = _diff_attn_prompt(q, k, v, lq, lk, ng, lam_init, tq=_row_tile(seq, ATTN_TQ))
            x1p = _mm_res_ln(o.reshape(tp, d), w_out, xp, g1, b1, tm=_row_tile(tp, MM_TM))
            outs["k_p"].append(k.reshape(bp, seq, DIFF_HEADS, 2, DIFF_DH))
            outs["v_p"].append(v.reshape(bp, seq, DIFF_HEADS, DIFF_VD))

            cos_s, sin_s = _rope_tables(pos_s)
            q, k, v = _qkv_rope(xs.reshape(1, ts, d), w_in, cos_s, sin_s, tm=ts)
            q, k, v = from_tm(q[0]), from_tm(k[0]), from_tm(v[0])
            q5 = q.reshape(bs, tdec, DIFF_HEADS * 2, DIFF_DH)
            eye = jnp.eye(DIFF_HEADS * 2, dtype=F32)
            qbd = jnp.einsum('bthd,hg->bhtgd', q5, eye)
            qbd = jnp.pad(qbd, ((0, 0), (0, 0), (0, DEC_TPAD - tdec), (0, 0), (0, 0)))
            qbd = qbd.reshape(bs, DEC_ROWS, d)
            padk = ((0, 0), (0, PAGE_SIZE - tdec), (0, 0))
            o = _diff_attn_sample(qbd, cache_k[j].reshape(-1, PAGE_SIZE, d), cache_v[j].reshape(-1, PAGE_SIZE, d),
                                  page_table, jnp.pad(k, padk), jnp.pad(v, padk), lq, lk, ng, lam_init,
                                  pages_per_step=DECODE_PAGES_PER_STEP)
            o = to_tm(o[:, :tdec])
            x1s = _mm_res_ln(o, w_out, xs, g1, b1, tm=ts)
            outs["k_s"].append(k.reshape(bs, tdec, DIFF_HEADS, 2, DIFF_DH))
            outs["v_s"].append(v.reshape(bs, tdec, DIFF_HEADS, DIFF_VD))
        else:
            w_in = w_ssd_in[j]
            w_z = w_in[:, :SSD_DI].astype(BF16)
            w_xbc = w_in[:, SSD_DI:SSD_DI + SSD_CONV_DIM].astype(BF16)
            w_dt = jnp.pad(w_in[:, SSD_DI + SSD_CONV_DIM:], ((0, 0), (0, LANES - SSD_H))).astype(BF16)
            w_out = w_ssd_out[j].astype(BF16)
            dtb_row = jnp.pad(ssd_dt_bias[j], (0, LANES - SSD_H)).reshape(1, LANES)
            alog_row = jnp.pad(ssd_a_log[j], (0, LANES - SSD_H)).reshape(1, LANES)
            dtb_col = ssd_dt_bias[j].reshape(SSD_H, 1)
            alog_col = ssd_a_log[j].reshape(SSD_H, 1)
            d_exp = jnp.repeat(ssd_d[j], SSD_P).reshape(1, SSD_DI)
            ng = row(ssd_norm_g[j])
            expand = (jnp.arange(SSD_DI)[None, :] // SSD_P == jnp.arange(LANES)[:, None]).astype(BF16)
            shared = (ssd_conv_w[j], row(ssd_conv_b[j]), dtb_row, dtb_col, alog_row, alog_col, d_exp, ng, expand)

            def in_proj(x, tm):
                z = _matmul(x, w_z, tm=tm, tn=1024)
                xbc = _matmul(x, w_xbc, tm=tm, tn=1024)
                dt = _matmul(x, w_dt, tm=tm, tn=LANES)
                return z, xbc, dt

            def hlast_to_state(hl, b):
                return jnp.transpose(hl.reshape(b, SSD_N, SSD_H, SSD_P), (0, 2, 3, 1))

            z, xbc, dt = in_proj(xp, _row_tile(tp, MM_TM))
            dtt = jnp.swapaxes(dt.reshape(bp, seq, LANES)[:, :, :SSD_H], 1, 2)
            y, hl = _ssd_scan(z.reshape(bp, seq, SSD_DI), xbc.reshape(bp, seq, SSD_CONV_DIM),
                              dt.reshape(bp, seq, LANES), dtt, jnp.zeros((bp, CONV_PAD, SSD_CONV_DIM), F32),
                              *shared, None, valid_len=None)
            x1p = _mm_res_ln(y.reshape(tp, SSD_DI), w_out, xp, g1, b1, tm=_row_tile(tp, MM_TM))
            outs["ssm_p"].append(hlast_to_state(hl, bp))
            outs["conv_p"].append(xbc.reshape(bp, seq, SSD_CONV_DIM)[:, seq - (SSD_CONV - 1):])

            z, xbc, dt = in_proj(xs, ts)
            z, xbc, dt = from_tm(z), from_tm(xbc), from_tm(dt)
            padt = ((0, 0), (0, SSD_CHUNK - tdec), (0, 0))
            dtp = jnp.pad(dt, padt)
            cpre = jnp.pad(state_conv[j], ((0, 0), (CONV_PAD - (SSD_CONV - 1), 0), (0, 0)))
            h0t = jnp.transpose(state_ssm[j], (0, 3, 1, 2)).reshape(bs, SSD_N, SSD_DI)
            y, hl = _ssd_scan(jnp.pad(z, padt), jnp.pad(xbc, padt), dtp, jnp.swapaxes(dtp[:, :, :SSD_H], 1, 2),
                              cpre, *shared, h0t, valid_len=tdec)
            x1s = _mm_res_ln(to_tm(y[:, :tdec]), w_out, xs, g1, b1, tm=ts)
            outs["ssm_s"].append(hlast_to_state(hl, bs))
            ext = jnp.concatenate([state_conv[j], xbc], axis=1)
            outs["conv_s"].append(ext[:, -(SSD_CONV - 1):])

        xp = finish(x1p, i, p_prompt[i].reshape(tp, PLE_DIM), tp)
        xs = finish(x1s, i, to_tm(p_sample[i]), ts)

    st = lambda name: jnp.stack(outs[name])
    return (xp.reshape(bp, seq, d), from_tm(xs), st("pool_p"), st("pool_s"), st("k_p"), st("v_p"), st("k_s"),
            st("v_s"), st("ssm_p"), st("conv_p"), st("ssm_s"), st("conv_s"))
```

```python
import functools
import math

import jax
import jax.numpy as jnp
from jax import lax
from jax.experimental import pallas as pl
from jax.experimental.pallas import tpu as pltpu

F32 = jnp.float32
BF16 = jnp.bfloat16

D_MODEL = 1024
DEPTH = 4
N_MIXERS = 3
PAGE_SIZE = 128

POOL_WINDOWS = (2, 4, 8, 16)
POOL_GD = D_MODEL // len(POOL_WINDOWS)
POOL_BUF = max(POOL_WINDOWS) - 1

DIFF_DH = 64
DIFF_VD = 2 * DIFF_DH
DIFF_HEADS = D_MODEL // DIFF_VD
ROPE_THETA = 10000.0

SSD_DI = 2 * D_MODEL
SSD_P = 64
SSD_H = SSD_DI // SSD_P
SSD_G = 4
SSD_N = 128
SSD_CONV = 4
SSD_CONV_DIM = SSD_DI + 2 * SSD_G * SSD_N
SSD_CHUNK = 128

N_EXPERTS = 16
N_EGROUPS = 4
EXP_PER_GROUP = N_EXPERTS // N_EGROUPS
D_FF_EXPERT = D_MODEL // 2

PLE_DIM = 256
DN_ALPHA = (2 * DEPTH) ** 0.25
NORM_EPS = 1e-5

LANES = 128
SUBLANES = 8
VMEM_LIMIT = 56 * 1024 * 1024


def _params(*sem):
    return pltpu.CompilerParams(dimension_semantics=sem, vmem_limit_bytes=VMEM_LIMIT)


def _bdot(a, b):
    return jnp.dot(a.astype(BF16), b.astype(BF16), preferred_element_type=F32)


def _bdot_nt(a, b):
    return lax.dot_general(a.astype(BF16), b.astype(BF16), (((1,), (1,)), ((), ())),
                           preferred_element_type=F32)


def _split3(x):
    hi = x.astype(BF16)
    r = x - hi.astype(F32)
    mid = r.astype(BF16)
    lo = (r - mid.astype(F32)).astype(BF16)
    return hi, mid, lo


def _dot_exact_rhs(x, m):
    hi, mid, lo = _split3(x)
    return (jnp.dot(hi, m, preferred_element_type=F32) + jnp.dot(mid, m, preferred_element_type=F32)
            + jnp.dot(lo, m, preferred_element_type=F32))


def _dot_exact_lhs(m, x):
    hi, mid, lo = _split3(x)
    return (jnp.dot(m, hi, preferred_element_type=F32) + jnp.dot(m, mid, preferred_element_type=F32)
            + jnp.dot(m, lo, preferred_element_type=F32))


def _ln(v, g, b):
    mu = jnp.mean(v, axis=-1, keepdims=True)
    c = v - mu
    var = jnp.mean(c * c, axis=-1, keepdims=True)
    return c * lax.rsqrt(var + NORM_EPS) * g + b


def _silu(v):
    return v * jax.nn.sigmoid(v)


def _softplus(v):
    return jnp.maximum(v, 0.0) + jnp.log(1.0 + jnp.exp(-jnp.abs(v)))


def _pool_kernel(x_ref, pre_ref, win_ref, wgrp_ref, scale_ref, wout_ref, g_ref, b_ref,
                 y_ref, newp_ref, ext_ref, *, tm, stride, pos0, n_tiles):
    i = pl.program_id(1)
    pre_rows = (POOL_BUF + 1) * stride

    @pl.when(i == 0)
    def _():
        ext_ref[0:pre_rows, :] = pre_ref[0]

    @pl.when(i > 0)
    def _():
        ext_ref[0:pre_rows, :] = ext_ref[tm:tm + pre_rows, :]

    x = x_ref[0]
    u = _bdot(x, win_ref[...])
    ext_ref[pre_rows:pre_rows + tm, :] = u
    row = lax.broadcasted_iota(jnp.int32, (tm, 1), 0)
    pos = pos0 + (i * tm + row) // stride
    mixed = []
    for gi, w in enumerate(POOL_WINDOWS):
        c0 = gi * POOL_GD
        ug = u[:, c0:c0 + POOL_GD]
        s = ug
        for k in range(1, w):
            r0 = pre_rows - k * stride
            s = s + ext_ref[r0:r0 + tm, c0:c0 + POOL_GD]
        cnt = jnp.minimum(w, pos + 1).astype(F32)
        pooled = s / cnt - ug
        mixed.append(_bdot(pooled, wgrp_ref[gi]))
    mixed = jnp.concatenate(mixed, axis=1) * scale_ref[...]
    mix = _bdot(mixed, wout_ref[...])
    y_ref[0] = _ln(DN_ALPHA * x + mix, g_ref[...], b_ref[...])

    @pl.when(i == n_tiles - 1)
    def _():
        newp_ref[0] = ext_ref[tm + stride:tm + pre_rows, :]


def _pool_layer(x, prefix, w_in, w_grp, scale, w_out, g, b, *, tm, stride, pos0):
    bsz, rows, d = x.shape
    n_tiles = rows // tm
    pre_rows = (POOL_BUF + 1) * stride
    kern = functools.partial(_pool_kernel, tm=tm, stride=stride, pos0=pos0, n_tiles=n_tiles)
    full = lambda *shape: pl.BlockSpec(shape, lambda bi, i: (0,) * len(shape))
    return pl.pallas_call(
        kern,
        grid=(bsz, n_tiles),
        in_specs=[
            pl.BlockSpec((1, tm, d), lambda bi, i: (bi, i, 0)),
            pl.BlockSpec((1, pre_rows, d), lambda bi, i: (bi, 0, 0)),
            full(d, d), full(len(POOL_WINDOWS), POOL_GD, POOL_GD), full(1, d), full(d, d),
            full(1, d), full(1, d),
        ],
        out_specs=[
            pl.BlockSpec((1, tm, d), lambda bi, i: (bi, i, 0)),
            pl.BlockSpec((1, POOL_BUF * stride, d), lambda bi, i: (bi, 0, 0)),
        ],
        out_shape=[jax.ShapeDtypeStruct((bsz, rows, d), F32),
                   jax.ShapeDtypeStruct((bsz, POOL_BUF * stride, d), F32)],
        scratch_shapes=[pltpu.VMEM((pre_rows + tm, d), F32)],
        compiler_params=_params("arbitrary", "arbitrary"),
        name="pool_layer",
    )(x, prefix, w_in, w_grp, scale, w_out, g, b)


def _mm_kernel(x_ref, w_ref, o_ref):
    o_ref[...] = _bdot(x_ref[...], w_ref[...]).astype(o_ref.dtype)


def _matmul(x, w, *, tm, tn, out_dtype=F32):
    t, k = x.shape
    n = w.shape[1]
    return pl.pallas_call(
        _mm_kernel,
        grid=(n // tn, t // tm),
        in_specs=[pl.BlockSpec((tm, k), lambda j, i: (i, 0)),
                  pl.BlockSpec((k, tn), lambda j, i: (0, j))],
        out_specs=pl.BlockSpec((tm, tn), lambda j, i: (i, j)),
        out_shape=jax.ShapeDtypeStruct((t, n), out_dtype),
        compiler_params=_params("arbitrary", "arbitrary"),
        name="matmul",
    )(x, w)


def _mm_res_ln_kernel(a_ref, w_ref, x_ref, g_ref, b_ref, o_ref):
    mix = _bdot(a_ref[...], w_ref[...])
    o_ref[...] = _ln(DN_ALPHA * x_ref[...] + mix, g_ref[...], b_ref[...])


def _mm_res_ln(a, w, x, g, b, *, tm):
    t, k = a.shape
    d = w.shape[1]
    return pl.pallas_call(
        _mm_res_ln_kernel,
        grid=(t // tm,),
        in_specs=[pl.BlockSpec((tm, k), lambda i: (i, 0)),
                  pl.BlockSpec((k, d), lambda i: (0, 0)),
                  pl.BlockSpec((tm, d), lambda i: (i, 0)),
                  pl.BlockSpec((1, d), lambda i: (0, 0)),
                  pl.BlockSpec((1, d), lambda i: (0, 0))],
        out_specs=pl.BlockSpec((tm, d), lambda i: (i, 0)),
        out_shape=jax.ShapeDtypeStruct((t, d), F32),
        compiler_params=_params("arbitrary"),
        name="mm_res_ln",
    )(a, w, x, g, b)


def _qkv_kernel(x_ref, w_ref, cos_ref, sin_ref, q_ref, k_ref, v_ref, *, tm):
    x = x_ref[0].astype(BF16)
    cos = cos_ref[...]
    sin = sin_ref[...]
    lane = lax.broadcasted_iota(jnp.int32, (tm, LANES), 1)
    first_half = (lane % DIFF_DH) < (DIFF_DH // 2)

    def rope(y):
        outs = []
        for c in range(D_MODEL // LANES):
            yc = y[:, c * LANES:(c + 1) * LANES]
            sw = jnp.where(first_half, pltpu.roll(yc, LANES - DIFF_DH // 2, 1),
                           pltpu.roll(yc, DIFF_DH // 2, 1))
            outs.append(yc * cos + sw * sin)
        return jnp.concatenate(outs, axis=1)

    q_ref[0] = rope(jnp.dot(x, w_ref[:, 0:D_MODEL], preferred_element_type=F32))
    k_ref[0] = rope(jnp.dot(x, w_ref[:, D_MODEL:2 * D_MODEL], preferred_element_type=F32))
    v_ref[0] = jnp.dot(x, w_ref[:, 2 * D_MODEL:3 * D_MODEL], preferred_element_type=F32)


def _rope_tables(pos):
    half = DIFF_DH // 2
    inv = ROPE_THETA ** (-jnp.arange(half, dtype=F32) * 2.0 / DIFF_DH)
    ang = pos.astype(F32)[:, None] * inv[None, :]
    cos, sin = jnp.cos(ang), jnp.sin(ang)
    reps = LANES // DIFF_DH
    cos_t = jnp.tile(jnp.concatenate([cos, cos], axis=1), (1, reps))
    sin_t = jnp.tile(jnp.concatenate([-sin, sin], axis=1), (1, reps))
    return cos_t, sin_t


def _qkv_rope(x, w, cos_t, sin_t, *, tm):
    bsz, rows, d = x.shape
    kern = functools.partial(_qkv_kernel, tm=tm)
    spec = pl.BlockSpec((1, tm, d), lambda bi, i: (bi, i, 0))
    tab = pl.BlockSpec((tm, LANES), lambda bi, i: (i, 0))
    shp = jax.ShapeDtypeStruct((bsz, rows, d), F32)
    return pl.pallas_call(
        kern,
        grid=(bsz, rows // tm),
        in_specs=[spec, pl.BlockSpec((d, 3 * d), lambda bi, i: (0, 0)), tab, tab],
        out_specs=[spec, spec, spec],
        out_shape=[shp, shp, shp],
        compiler_params=_params("arbitrary", "arbitrary"),
        name="qkv_rope",
    )(x, w, cos_t, sin_t)


def _diff_lambda(lq_ref, lk_ref, lam_init):
    dots = jnp.sum(lq_ref[...] * lk_ref[...], axis=-1, keepdims=True)
    e = jnp.exp(dots)
    return e[0:1, :] - e[1:2, :] + lam_init


def _head_norm(o, ng, lam_init):
    ms = jnp.mean(o * o, axis=-1, keepdims=True)
    return o * lax.rsqrt(ms + NORM_EPS) * ng * (1.0 - lam_init)


def _attn_kernel(q_ref, k_ref, v_ref, lq_ref, lk_ref, ng_ref, o_ref, *, tq, lam_init):
    qi = pl.program_id(2)
    q = q_ref[0] * (DIFF_DH ** -0.5)
    q1 = q[:, :DIFF_DH].astype(BF16)
    q2 = q[:, DIFF_DH:].astype(BF16)
    ones = jnp.ones((tq, DIFF_VD), BF16)

    def scores(kb):
        start = pl.multiple_of(kb * tq, tq)
        k = k_ref[0, pl.ds(start, tq), :]
        v1 = jnp.concatenate([v_ref[0, pl.ds(start, tq), :].astype(BF16), ones], axis=1)
        s = jnp.concatenate([_bdot_nt(q1, k[:, :DIFF_DH]), _bdot_nt(q2, k[:, DIFF_DH:])], axis=0)
        return s, v1

    def update(s, v1, m, a):
        mn = jnp.maximum(m, jnp.max(s, axis=1, keepdims=True))
        p = jnp.exp(s - mn).astype(BF16)
        a = jnp.exp(m - mn) * a + jnp.dot(p, v1, preferred_element_type=F32)
        return mn, a

    def body(kb, carry):
        s, v1 = scores(kb)
        return update(s, v1, *carry)

    init = (jnp.full((2 * tq, 1), -jnp.inf, F32), jnp.zeros((2 * tq, 2 * DIFF_VD), F32))
    m, a = lax.fori_loop(0, qi, body, init)
    s, v1 = scores(qi)
    row = lax.broadcasted_iota(jnp.int32, (2 * tq, tq), 0) % tq
    col = lax.broadcasted_iota(jnp.int32, (2 * tq, tq), 1)
    m, a = update(jnp.where(col <= row, s, -jnp.inf), v1, m, a)
    lam = _diff_lambda(lq_ref, lk_ref, lam_init)
    o = (a[:tq, :DIFF_VD] / a[:tq, DIFF_VD:DIFF_VD + 1]
         - lam * (a[tq:, :DIFF_VD] / a[tq:, DIFF_VD:DIFF_VD + 1]))
    o_ref[0] = _head_norm(o, ng_ref[...], lam_init).astype(o_ref.dtype)


def _diff_attn_prompt(q, k, v, lq, lk, ng, lam_init, *, tq):
    bsz, seq, d = q.shape
    kern = functools.partial(_attn_kernel, tq=tq, lam_init=lam_init)
    small = lambda *shape: pl.BlockSpec(shape, lambda bi, h, i: (0,) * len(shape))
    return pl.pallas_call(
        kern,
        grid=(bsz, DIFF_HEADS, seq // tq),
        in_specs=[pl.BlockSpec((1, tq, DIFF_VD), lambda bi, h, i: (bi, i, h)),
                  pl.BlockSpec((1, seq, DIFF_VD), lambda bi, h, i: (bi, 0, h)),
                  pl.BlockSpec((1, seq, DIFF_VD), lambda bi, h, i: (bi, 0, h)),
                  small(2, DIFF_DH), small(2, DIFF_DH), small(1, DIFF_VD)],
        out_specs=pl.BlockSpec((1, tq, DIFF_VD), lambda bi, h, i: (bi, i, h)),
        out_shape=jax.ShapeDtypeStruct((bsz, seq, d), BF16),
        compiler_params=_params("arbitrary", "arbitrary", "arbitrary"),
        name="diff_attn_prompt",
    )(q, k, v, lq, lk, ng)


DEC_TPAD = SUBLANES
DEC_ROWS = DIFF_HEADS * 2 * DEC_TPAD


def _decode_kernel(pt_ref, q_ref, *refs, pages_per_step, n_steps, lam_init):
    g = pages_per_step
    k_refs = refs[:g]
    v_refs = refs[g:2 * g]
    knew_ref, vnew_ref, lq_ref, lk_ref, ng_ref, o_ref, m_ref, l_ref, acc_ref = refs[2 * g:]
    s = pl.program_id(1)
    scale = DIFF_DH ** -0.5

    @pl.when(s == 0)
    def _():
        m_ref[...] = jnp.full(m_ref.shape, -jnp.inf, F32)
        l_ref[...] = jnp.zeros(l_ref.shape, F32)
        acc_ref[...] = jnp.zeros(acc_ref.shape, F32)

    q = (q_ref[0] * scale).astype(BF16)

    def block(ks, vs, mask):
        sc = jnp.concatenate([_bdot_nt(q, k[0]) for k in ks], axis=1)
        if mask is not None:
            sc = jnp.where(mask, sc, -jnp.inf)
        m_old = m_ref[...]
        mn = jnp.maximum(m_old, jnp.max(sc, axis=1, keepdims=True))
        alpha = jnp.exp(m_old - mn)
        p = jnp.exp(sc - mn)
        pv = _bdot(p[:, 0:PAGE_SIZE], vs[0][0])
        for gi in range(1, len(vs)):
            pv = pv + _bdot(p[:, gi * PAGE_SIZE:(gi + 1) * PAGE_SIZE], vs[gi][0])
        l_ref[...] = alpha * l_ref[...] + jnp.sum(p, axis=1, keepdims=True)
        acc_ref[...] = alpha * acc_ref[...] + pv
        m_ref[...] = mn

    block(k_refs, v_refs, None)

    @pl.when(s == n_steps - 1)
    def _():
        t_row = lax.broadcasted_iota(jnp.int32, (DEC_ROWS, PAGE_SIZE), 0) % DEC_TPAD
        key = lax.broadcasted_iota(jnp.int32, (DEC_ROWS, PAGE_SIZE), 1)
        block([knew_ref], [vnew_ref], key <= t_row)
        lam = _diff_lambda(lq_ref, lk_ref, lam_init)
        ng = ng_ref[...]
        for h in range(DIFF_HEADS):
            r1 = h * 2 * DEC_TPAD
            r2 = r1 + DEC_TPAD
            cols = slice(h * DIFF_VD, (h + 1) * DIFF_VD)
            a1 = acc_ref[r1:r1 + DEC_TPAD, cols] / l_ref[r1:r1 + DEC_TPAD, :]
            a2 = acc_ref[r2:r2 + DEC_TPAD, cols] / l_ref[r2:r2 + DEC_TPAD, :]
            o_ref[0, :, cols] = _head_norm(a1 - lam * a2, ng, lam_init).astype(o_ref.dtype)


def _diff_attn_sample(qbd, cache_k, cache_v, page_table, knew, vnew, lq, lk, ng, lam_init,
                      *, pages_per_step, page0):
    bsz = qbd.shape[0]
    n_pages = page_table.shape[1]
    g = pages_per_step
    n_steps = n_pages // g
    kern = functools.partial(_decode_kernel, pages_per_step=g, n_steps=n_steps, lam_init=lam_init)

    def page_spec(gi):
        return pl.BlockSpec((1, PAGE_SIZE, D_MODEL), lambda b, s, pt: (page0 + pt[b, s * g + gi], 0, 0))

    per_b = lambda rows: pl.BlockSpec((1, rows, D_MODEL), lambda b, s, pt: (b, 0, 0))
    small = lambda *shape: pl.BlockSpec(shape, lambda b, s, pt: (0,) * len(shape))
    grid_spec = pltpu.PrefetchScalarGridSpec(
        num_scalar_prefetch=1,
        grid=(bsz, n_steps),
        in_specs=([per_b(DEC_ROWS)] + [page_spec(gi) for gi in range(g)] + [page_spec(gi) for gi in range(g)]
                  + [per_b(PAGE_SIZE), per_b(PAGE_SIZE), small(2, DIFF_DH), small(2, DIFF_DH), small(1, DIFF_VD)]),
        out_specs=per_b(DEC_TPAD),
        scratch_shapes=[pltpu.VMEM((DEC_ROWS, 1), F32), pltpu.VMEM((DEC_ROWS, 1), F32),
                        pltpu.VMEM((DEC_ROWS, D_MODEL), F32)],
    )
    return pl.pallas_call(
        kern,
        grid_spec=grid_spec,
        out_shape=jax.ShapeDtypeStruct((bsz, DEC_TPAD, D_MODEL), BF16),
        compiler_params=_params("arbitrary", "arbitrary"),
        name="diff_attn_sample",
    )(page_table, qbd, *([cache_k] * g), *([cache_v] * g), knew, vnew, lq, lk, ng)


CONV_PAD = SUBLANES


def _ssd_kernel(*refs, q, n_chunks, valid_len, has_h0):
    if has_h0:
        (z_ref, xbc_ref, dt_ref, dtt_ref, cpre_ref, cw_ref, cb_ref, dtb_row_ref, dtb_col_ref,
         alog_row_ref, alog_col_ref, dexp_ref, ng_ref, expand_ref, h0_ref,
         y_ref, hl_ref, s_ref, cext_ref) = refs
    else:
        (z_ref, xbc_ref, dt_ref, dtt_ref, cpre_ref, cw_ref, cb_ref, dtb_row_ref, dtb_col_ref,
         alog_row_ref, alog_col_ref, dexp_ref, ng_ref, expand_ref,
         y_ref, hl_ref, s_ref, cext_ref) = refs
        h0_ref = None
    c = pl.program_id(1)

    @pl.when(c == 0)
    def _():
        if has_h0:
            s_ref[...] = h0_ref[0]
        else:
            s_ref[...] = jnp.zeros(s_ref.shape, F32)
        cext_ref[0:CONV_PAD, :] = cpre_ref[0]

    @pl.when(c > 0)
    def _():
        cext_ref[0:CONV_PAD, :] = cext_ref[q:q + CONV_PAD, :]

    cext_ref[CONV_PAD:CONV_PAD + q, :] = xbc_ref[0]
    xbc = cb_ref[...]
    for kk in range(SSD_CONV):
        r0 = CONV_PAD - (SSD_CONV - 1) + kk
        xbc = xbc + cext_ref[r0:r0 + q, :] * cw_ref[kk:kk + 1, :]
    xbc = _silu(xbc)
    xs = xbc[:, :SSD_DI]
    bm = xbc[:, SSD_DI:SSD_DI + SSD_G * SSD_N]
    cm = xbc[:, SSD_DI + SSD_G * SSD_N:]

    dtc = _softplus(dt_ref[0] + dtb_row_ref[...])
    dtr = _softplus(dtt_ref[0] + dtb_col_ref[...])
    if valid_len is not None:
        t_col = c * q + lax.broadcasted_iota(jnp.int32, (q, LANES), 0)
        t_row = c * q + lax.broadcasted_iota(jnp.int32, (SSD_H, q), 1)
        dtc = jnp.where(t_col < valid_len, dtc, 0.0)
        dtr = jnp.where(t_row < valid_len, dtr, 0.0)
    ii = lax.broadcasted_iota(jnp.int32, (q, q), 0)
    jj = lax.broadcasted_iota(jnp.int32, (q, q), 1)
    causal = jj <= ii
    tri = jnp.where(causal, 1.0, 0.0).astype(BF16)
    tri_t = jnp.where(ii <= jj, 1.0, 0.0).astype(BF16)
    acum_c = _dot_exact_lhs(tri, dtc * (-jnp.exp(alog_row_ref[...])))
    acum_r = _dot_exact_rhs(dtr * (-jnp.exp(alog_col_ref[...])), tri_t)
    total = acum_c[q - 1:q, :]
    expand = expand_ref[...]
    dte_x = _dot_exact_rhs(jnp.exp(total - acum_c) * dtc, expand)
    eac_x = _dot_exact_rhs(jnp.exp(acum_c), expand)
    cdec_x = _dot_exact_rhs(jnp.broadcast_to(jnp.exp(total), (SUBLANES, LANES)), expand)[0:1, :]

    gw = SSD_DI // SSD_G
    hg = SSD_H // SSD_G
    lane = lax.broadcasted_iota(jnp.int32, (q, 2 * SSD_P), 1)
    ys = []
    for g in range(SSD_G):
        gc = slice(g * gw, (g + 1) * gw)
        bg = bm[:, g * SSD_N:(g + 1) * SSD_N]
        cg = cm[:, g * SSD_N:(g + 1) * SSD_N]
        cbm = _bdot_nt(cg, bg)
        xg = xs[:, gc]
        s_old = s_ref[:, gc]
        y_g = _bdot(cg, s_old) * eac_x[:, gc]
        s_ref[:, gc] = s_old * cdec_x[:, gc] + _bdot(bg.T, xg * dte_x[:, gc])
        pairs = []
        for hp in range(hg // 2):
            wts = []
            for hh in range(2):
                h = g * hg + 2 * hp + hh
                seg = acum_c[:, h:h + 1] - acum_r[h:h + 1, :]
                dec = jnp.exp(jnp.where(causal, seg, -jnp.inf))
                wts.append((cbm * dec * dtr[h:h + 1, :]).astype(BF16))
            xp = xg[:, hp * 2 * SSD_P:(hp + 1) * 2 * SSD_P]
            xbd = jnp.concatenate([jnp.where(lane < SSD_P, xp, 0.0), jnp.where(lane >= SSD_P, xp, 0.0)],
                                  axis=0).astype(BF16)
            pairs.append(jnp.dot(jnp.concatenate(wts, axis=1), xbd, preferred_element_type=F32))
        y_g = y_g + jnp.concatenate(pairs, axis=1) + xg * dexp_ref[:, gc]
        zg = z_ref[0, :, gc]
        y_g = y_g * _silu(zg)
        y_g = y_g * lax.rsqrt(jnp.mean(y_g * y_g, axis=-1, keepdims=True) + NORM_EPS)
        ys.append(y_g)
    y_ref[0] = (jnp.concatenate(ys, axis=1) * ng_ref[...]).astype(y_ref.dtype)

    @pl.when(c == n_chunks - 1)
    def _():
        hl_ref[0] = s_ref[...]


def _ssd_scan(z, xbc, dt, dtt, cpre, conv_w, conv_b, dtb_row, dtb_col, alog_row, alog_col, d_exp, ng,
              expand, h0t, *, valid_len):
    bsz, seq, _ = z.shape
    q = SSD_CHUNK
    n_chunks = seq // q
    has_h0 = h0t is not None
    kern = functools.partial(_ssd_kernel, q=q, n_chunks=n_chunks, valid_len=valid_len, has_h0=has_h0)
    full = lambda *shape: pl.BlockSpec(shape, lambda b, c: (0,) * len(shape))
    in_specs = [
        pl.BlockSpec((1, q, SSD_DI), lambda b, c: (b, c, 0)),
        pl.BlockSpec((1, q, SSD_CONV_DIM), lambda b, c: (b, c, 0)),
        pl.BlockSpec((1, q, LANES), lambda b, c: (b, c, 0)),
        pl.BlockSpec((1, SSD_H, q), lambda b, c: (b, 0, c)),
        pl.BlockSpec((1, CONV_PAD, SSD_CONV_DIM), lambda b, c: (b, 0, 0)),
        full(SSD_CONV, SSD_CONV_DIM), full(1, SSD_CONV_DIM),
        full(1, LANES), full(SSD_H, 1), full(1, LANES), full(SSD_H, 1),
        full(1, SSD_DI), full(1, SSD_DI), full(LANES, SSD_DI),
    ]
    args = [z, xbc, dt, dtt, cpre, conv_w, conv_b, dtb_row, dtb_col, alog_row, alog_col, d_exp, ng, expand]
    if has_h0:
        in_specs.append(pl.BlockSpec((1, SSD_N, SSD_DI), lambda b, c: (b, 0, 0)))
        args.append(h0t)
    return pl.pallas_call(
        kern,
        grid=(bsz, n_chunks),
        in_specs=in_specs,
        out_specs=[pl.BlockSpec((1, q, SSD_DI), lambda b, c: (b, c, 0)),
                   pl.BlockSpec((1, SSD_N, SSD_DI), lambda b, c: (b, 0, 0))],
        out_shape=[jax.ShapeDtypeStruct((bsz, seq, SSD_DI), BF16),
                   jax.ShapeDtypeStruct((bsz, SSD_N, SSD_DI), F32)],
        scratch_shapes=[pltpu.VMEM((SSD_N, SSD_DI), F32), pltpu.VMEM((CONV_PAD + q, SSD_CONV_DIM), F32)],
        compiler_params=_params("arbitrary", "arbitrary"),
        name="ssd_scan",
    )(*args)


def _first_max(vals):
    m = vals[0]
    for v in vals[1:]:
        m = jnp.maximum(m, v)
    flags = []
    taken = jnp.zeros_like(m)
    for v in vals:
        f = jnp.where(v == m, 1.0 - taken, 0.0)
        taken = taken + f
        flags.append(f)
    return m, flags


def _router_kernel(x_ref, wh_ref, wl_ref, b_ref, o_ref):
    x = x_ref[...]
    xh = x.astype(BF16)
    xl = (x - xh.astype(F32)).astype(BF16)
    logits = (jnp.dot(xh, wh_ref[...], preferred_element_type=F32)
              + jnp.dot(xh, wl_ref[...], preferred_element_type=F32)
              + jnp.dot(xl, wh_ref[...], preferred_element_type=F32))
    lt = logits.T
    rows = [lt[e:e + 1, :] for e in range(N_EXPERTS)]
    mx = rows[0]
    for r in rows[1:]:
        mx = jnp.maximum(mx, r)
    ex = [jnp.exp(r - mx) for r in rows]
    den = ex[0]
    for r in ex[1:]:
        den = den + r
    probs = [r / den for r in ex]
    sel = [probs[e] + b_ref[e:e + 1, :] for e in range(N_EXPERTS)]
    scores, chosen = [], []
    for g in range(N_EGROUPS):
        v = sel[g * EXP_PER_GROUP:(g + 1) * EXP_PER_GROUP]
        m1, f1 = _first_max(v)
        rest = [jnp.where(f > 0.0, -jnp.inf, vi) for f, vi in zip(f1, v)]
        m2, f2 = _first_max(rest)
        scores.append(m1 + m2)
        chosen.extend([a + b for a, b in zip(f1, f2)])
    _, gflag = _first_max(scores)
    gates = [probs[e] * chosen[e] * gflag[e // EXP_PER_GROUP] for e in range(N_EXPERTS)]
    tot = gates[0]
    for r in gates[1:]:
        tot = tot + r
    for e in range(N_EXPERTS):
        o_ref[e:e + 1, :] = gates[e] / tot


def _router(x, wr_hi, wr_lo, b_col, *, tm):
    t, d = x.shape
    return pl.pallas_call(
        _router_kernel,
        grid=(t // tm,),
        in_specs=[pl.BlockSpec((tm, d), lambda i: (i, 0)),
                  pl.BlockSpec((d, LANES), lambda i: (0, 0)),
                  pl.BlockSpec((d, LANES), lambda i: (0, 0)),
                  pl.BlockSpec((N_EXPERTS, 1), lambda i: (0, 0))],
        out_specs=pl.BlockSpec((N_EXPERTS, tm), lambda i: (0, i)),
        out_shape=jax.ShapeDtypeStruct((N_EXPERTS, t), F32),
        compiler_params=_params("arbitrary"),
        name="router",
    )(x, wr_hi, wr_lo, b_col)


def _moe_kernel(x_ref, gate_ref, wg_ref, wu_ref, wd_ref, g_ref, b_ref, o_ref, xbf_ref, acc_ref):
    e = pl.program_id(1)

    @pl.when(e == 0)
    def _():
        xbf_ref[...] = x_ref[...].astype(BF16)
        acc_ref[...] = jnp.zeros(acc_ref.shape, F32)

    xb = xbf_ref[...]
    gate = gate_ref[...]
    lane = lax.broadcasted_iota(jnp.int32, gate.shape, 1)
    gcol = jnp.sum(jnp.where(lane == e, gate, 0.0), axis=1, keepdims=True)
    hgate = jnp.dot(xb, wg_ref[0].astype(BF16), preferred_element_type=F32)
    hup = jnp.dot(xb, wu_ref[0].astype(BF16), preferred_element_type=F32)
    h = _silu(hgate) * hup * gcol
    acc_ref[...] += _bdot(h, wd_ref[0])

    @pl.when(e == N_EXPERTS - 1)
    def _():
        o_ref[...] = _ln(DN_ALPHA * x_ref[...] + acc_ref[...], g_ref[...], b_ref[...])


def _moe(x, gate, w_gate, w_up, w_down, g, b, *, tm):
    t, d = x.shape
    f = w_gate.shape[2]
    return pl.pallas_call(
        _moe_kernel,
        grid=(t // tm, N_EXPERTS),
        in_specs=[pl.BlockSpec((tm, d), lambda i, e: (i, 0)),
                  pl.BlockSpec((tm, N_EXPERTS), lambda i, e: (i, 0)),
                  pl.BlockSpec((1, d, f), lambda i, e: (e, 0, 0)),
                  pl.BlockSpec((1, d, f), lambda i, e: (e, 0, 0)),
                  pl.BlockSpec((1, f, d), lambda i, e: (e, 0, 0)),
                  pl.BlockSpec((1, d), lambda i, e: (0, 0)),
                  pl.BlockSpec((1, d), lambda i, e: (0, 0))],
        out_specs=pl.BlockSpec((tm, d), lambda i, e: (i, 0)),
        out_shape=jax.ShapeDtypeStruct((t, d), F32),
        scratch_shapes=[pltpu.VMEM((tm, d), BF16), pltpu.VMEM((tm, d), F32)],
        compiler_params=_params("arbitrary", "arbitrary"),
        name="moe_experts",
    )(x, gate, w_gate, w_up, w_down, g, b)


def _ple_kernel(x_ref, p_ref, wg_ref, bg_ref, wp_ref, o_ref):
    x = x_ref[...]
    gate = jax.nn.sigmoid(_bdot(x, wg_ref[...]) + bg_ref[...])
    o_ref[...] = x + gate * _bdot(p_ref[...], wp_ref[...])


def _ple(x, p, w_gate, b_gate, w_p, *, tm):
    t, d = x.shape
    pd = p.shape[1]
    return pl.pallas_call(
        _ple_kernel,
        grid=(t // tm,),
        in_specs=[pl.BlockSpec((tm, d), lambda i: (i, 0)),
                  pl.BlockSpec((tm, pd), lambda i: (i, 0)),
                  pl.BlockSpec((d, d), lambda i: (0, 0)),
                  pl.BlockSpec((1, d), lambda i: (0, 0)),
                  pl.BlockSpec((pd, d), lambda i: (0, 0))],
        out_specs=pl.BlockSpec((tm, d), lambda i: (i, 0)),
        out_shape=jax.ShapeDtypeStruct((t, d), F32),
        compiler_params=_params("arbitrary"),
        name="ple_gate",
    )(x, p, w_gate, b_gate, w_p)


def _row_tile(rows, want):
    tm = min(rows, want)
    assert rows % tm == 0
    return tm


POOL_TM = 512
MM_TM = 512
QKV_TM = 512
ATTN_TQ = 512
MOE_TM = 1024
ROUTER_TM = 1024
PLE_TM = 512
DECODE_PAGES_PER_STEP = 8


def kernel(x_prompt, x_sample, p_prompt, p_sample, state_pool, cache_k, cache_v, page_table, state_ssm, state_conv,
           ln_g, ln_b, w_pool_in, w_pool_grp, pool_scale, w_pool_out, w_diff_in, diff_lambda_q, diff_lambda_k,
           diff_norm_g, w_diff_out, w_ssd_in, ssd_conv_w, ssd_conv_b, ssd_dt_bias, ssd_a_log, ssd_d, ssd_norm_g,
           w_ssd_out, w_router, b_router, w_exp_gate, w_exp_up, w_exp_down, w_ple, w_ple_gate, b_ple_gate):
    bp, seq, d = x_prompt.shape
    bs, tdec, _ = x_sample.shape
    n_pages = page_table.shape[1]
    past = n_pages * cache_k.shape[2]
    tp = bp * seq
    ts = bs * tdec

    def to_tm(a):
        return jnp.swapaxes(a, 0, 1).reshape((ts,) + a.shape[2:])

    def from_tm(a):
        return jnp.swapaxes(a.reshape((tdec, bs) + a.shape[1:]), 0, 1)

    row = lambda v: v.reshape(1, -1)

    wr = jnp.pad(w_router, ((0, 0), (0, LANES - N_EXPERTS)))
    wr_hi = wr.astype(BF16)
    wr_lo = (wr - wr_hi.astype(F32)).astype(BF16)
    br_col = b_router.reshape(N_EXPERTS, 1)

    def finish(x1, i, p, tm_cap):
        t = x1.shape[0]
        gate = _router(x1, wr_hi, wr_lo, br_col, tm=_row_tile(t, min(ROUTER_TM, tm_cap))).T
        x2 = _moe(x1, gate, w_exp_gate[i], w_exp_up[i], w_exp_down[i], row(ln_g[i, 1]), row(ln_b[i, 1]),
                  tm=_row_tile(t, min(MOE_TM, tm_cap)))
        return _ple(x2, p, w_ple_gate[i].astype(BF16), row(b_ple_gate[i]), w_ple[i].astype(BF16),
                    tm=_row_tile(t, min(PLE_TM, tm_cap)))

    xp = x_prompt.reshape(tp, d)
    xs = to_tm(x_sample)
    pos_p = jnp.arange(seq, dtype=jnp.int32)
    pos_s = jnp.repeat(past + jnp.arange(tdec, dtype=jnp.int32), bs)
    outs = {k: [] for k in ("pool_p", "pool_s", "k_p", "v_p", "k_s", "v_s", "ssm_p", "conv_p", "ssm_s", "conv_s")}

    for i in range(DEPTH):
        kind, j = i % N_MIXERS, i // N_MIXERS
        g1, b1 = row(ln_g[i, 0]), row(ln_b[i, 0])
        if kind == 0:
            wts = (w_pool_in[j].astype(BF16), w_pool_grp[j].astype(BF16), row(pool_scale[j]),
                   w_pool_out[j].astype(BF16), g1, b1)
            x1p, newp = _pool_layer(xp.reshape(bp, seq, d), jnp.zeros((bp, POOL_BUF + 1, d), F32), *wts,
                                    tm=_row_tile(seq, POOL_TM), stride=1, pos0=0)
            pre_s = jnp.swapaxes(state_pool[j], 0, 1).reshape(1, POOL_BUF * bs, d)
            pre_s = jnp.pad(pre_s, ((0, 0), (bs, 0), (0, 0)))
            x1s, news = _pool_layer(xs.reshape(1, ts, d), pre_s, *wts, tm=ts, stride=bs, pos0=past)
            outs["pool_p"].append(newp)
            outs["pool_s"].append(jnp.swapaxes(news.reshape(POOL_BUF, bs, d), 0, 1))
            x1p = x1p.reshape(tp, d)
            x1s = x1s.reshape(ts, d)
        elif kind == 1:
            lam_init = 0.8 - 0.6 * math.exp(-0.3 * i)
            w_in = w_diff_in[j].astype(BF16)
            w_out = w_diff_out[j].astype(BF16)
            lq, lk, ng = diff_lambda_q[j], diff_lambda_k[j], row(diff_norm_g[j])
            cos_p, sin_p = _rope_tables(pos_p)
            q, k, v = _qkv_rope(xp.reshape(bp, seq, d), w_in, cos_p, sin_p, tm=_row_tile(seq, QKV_TM))
            o = _diff_attn_prompt(q, k, v, lq, lk, ng, lam_init, tq=_row_tile(seq, ATTN_TQ))
            x1p = _mm_res_ln(o.reshape(tp, d), w_out, xp, g1, b1, tm=_row_tile(tp, MM_TM))
            outs["k_p"].append(k.reshape(bp, seq, DIFF_HEADS, 2, DIFF_DH))
            outs["v_p"].append(v.reshape(bp, seq, DIFF_HEADS, DIFF_VD))

            cos_s, sin_s = _rope_tables(pos_s)
            q, k, v = _qkv_rope(xs.reshape(1, ts, d), w_in, cos_s, sin_s, tm=ts)
            q, k, v = from_tm(q[0]), from_tm(k[0]), from_tm(v[0])
            q5 = q.reshape(bs, tdec, DIFF_HEADS * 2, DIFF_DH)
            eye = jnp.eye(DIFF_HEADS * 2, dtype=F32)
            qbd = jnp.einsum('bthd,hg->bhtgd', q5, eye)
            qbd = jnp.pad(qbd, ((0, 0), (0, 0), (0, DEC_TPAD - tdec), (0, 0), (0, 0)))
            qbd = qbd.reshape(bs, DEC_ROWS, d)
            padk = ((0, 0), (0, PAGE_SIZE - tdec), (0, 0))
            o = _diff_attn_sample(qbd, cache_k.reshape(-1, PAGE_SIZE, d), cache_v.reshape(-1, PAGE_SIZE, d),
                                  page_table, jnp.pad(k, padk), jnp.pad(v, padk), lq, lk, ng, lam_init,
                                  pages_per_step=DECODE_PAGES_PER_STEP, page0=j * cache_k.shape[1])
            o = to_tm(o[:, :tdec])
            x1s = _mm_res_ln(o, w_out, xs, g1, b1, tm=ts)
            outs["k_s"].append(k.reshape(bs, tdec, DIFF_HEADS, 2, DIFF_DH))
            outs["v_s"].append(v.reshape(bs, tdec, DIFF_HEADS, DIFF_VD))
        else:
            w_in = w_ssd_in[j]
            w_z = w_in[:, :SSD_DI].astype(BF16)
            w_xbc = w_in[:, SSD_DI:SSD_DI + SSD_CONV_DIM].astype(BF16)
            w_dt = jnp.pad(w_in[:, SSD_DI + SSD_CONV_DIM:], ((0, 0), (0, LANES - SSD_H))).astype(BF16)
            w_out = w_ssd_out[j].astype(BF16)
            dtb_row = jnp.pad(ssd_dt_bias[j], (0, LANES - SSD_H)).reshape(1, LANES)
            alog_row = jnp.pad(ssd_a_log[j], (0, LANES - SSD_H)).reshape(1, LANES)
            dtb_col = ssd_dt_bias[j].reshape(SSD_H, 1)
            alog_col = ssd_a_log[j].reshape(SSD_H, 1)
            d_exp = jnp.repeat(ssd_d[j], SSD_P).reshape(1, SSD_DI)
            ng = row(ssd_norm_g[j])
            expand = (jnp.arange(SSD_DI)[None, :] // SSD_P == jnp.arange(LANES)[:, None]).astype(BF16)
            shared = (ssd_conv_w[j], row(ssd_conv_b[j]), dtb_row, dtb_col, alog_row, alog_col, d_exp, ng, expand)

            def in_proj(x, tm):
                z = _matmul(x, w_z, tm=tm, tn=1024)
                xbc = _matmul(x, w_xbc, tm=tm, tn=1024)
                dt = _matmul(x, w_dt, tm=tm, tn=LANES)
                return z, xbc, dt

            def hlast_to_state(hl, b):
                return jnp.transpose(hl.reshape(b, SSD_N, SSD_H, SSD_P), (0, 2, 3, 1))

            z, xbc, dt = in_proj(xp, _row_tile(tp, MM_TM))
            dtt = jnp.swapaxes(dt.reshape(bp, seq, LANES)[:, :, :SSD_H], 1, 2)
            y, hl = _ssd_scan(z.reshape(bp, seq, SSD_DI), xbc.reshape(bp, seq, SSD_CONV_DIM),
                              dt.reshape(bp, seq, LANES), dtt, jnp.zeros((bp, CONV_PAD, SSD_CONV_DIM), F32),
                              *shared, None, valid_len=None)
            x1p = _mm_res_ln(y.reshape(tp, SSD_DI), w_out, xp, g1, b1, tm=_row_tile(tp, MM_TM))
            outs["ssm_p"].append(hlast_to_state(hl, bp))
            outs["conv_p"].append(xbc.reshape(bp, seq, SSD_CONV_DIM)[:, seq - (SSD_CONV - 1):])

            z, xbc, dt = in_proj(xs, ts)
            z, xbc, dt = from_tm(z), from_tm(xbc), from_tm(dt)
            padt = ((0, 0), (0, SSD_CHUNK - tdec), (0, 0))
            dtp = jnp.pad(dt, padt)
            cpre = jnp.pad(state_conv[j], ((0, 0), (CONV_PAD - (SSD_CONV - 1), 0), (0, 0)))
            h0t = jnp.transpose(state_ssm[j], (0, 3, 1, 2)).reshape(bs, SSD_N, SSD_DI)
            y, hl = _ssd_scan(jnp.pad(z, padt), jnp.pad(xbc, padt), dtp, jnp.swapaxes(dtp[:, :, :SSD_H], 1, 2),
                              cpre, *shared, h0t, valid_len=tdec)
            x1s = _mm_res_ln(to_tm(y[:, :tdec]), w_out, xs, g1, b1, tm=ts)
            outs["ssm_s"].append(hlast_to_state(hl, bs))
            ext = jnp.concatenate([state_conv[j], xbc], axis=1)
            outs["conv_s"].append(ext[:, -(SSD_CONV - 1):])

        xp = finish(x1p, i, p_prompt[i].reshape(tp, PLE_DIM), tp)
        xs = finish(x1s, i, to_tm(p_sample[i]), ts)

    st = lambda name: jnp.stack(outs[name])
    return (xp.reshape(bp, seq, d), from_tm(xs), st("pool_p"), st("pool_s"), st("k_p"), st("v_p"), st("k_s"),
            st("v_s"), st("ssm_p"), st("conv_p"), st("ssm_s"), st("conv_s"))
```

```python
import functools
import math

import jax
import jax.numpy as jnp
from jax import lax
from jax.experimental import pallas as pl
from jax.experimental.pallas import tpu as pltpu

F32 = jnp.float32
BF16 = jnp.bfloat16

D_MODEL = 1024
DEPTH = 4
N_MIXERS = 3
PAGE_SIZE = 128

POOL_WINDOWS = (2, 4, 8, 16)
POOL_GD = D_MODEL // len(POOL_WINDOWS)
POOL_BUF = max(POOL_WINDOWS) - 1

DIFF_DH = 64
DIFF_VD = 2 * DIFF_DH
DIFF_HEADS = D_MODEL // DIFF_VD
ROPE_THETA = 10000.0

SSD_DI = 2 * D_MODEL
SSD_P = 64
SSD_H = SSD_DI // SSD_P
SSD_G = 4
SSD_N = 128
SSD_CONV = 4
SSD_CONV_DIM = SSD_DI + 2 * SSD_G * SSD_N
SSD_CHUNK = 128

N_EXPERTS = 16
N_EGROUPS = 4
EXP_PER_GROUP = N_EXPERTS // N_EGROUPS
D_FF_EXPERT = D_MODEL // 2

PLE_DIM = 256
DN_ALPHA = (2 * DEPTH) ** 0.25
NORM_EPS = 1e-5

LANES = 128
SUBLANES = 8
VMEM_LIMIT = 56 * 1024 * 1024


def _params(*sem):
    return pltpu.CompilerParams(dimension_semantics=sem, vmem_limit_bytes=VMEM_LIMIT)


def _bdot(a, b):
    return jnp.dot(a.astype(BF16), b.astype(BF16), preferred_element_type=F32)


def _bdot_nt(a, b):
    return lax.dot_general(a.astype(BF16), b.astype(BF16), (((1,), (1,)), ((), ())),
                           preferred_element_type=F32)


def _split3(x):
    hi = x.astype(BF16)
    r = x - hi.astype(F32)
    mid = r.astype(BF16)
    lo = (r - mid.astype(F32)).astype(BF16)
    return hi, mid, lo


def _dot_exact_rhs(x, m):
    hi, mid, lo = _split3(x)
    return (jnp.dot(hi, m, preferred_element_type=F32) + jnp.dot(mid, m, preferred_element_type=F32)
            + jnp.dot(lo, m, preferred_element_type=F32))


def _dot_exact_lhs(m, x):
    hi, mid, lo = _split3(x)
    return (jnp.dot(m, hi, preferred_element_type=F32) + jnp.dot(m, mid, preferred_element_type=F32)
            + jnp.dot(m, lo, preferred_element_type=F32))


def _ln(v, g, b):
    mu = jnp.mean(v, axis=-1, keepdims=True)
    c = v - mu
    var = jnp.mean(c * c, axis=-1, keepdims=True)
    return c * lax.rsqrt(var + NORM_EPS) * g + b


def _silu(v):
    return v * jax.nn.sigmoid(v)


def _softplus(v):
    return jnp.maximum(v, 0.0) + jnp.log(1.0 + jnp.exp(-jnp.abs(v)))


def _pool_kernel(x_ref, pre_ref, win_ref, wgrp_ref, scale_ref, wout_ref, g_ref, b_ref,
                 y_ref, newp_ref, ext_ref, *, tm, stride, pos0, n_tiles):
    i = pl.program_id(1)
    pre_rows = (POOL_BUF + 1) * stride

    @pl.when(i == 0)
    def _():
        ext_ref[0:pre_rows, :] = pre_ref[0]

    @pl.when(i > 0)
    def _():
        ext_ref[0:pre_rows, :] = ext_ref[tm:tm + pre_rows, :]

    x = x_ref[0]
    u = _bdot(x, win_ref[...])
    ext_ref[pre_rows:pre_rows + tm, :] = u
    row = lax.broadcasted_iota(jnp.int32, (tm, 1), 0)
    pos = pos0 + (i * tm + row) // stride
    mixed = []
    for gi, w in enumerate(POOL_WINDOWS):
        c0 = gi * POOL_GD
        ug = u[:, c0:c0 + POOL_GD]
        s = ug
        for k in range(1, w):
            r0 = pre_rows - k * stride
            s = s + ext_ref[r0:r0 + tm, c0:c0 + POOL_GD]
        cnt = jnp.minimum(w, pos + 1).astype(F32)
        pooled = s / cnt - ug
        mixed.append(_bdot(pooled, wgrp_ref[gi]))
    mixed = jnp.concatenate(mixed, axis=1) * scale_ref[...]
    mix = _bdot(mixed, wout_ref[...])
    y_ref[0] = _ln(DN_ALPHA * x + mix, g_ref[...], b_ref[...])

    @pl.when(i == n_tiles - 1)
    def _():
        newp_ref[0] = ext_ref[tm + stride:tm + pre_rows, :]


def _pool_layer(x, prefix, w_in, w_grp, scale, w_out, g, b, *, tm, stride, pos0):
    bsz, rows, d = x.shape
    n_tiles = rows // tm
    pre_rows = (POOL_BUF + 1) * stride
    kern = functools.partial(_pool_kernel, tm=tm, stride=stride, pos0=pos0, n_tiles=n_tiles)
    full = lambda *shape: pl.BlockSpec(shape, lambda bi, i: (0,) * len(shape))
    return pl.pallas_call(
        kern,
        grid=(bsz, n_tiles),
        in_specs=[
            pl.BlockSpec((1, tm, d), lambda bi, i: (bi, i, 0)),
            pl.BlockSpec((1, pre_rows, d), lambda bi, i: (bi, 0, 0)),
            full(d, d), full(len(POOL_WINDOWS), POOL_GD, POOL_GD), full(1, d), full(d, d),
            full(1, d), full(1, d),
        ],
        out_specs=[
            pl.BlockSpec((1, tm, d), lambda bi, i: (bi, i, 0)),
            pl.BlockSpec((1, POOL_BUF * stride, d), lambda bi, i: (bi, 0, 0)),
        ],
        out_shape=[jax.ShapeDtypeStruct((bsz, rows, d), F32),
                   jax.ShapeDtypeStruct((bsz, POOL_BUF * stride, d), F32)],
        scratch_shapes=[pltpu.VMEM((pre_rows + tm, d), F32)],
        compiler_params=_params("arbitrary", "arbitrary"),
        name="pool_layer",
    )(x, prefix, w_in, w_grp, scale, w_out, g, b)


def _mm_kernel(x_ref, w_ref, o_ref):
    o_ref[...] = _bdot(x_ref[...], w_ref[...]).astype(o_ref.dtype)


def _matmul(x, w, *, tm, tn, out_dtype=F32):
    t, k = x.shape
    n = w.shape[1]
    return pl.pallas_call(
        _mm_kernel,
        grid=(n // tn, t // tm),
        in_specs=[pl.BlockSpec((tm, k), lambda j, i: (i, 0)),
                  pl.BlockSpec((k, tn), lambda j, i: (0, j))],
        out_specs=pl.BlockSpec((tm, tn), lambda j, i: (i, j)),
        out_shape=jax.ShapeDtypeStruct((t, n), out_dtype),
        compiler_params=_params("arbitrary", "arbitrary"),
        name="matmul",
    )(x, w)


def _mm_res_ln_kernel(a_ref, w_ref, x_ref, g_ref, b_ref, o_ref):
    mix = _bdot(a_ref[...], w_ref[...])
    o_ref[...] = _ln(DN_ALPHA * x_ref[...] + mix, g_ref[...], b_ref[...])


def _mm_res_ln(a, w, x, g, b, *, tm):
    t, k = a.shape
    d = w.shape[1]
    return pl.pallas_call(
        _mm_res_ln_kernel,
        grid=(t // tm,),
        in_specs=[pl.BlockSpec((tm, k), lambda i: (i, 0)),
                  pl.BlockSpec((k, d), lambda i: (0, 0)),
                  pl.BlockSpec((tm, d), lambda i: (i, 0)),
                  pl.BlockSpec((1, d), lambda i: (0, 0)),
                  pl.BlockSpec((1, d), lambda i: (0, 0))],
        out_specs=pl.BlockSpec((tm, d), lambda i: (i, 0)),
        out_shape=jax.ShapeDtypeStruct((t, d), F32),
        compiler_params=_params("arbitrary"),
        name="mm_res_ln",
    )(a, w, x, g, b)


def _qkv_kernel(x_ref, w_ref, cos_ref, sin_ref, q_ref, k_ref, v_ref, *, tm):
    x = x_ref[0].astype(BF16)
    cos = cos_ref[...]
    sin = sin_ref[...]
    lane = lax.broadcasted_iota(jnp.int32, (tm, LANES), 1)
    first_half = (lane % DIFF_DH) < (DIFF_DH // 2)

    def rope(y):
        outs = []
        for c in range(D_MODEL // LANES):
            yc = y[:, c * LANES:(c + 1) * LANES]
            sw = jnp.where(first_half, pltpu.roll(yc, LANES - DIFF_DH // 2, 1),
                           pltpu.roll(yc, DIFF_DH // 2, 1))
            outs.append(yc * cos + sw * sin)
        return jnp.concatenate(outs, axis=1)

    q_ref[0] = rope(jnp.dot(x, w_ref[:, 0:D_MODEL], preferred_element_type=F32))
    k_ref[0] = rope(jnp.dot(x, w_ref[:, D_MODEL:2 * D_MODEL], preferred_element_type=F32))
    v_ref[0] = jnp.dot(x, w_ref[:, 2 * D_MODEL:3 * D_MODEL], preferred_element_type=F32)


def _rope_tables(pos):
    half = DIFF_DH // 2
    inv = ROPE_THETA ** (-jnp.arange(half, dtype=F32) * 2.0 / DIFF_DH)
    ang = pos.astype(F32)[:, None] * inv[None, :]
    cos, sin = jnp.cos(ang), jnp.sin(ang)
    reps = LANES // DIFF_DH
    cos_t = jnp.tile(jnp.concatenate([cos, cos], axis=1), (1, reps))
    sin_t = jnp.tile(jnp.concatenate([-sin, sin], axis=1), (1, reps))
    return cos_t, sin_t


def _qkv_rope(x, w, cos_t, sin_t, *, tm):
    bsz, rows, d = x.shape
    kern = functools.partial(_qkv_kernel, tm=tm)
    spec = pl.BlockSpec((1, tm, d), lambda bi, i: (bi, i, 0))
    tab = pl.BlockSpec((tm, LANES), lambda bi, i: (i, 0))
    shp = jax.ShapeDtypeStruct((bsz, rows, d), F32)
    return pl.pallas_call(
        kern,
        grid=(bsz, rows // tm),
        in_specs=[spec, pl.BlockSpec((d, 3 * d), lambda bi, i: (0, 0)), tab, tab],
        out_specs=[spec, spec, spec],
        out_shape=[shp, shp, shp],
        compiler_params=_params("arbitrary", "arbitrary"),
        name="qkv_rope",
    )(x, w, cos_t, sin_t)


def _diff_lambda(lq_ref, lk_ref, lam_init):
    dots = jnp.sum(lq_ref[...] * lk_ref[...], axis=-1, keepdims=True)
    e = jnp.exp(dots)
    return e[0:1, :] - e[1:2, :] + lam_init


def _head_norm(o, ng, lam_init):
    ms = jnp.mean(o * o, axis=-1, keepdims=True)
    return o * lax.rsqrt(ms + NORM_EPS) * ng * (1.0 - lam_init)


def _attn_kernel(q_ref, k_ref, v_ref, lq_ref, lk_ref, ng_ref, o_ref, *, tq, lam_init):
    qi = pl.program_id(2)
    q = q_ref[0] * (DIFF_DH ** -0.5)
    q1 = q[:, :DIFF_DH].astype(BF16)
    q2 = q[:, DIFF_DH:].astype(BF16)
    ones = jnp.ones((tq, DIFF_VD), BF16)

    def scores(kb):
        start = pl.multiple_of(kb * tq, tq)
        k = k_ref[0, pl.ds(start, tq), :]
        v1 = jnp.concatenate([v_ref[0, pl.ds(start, tq), :].astype(BF16), ones], axis=1)
        s = jnp.concatenate([_bdot_nt(q1, k[:, :DIFF_DH]), _bdot_nt(q2, k[:, DIFF_DH:])], axis=0)
        return s, v1

    def update(s, v1, m, a):
        mn = jnp.maximum(m, jnp.max(s, axis=1, keepdims=True))
        p = jnp.exp(s - mn).astype(BF16)
        a = jnp.exp(m - mn) * a + jnp.dot(p, v1, preferred_element_type=F32)
        return mn, a

    def body(kb, carry):
        s, v1 = scores(kb)
        return update(s, v1, *carry)

    init = (jnp.full((2 * tq, 1), -jnp.inf, F32), jnp.zeros((2 * tq, 2 * DIFF_VD), F32))
    m, a = lax.fori_loop(0, qi, body, init)
    s, v1 = scores(qi)
    row = lax.broadcasted_iota(jnp.int32, (2 * tq, tq), 0) % tq
    col = lax.broadcasted_iota(jnp.int32, (2 * tq, tq), 1)
    m, a = update(jnp.where(col <= row, s, -jnp.inf), v1, m, a)
    lam = _diff_lambda(lq_ref, lk_ref, lam_init)
    o = (a[:tq, :DIFF_VD] / a[:tq, DIFF_VD:DIFF_VD + 1]
         - lam * (a[tq:, :DIFF_VD] / a[tq:, DIFF_VD:DIFF_VD + 1]))
    o_ref[0] = _head_norm(o, ng_ref[...], lam_init).astype(o_ref.dtype)


def _diff_attn_prompt(q, k, v, lq, lk, ng, lam_init, *, tq):
    bsz, seq, d = q.shape
    kern = functools.partial(_attn_kernel, tq=tq, lam_init=lam_init)
    small = lambda *shape: pl.BlockSpec(shape, lambda bi, h, i: (0,) * len(shape))
    return pl.pallas_call(
        kern,
        grid=(bsz, DIFF_HEADS, seq // tq),
        in_specs=[pl.BlockSpec((1, tq, DIFF_VD), lambda bi, h, i: (bi, i, h)),
                  pl.BlockSpec((1, seq, DIFF_VD), lambda bi, h, i: (bi, 0, h)),
                  pl.BlockSpec((1, seq, DIFF_VD), lambda bi, h, i: (bi, 0, h)),
                  small(2, DIFF_DH), small(2, DIFF_DH), small(1, DIFF_VD)],
        out_specs=pl.BlockSpec((1, tq, DIFF_VD), lambda bi, h, i: (bi, i, h)),
        out_shape=jax.ShapeDtypeStruct((bsz, seq, d), BF16),
        compiler_params=_params("arbitrary", "arbitrary", "arbitrary"),
        name="diff_attn_prompt",
    )(q, k, v, lq, lk, ng)


DEC_TPAD = SUBLANES
DEC_ROWS = DIFF_HEADS * 2 * DEC_TPAD


def _decode_kernel(pt_ref, q_ref, *refs, pages_per_step, n_steps, lam_init):
    g = pages_per_step
    k_refs = refs[:g]
    v_refs = refs[g:2 * g]
    knew_ref, vnew_ref, rep_ref, lq_ref, lk_ref, ng_ref, o_ref, m_ref, l_ref, acc_ref = refs[2 * g:]
    s = pl.program_id(1)
    scale = DIFF_DH ** -0.5
    rep = rep_ref[...]
    row_head = lax.broadcasted_iota(jnp.int32, (DEC_ROWS, PAGE_SIZE * DIFF_HEADS), 0) // (2 * DEC_TPAD)
    col_head = lax.broadcasted_iota(jnp.int32, (DEC_ROWS, PAGE_SIZE * DIFF_HEADS), 1) % DIFF_HEADS
    own_head = row_head == col_head

    @pl.when(s == 0)
    def _():
        m_ref[...] = jnp.full(m_ref.shape, -jnp.inf, F32)
        l_ref[...] = jnp.zeros(l_ref.shape, F32)
        acc_ref[...] = jnp.zeros(acc_ref.shape, F32)

    q = (q_ref[0] * scale).astype(BF16)

    def block(ks, vs, mask):
        sc = jnp.concatenate([_bdot(q, k[0]) for k in ks], axis=1)
        if mask is not None:
            sc = jnp.where(mask, sc, -jnp.inf)
        m_old = m_ref[...]
        mn = jnp.maximum(m_old, jnp.max(sc, axis=1, keepdims=True))
        alpha = jnp.exp(m_old - mn)
        p = jnp.exp(sc - mn)
        pv = None
        for gi in range(len(vs)):
            pe = jnp.dot(p[:, gi * PAGE_SIZE:(gi + 1) * PAGE_SIZE].astype(BF16), rep, preferred_element_type=F32)
            part = _bdot(jnp.where(own_head, pe, 0.0), vs[gi][0])
            pv = part if pv is None else pv + part
        l_ref[...] = alpha * l_ref[...] + jnp.sum(p, axis=1, keepdims=True)
        acc_ref[...] = alpha * acc_ref[...] + pv
        m_ref[...] = mn

    block(k_refs, v_refs, None)

    @pl.when(s == n_steps - 1)
    def _():
        t_row = lax.broadcasted_iota(jnp.int32, (DEC_ROWS, PAGE_SIZE), 0) % DEC_TPAD
        key = lax.broadcasted_iota(jnp.int32, (DEC_ROWS, PAGE_SIZE), 1)
        block([knew_ref], [vnew_ref], key <= t_row)
        lam = _diff_lambda(lq_ref, lk_ref, lam_init)
        ng = ng_ref[...]
        for h in range(DIFF_HEADS):
            r1 = h * 2 * DEC_TPAD
            r2 = r1 + DEC_TPAD
            cols = slice(h * DIFF_VD, (h + 1) * DIFF_VD)
            a1 = acc_ref[r1:r1 + DEC_TPAD, :] / l_ref[r1:r1 + DEC_TPAD, :]
            a2 = acc_ref[r2:r2 + DEC_TPAD, :] / l_ref[r2:r2 + DEC_TPAD, :]
            o_ref[0, :, cols] = _head_norm(a1 - lam * a2, ng, lam_init).astype(o_ref.dtype)


def _diff_attn_sample(qbd, cache_k, cache_v, page_table, knew, vnew, lq, lk, ng, lam_init,
                      *, pages_per_step, page0):
    bsz = qbd.shape[0]
    n_pages = page_table.shape[1]
    g = pages_per_step
    n_steps = n_pages // g
    kern = functools.partial(_decode_kernel, pages_per_step=g, n_steps=n_steps, lam_init=lam_init)
    rep = (jnp.arange(PAGE_SIZE * DIFF_HEADS)[None, :] // DIFF_HEADS == jnp.arange(PAGE_SIZE)[:, None]).astype(BF16)

    def page_spec(gi):
        return pl.BlockSpec((1, D_MODEL, PAGE_SIZE), lambda b, s, pt: (page0 + pt[b, s * g + gi], 0, 0))

    per_b = lambda rows, cols: pl.BlockSpec((1, rows, cols), lambda b, s, pt: (b, 0, 0))
    small = lambda *shape: pl.BlockSpec(shape, lambda b, s, pt: (0,) * len(shape))
    grid_spec = pltpu.PrefetchScalarGridSpec(
        num_scalar_prefetch=1,
        grid=(bsz, n_steps),
        in_specs=([per_b(DEC_ROWS, D_MODEL)] + [page_spec(gi) for gi in range(g)] + [page_spec(gi) for gi in range(g)]
                  + [per_b(D_MODEL, PAGE_SIZE), per_b(D_MODEL, PAGE_SIZE), small(PAGE_SIZE, PAGE_SIZE * DIFF_HEADS),
                     small(2, DIFF_DH), small(2, DIFF_DH), small(1, DIFF_VD)]),
        out_specs=per_b(DEC_TPAD, D_MODEL),
        scratch_shapes=[pltpu.VMEM((DEC_ROWS, 1), F32), pltpu.VMEM((DEC_ROWS, 1), F32),
                        pltpu.VMEM((DEC_ROWS, DIFF_VD), F32)],
    )
    return pl.pallas_call(
        kern,
        grid_spec=grid_spec,
        out_shape=jax.ShapeDtypeStruct((bsz, DEC_TPAD, D_MODEL), BF16),
        compiler_params=_params("arbitrary", "arbitrary"),
        name="diff_attn_sample",
    )(page_table, qbd, *([cache_k] * g), *([cache_v] * g), knew, vnew, rep, lq, lk, ng)


CONV_PAD = SUBLANES


def _ssd_kernel(*refs, q, n_chunks, valid_len, has_h0):
    if has_h0:
        (z_ref, xbc_ref, dt_ref, dtt_ref, cpre_ref, cw_ref, cb_ref, dtb_row_ref, dtb_col_ref,
         alog_row_ref, alog_col_ref, dexp_ref, ng_ref, expand_ref, h0_ref,
         y_ref, hl_ref, s_ref, cext_ref) = refs
    else:
        (z_ref, xbc_ref, dt_ref, dtt_ref, cpre_ref, cw_ref, cb_ref, dtb_row_ref, dtb_col_ref,
         alog_row_ref, alog_col_ref, dexp_ref, ng_ref, expand_ref,
         y_ref, hl_ref, s_ref, cext_ref) = refs
        h0_ref = None
    c = pl.program_id(1)

    @pl.when(c == 0)
    def _():
        if has_h0:
            s_ref[...] = h0_ref[0]
        else:
            s_ref[...] = jnp.zeros(s_ref.shape, F32)
        cext_ref[0:CONV_PAD, :] = cpre_ref[0]

    @pl.when(c > 0)
    def _():
        cext_ref[0:CONV_PAD, :] = cext_ref[q:q + CONV_PAD, :]

    cext_ref[CONV_PAD:CONV_PAD + q, :] = xbc_ref[0]
    xbc = cb_ref[...]
    for kk in range(SSD_CONV):
        r0 = CONV_PAD - (SSD_CONV - 1) + kk
        xbc = xbc + cext_ref[r0:r0 + q, :] * cw_ref[kk:kk + 1, :]
    xbc = _silu(xbc)
    xs = xbc[:, :SSD_DI]
    bm = xbc[:, SSD_DI:SSD_DI + SSD_G * SSD_N]
    cm = xbc[:, SSD_DI + SSD_G * SSD_N:]

    dtc = _softplus(dt_ref[0] + dtb_row_ref[...])
    dtr = _softplus(dtt_ref[0] + dtb_col_ref[...])
    if valid_len is not None:
        t_col = c * q + lax.broadcasted_iota(jnp.int32, (q, LANES), 0)
        t_row = c * q + lax.broadcasted_iota(jnp.int32, (SSD_H, q), 1)
        dtc = jnp.where(t_col < valid_len, dtc, 0.0)
        dtr = jnp.where(t_row < valid_len, dtr, 0.0)
    ii = lax.broadcasted_iota(jnp.int32, (q, q), 0)
    jj = lax.broadcasted_iota(jnp.int32, (q, q), 1)
    causal = jj <= ii
    tri = jnp.where(causal, 1.0, 0.0).astype(BF16)
    tri_t = jnp.where(ii <= jj, 1.0, 0.0).astype(BF16)
    acum_c = _dot_exact_lhs(tri, dtc * (-jnp.exp(alog_row_ref[...])))
    acum_r = _dot_exact_rhs(dtr * (-jnp.exp(alog_col_ref[...])), tri_t)
    total = acum_c[q - 1:q, :]
    expand = expand_ref[...]
    dte_x = _dot_exact_rhs(jnp.exp(total - acum_c) * dtc, expand)
    eac_x = _dot_exact_rhs(jnp.exp(acum_c), expand)
    cdec_x = _dot_exact_rhs(jnp.broadcast_to(jnp.exp(total), (SUBLANES, LANES)), expand)[0:1, :]

    gw = SSD_DI // SSD_G
    hg = SSD_H // SSD_G
    lane = lax.broadcasted_iota(jnp.int32, (q, 2 * SSD_P), 1)
    ys = []
    for g in range(SSD_G):
        gc = slice(g * gw, (g + 1) * gw)
        bg = bm[:, g * SSD_N:(g + 1) * SSD_N]
        cg = cm[:, g * SSD_N:(g + 1) * SSD_N]
        cbm = _bdot_nt(cg, bg)
        xg = xs[:, gc]
        s_old = s_ref[:, gc]
        y_g = _bdot(cg, s_old) * eac_x[:, gc]
        s_ref[:, gc] = s_old * cdec_x[:, gc] + _bdot(bg.T, xg * dte_x[:, gc])
        pairs = []
        for hp in range(hg // 2):
            wts = []
            for hh in range(2):
                h = g * hg + 2 * hp + hh
                seg = acum_c[:, h:h + 1] - acum_r[h:h + 1, :]
                dec = jnp.exp(jnp.where(causal, seg, -jnp.inf))
                wts.append((cbm * dec * dtr[h:h + 1, :]).astype(BF16))
            xp = xg[:, hp * 2 * SSD_P:(hp + 1) * 2 * SSD_P]
            xbd = jnp.concatenate([jnp.where(lane < SSD_P, xp, 0.0), jnp.where(lane >= SSD_P, xp, 0.0)],
                                  axis=0).astype(BF16)
            pairs.append(jnp.dot(jnp.concatenate(wts, axis=1), xbd, preferred_element_type=F32))
        y_g = y_g + jnp.concatenate(pairs, axis=1) + xg * dexp_ref[:, gc]
        zg = z_ref[0, :, gc]
        y_g = y_g * _silu(zg)
        y_g = y_g * lax.rsqrt(jnp.mean(y_g * y_g, axis=-1, keepdims=True) + NORM_EPS)
        ys.append(y_g)
    y_ref[0] = (jnp.concatenate(ys, axis=1) * ng_ref[...]).astype(y_ref.dtype)

    @pl.when(c == n_chunks - 1)
    def _():
        hl_ref[0] = s_ref[...]


def _ssd_scan(z, xbc, dt, dtt, cpre, conv_w, conv_b, dtb_row, dtb_col, alog_row, alog_col, d_exp, ng,
              expand, h0t, *, valid_len):
    bsz, seq, _ = z.shape
    q = SSD_CHUNK
    n_chunks = seq // q
    has_h0 = h0t is not None
    kern = functools.partial(_ssd_kernel, q=q, n_chunks=n_chunks, valid_len=valid_len, has_h0=has_h0)
    full = lambda *shape: pl.BlockSpec(shape, lambda b, c: (0,) * len(shape))
    in_specs = [
        pl.BlockSpec((1, q, SSD_DI), lambda b, c: (b, c, 0)),
        pl.BlockSpec((1, q, SSD_CONV_DIM), lambda b, c: (b, c, 0)),
        pl.BlockSpec((1, q, LANES), lambda b, c: (b, c, 0)),
        pl.BlockSpec((1, SSD_H, q), lambda b, c: (b, 0, c)),
        pl.BlockSpec((1, CONV_PAD, SSD_CONV_DIM), lambda b, c: (b, 0, 0)),
        full(SSD_CONV, SSD_CONV_DIM), full(1, SSD_CONV_DIM),
        full(1, LANES), full(SSD_H, 1), full(1, LANES), full(SSD_H, 1),
        full(1, SSD_DI), full(1, SSD_DI), full(LANES, SSD_DI),
    ]
    args = [z, xbc, dt, dtt, cpre, conv_w, conv_b, dtb_row, dtb_col, alog_row, alog_col, d_exp, ng, expand]
    if has_h0:
        in_specs.append(pl.BlockSpec((1, SSD_N, SSD_DI), lambda b, c: (b, 0, 0)))
        args.append(h0t)
    return pl.pallas_call(
        kern,
        grid=(bsz, n_chunks),
        in_specs=in_specs,
        out_specs=[pl.BlockSpec((1, q, SSD_DI), lambda b, c: (b, c, 0)),
                   pl.BlockSpec((1, SSD_N, SSD_DI), lambda b, c: (b, 0, 0))],
        out_shape=[jax.ShapeDtypeStruct((bsz, seq, SSD_DI), BF16),
                   jax.ShapeDtypeStruct((bsz, SSD_N, SSD_DI), F32)],
        scratch_shapes=[pltpu.VMEM((SSD_N, SSD_DI), F32), pltpu.VMEM((CONV_PAD + q, SSD_CONV_DIM), F32)],
        compiler_params=_params("arbitrary", "arbitrary"),
        name="ssd_scan",
    )(*args)


def _first_max(vals):
    m = vals[0]
    for v in vals[1:]:
        m = jnp.maximum(m, v)
    flags = []
    taken = jnp.zeros_like(m)
    for v in vals:
        f = jnp.where(v == m, 1.0 - taken, 0.0)
        taken = taken + f
        flags.append(f)
    return m, flags


def _sum_rows(rows):
    tot = rows[0]
    for r in rows[1:]:
        tot = tot + r
    return tot


def _router_rows(x, wh_ref, wl_ref, b_ref):
    xh = x.astype(BF16)
    xl = (x - xh.astype(F32)).astype(BF16)
    logits = (jnp.dot(xh, wh_ref[...], preferred_element_type=F32)
              + jnp.dot(xh, wl_ref[...], preferred_element_type=F32)
              + jnp.dot(xl, wh_ref[...], preferred_element_type=F32))
    lt = logits.T
    rows = [lt[e:e + 1, :] for e in range(N_EXPERTS)]
    mx = rows[0]
    for r in rows[1:]:
        mx = jnp.maximum(mx, r)
    ex = [jnp.exp(r - mx) for r in rows]
    den = _sum_rows(ex)
    probs = [r / den for r in ex]
    sel = [probs[e] + b_ref[e:e + 1, :] for e in range(N_EXPERTS)]
    scores, chosen = [], []
    for g in range(N_EGROUPS):
        v = sel[g * EXP_PER_GROUP:(g + 1) * EXP_PER_GROUP]
        m1, f1 = _first_max(v)
        rest = [jnp.where(f > 0.0, -jnp.inf, vi) for f, vi in zip(f1, v)]
        m2, f2 = _first_max(rest)
        scores.append(m1 + m2)
        chosen.extend([a + b for a, b in zip(f1, f2)])
    return probs, chosen, scores


def _router_kernel(x_ref, wh_ref, wl_ref, b_ref, o_ref):
    probs, chosen, scores = _router_rows(x_ref[...], wh_ref, wl_ref, b_ref)
    _, gflag = _first_max(scores)
    gates = [probs[e] * chosen[e] * gflag[e // EXP_PER_GROUP] for e in range(N_EXPERTS)]
    tot = _sum_rows(gates)
    for e in range(N_EXPERTS):
        o_ref[e:e + 1, :] = gates[e] / tot


def _route_group_kernel(x_ref, wh_ref, wl_ref, b_ref, o_ref):
    _, _, scores = _router_rows(x_ref[...], wh_ref, wl_ref, b_ref)
    _, gflag = _first_max(scores)
    grp = _sum_rows([gflag[g] * float(g) for g in range(1, N_EGROUPS)])
    o_ref[...] = grp.astype(jnp.int32)


def _route_group(x, wr_hi, wr_lo, b_col, *, tm):
    t, d = x.shape
    return pl.pallas_call(
        _route_group_kernel,
        grid=(t // tm,),
        in_specs=[pl.BlockSpec((tm, d), lambda i: (i, 0)),
                  pl.BlockSpec((d, LANES), lambda i: (0, 0)),
                  pl.BlockSpec((d, LANES), lambda i: (0, 0)),
                  pl.BlockSpec((N_EXPERTS, 1), lambda i: (0, 0))],
        out_specs=pl.BlockSpec((1, tm), lambda i: (0, i)),
        out_shape=jax.ShapeDtypeStruct((1, t), jnp.int32),
        compiler_params=_params("arbitrary"),
        name="route_group",
    )(x, wr_hi, wr_lo, b_col)


PERMUTE_WINDOW = 256
PERMUTE_UNROLL = 8


def _permute_kernel(sidx_ref, didx_ref, src_ref, *rest, n, src_identity, dst_identity):
    dst_ref, sem = rest[-2:]

    def copy(i):
        s = i if src_identity else sidx_ref[i]
        d = i if dst_identity else didx_ref[i]
        return pltpu.make_async_copy(src_ref.at[pl.ds(s, 1)], dst_ref.at[pl.ds(d, 1)], sem)

    window = min(PERMUTE_WINDOW, n)

    def body(i, c):
        copy(i).start()

        @pl.when(i >= window)
        def _():
            copy(i - window).wait()
        return c

    def drain(i, c):
        copy(i).wait()
        return c

    lax.fori_loop(0, n, body, 0, unroll=PERMUTE_UNROLL)
    lax.fori_loop(n - window, n, drain, 0, unroll=PERMUTE_UNROLL)


def _permute_rows(src, idx, *, out_rows, scatter):
    n = idx.shape[0]
    d = src.shape[1]
    kern = functools.partial(_permute_kernel, n=n, src_identity=scatter, dst_identity=not scatter)
    any_spec = pl.BlockSpec(memory_space=pl.ANY)
    args = [idx, idx, src]
    in_specs = [any_spec]
    aliases = {}
    if scatter:
        args.append(jnp.zeros((out_rows, d), src.dtype))
        in_specs.append(any_spec)
        aliases = {3: 0}
    return pl.pallas_call(
        kern,
        grid_spec=pltpu.PrefetchScalarGridSpec(
            num_scalar_prefetch=2, grid=(1,), in_specs=in_specs, out_specs=any_spec,
            scratch_shapes=[pltpu.SemaphoreType.DMA(())]),
        out_shape=jax.ShapeDtypeStruct((out_rows, d), src.dtype),
        input_output_aliases=aliases,
        compiler_params=_params("arbitrary"),
        name="permute_rows",
    )(*args)


def _moe_group_kernel(tg_ref, nu_ref, x_ref, wh_ref, wl_ref, b_ref, wg_ref, wu_ref, wd_ref, o_ref,
                      xbf_ref, gcol_ref, acc_ref, *, tr):
    t = pl.program_id(0)
    e = pl.program_id(1)
    used = t < nu_ref[0]

    @pl.when(jnp.logical_and(used, e == 0))
    def _():
        x = x_ref[...]
        xbf_ref[...] = x.astype(BF16)
        probs, chosen, _ = _router_rows(x, wh_ref, wl_ref, b_ref)
        tg = tg_ref[t]
        local = []
        for le in range(EXP_PER_GROUP):
            cand = [jnp.where(tg == g, probs[g * EXP_PER_GROUP + le] * chosen[g * EXP_PER_GROUP + le], 0.0)
                    for g in range(N_EGROUPS)]
            local.append(_sum_rows(cand))
        tot = _sum_rows(local)
        rows = jnp.concatenate([r / tot for r in local] + [jnp.zeros((LANES - EXP_PER_GROUP, tr), F32)], axis=0)
        gcol_ref[...] = rows.T
        acc_ref[...] = jnp.zeros(acc_ref.shape, F32)

    @pl.when(used)
    def _():
        xb = xbf_ref[...]
        gates = gcol_ref[...]
        lane = lax.broadcasted_iota(jnp.int32, gates.shape, 1)
        gcol = jnp.sum(jnp.where(lane == e, gates, 0.0), axis=1, keepdims=True)
        hgate = jnp.dot(xb, wg_ref[0].astype(BF16), preferred_element_type=F32)
        hup = jnp.dot(xb, wu_ref[0].astype(BF16), preferred_element_type=F32)
        acc_ref[...] += _bdot(_silu(hgate) * hup * gcol, wd_ref[0])

    @pl.when(jnp.logical_and(used, e == EXP_PER_GROUP - 1))
    def _():
        o_ref[...] = acc_ref[...]

    @pl.when(jnp.logical_and(jnp.logical_not(used), e == EXP_PER_GROUP - 1))
    def _():
        o_ref[...] = jnp.zeros(o_ref.shape, F32)


def _moe_group(x_sorted, tile_group, n_used, wr_hi, wr_lo, b_col, w_gate, w_up, w_down, *, tr):
    r, d = x_sorted.shape
    f = w_gate.shape[2]

    def w_index(t, e, tg, nu):
        blk = jnp.where(t < nu[0], tg[t] * EXP_PER_GROUP + e,
                        tg[nu[0] - 1] * EXP_PER_GROUP + EXP_PER_GROUP - 1)
        return (blk, 0, 0)

    fixed = lambda *shape: pl.BlockSpec(shape, lambda t, e, tg, nu: (0,) * len(shape))
    grid_spec = pltpu.PrefetchScalarGridSpec(
        num_scalar_prefetch=2,
        grid=(r // tr, EXP_PER_GROUP),
        in_specs=[pl.BlockSpec((tr, d), lambda t, e, tg, nu: (t, 0)),
                  fixed(d, LANES), fixed(d, LANES), fixed(N_EXPERTS, 1),
                  pl.BlockSpec((1, d, f), w_index), pl.BlockSpec((1, d, f), w_index),
                  pl.BlockSpec((1, f, d), w_index)],
        out_specs=pl.BlockSpec((tr, d), lambda t, e, tg, nu: (t, 0)),
        scratch_shapes=[pltpu.VMEM((tr, d), BF16), pltpu.VMEM((tr, LANES), F32), pltpu.VMEM((tr, d), F32)],
    )
    return pl.pallas_call(
        functools.partial(_moe_group_kernel, tr=tr),
        grid_spec=grid_spec,
        out_shape=jax.ShapeDtypeStruct((r, d), F32),
        compiler_params=_params("arbitrary", "arbitrary"),
        name="moe_group_experts",
    )(tile_group, n_used, x_sorted, wr_hi, wr_lo, b_col, w_gate, w_up, w_down)


def _dispatch_plan(grp, tr, n_tiles):
    oh = (grp[:, None] == jnp.arange(N_EGROUPS, dtype=jnp.int32)[None, :]).astype(jnp.int32)
    csum = jnp.cumsum(oh, axis=0)
    padded = ((csum[-1] + tr - 1) // tr) * tr
    gend = jnp.cumsum(padded)
    pos = jnp.sum(oh * ((gend - padded)[None, :] + csum - 1), axis=1)
    tile_row0 = jnp.arange(n_tiles, dtype=jnp.int32) * tr
    tile_group = jnp.minimum(jnp.sum((tile_row0[:, None] >= gend[None, :]).astype(jnp.int32), axis=1),
                             N_EGROUPS - 1)
    return pos.astype(jnp.int32), tile_group.astype(jnp.int32), (gend[-1:] // tr).astype(jnp.int32)


def _router(x, wr_hi, wr_lo, b_col, *, tm):
    t, d = x.shape
    return pl.pallas_call(
        _router_kernel,
        grid=(t // tm,),
        in_specs=[pl.BlockSpec((tm, d), lambda i: (i, 0)),
                  pl.BlockSpec((d, LANES), lambda i: (0, 0)),
                  pl.BlockSpec((d, LANES), lambda i: (0, 0)),
                  pl.BlockSpec((N_EXPERTS, 1), lambda i: (0, 0))],
        out_specs=pl.BlockSpec((N_EXPERTS, tm), lambda i: (0, i)),
        out_shape=jax.ShapeDtypeStruct((N_EXPERTS, t), F32),
        compiler_params=_params("arbitrary"),
        name="router",
    )(x, wr_hi, wr_lo, b_col)


def _moe_kernel(x_ref, gate_ref, wg_ref, wu_ref, wd_ref, o_ref, xbf_ref):
    e = pl.program_id(1)

    @pl.when(e == 0)
    def _():
        xbf_ref[...] = x_ref[...].astype(BF16)
        o_ref[...] = jnp.zeros(o_ref.shape, F32)

    xb = xbf_ref[...]
    gate = gate_ref[...]
    lane = lax.broadcasted_iota(jnp.int32, gate.shape, 1)
    gcol = jnp.sum(jnp.where(lane == e, gate, 0.0), axis=1, keepdims=True)
    hgate = jnp.dot(xb, wg_ref[0].astype(BF16), preferred_element_type=F32)
    hup = jnp.dot(xb, wu_ref[0].astype(BF16), preferred_element_type=F32)
    o_ref[...] += _bdot(_silu(hgate) * hup * gcol, wd_ref[0])


def _moe(x, gate, w_gate, w_up, w_down, *, tm):
    t, d = x.shape
    f = w_gate.shape[2]
    return pl.pallas_call(
        _moe_kernel,
        grid=(t // tm, N_EXPERTS),
        in_specs=[pl.BlockSpec((tm, d), lambda i, e: (i, 0)),
                  pl.BlockSpec((tm, N_EXPERTS), lambda i, e: (i, 0)),
                  pl.BlockSpec((1, d, f), lambda i, e: (e, 0, 0)),
                  pl.BlockSpec((1, d, f), lambda i, e: (e, 0, 0)),
                  pl.BlockSpec((1, f, d), lambda i, e: (e, 0, 0))],
        out_specs=pl.BlockSpec((tm, d), lambda i, e: (i, 0)),
        out_shape=jax.ShapeDtypeStruct((t, d), F32),
        scratch_shapes=[pltpu.VMEM((tm, d), BF16)],
        compiler_params=_params("arbitrary", "arbitrary"),
        name="moe_experts",
    )(x, gate, w_gate, w_up, w_down)


def _ln_ple_kernel(x_ref, f_ref, p_ref, g_ref, b_ref, wg_ref, bg_ref, wp_ref, o_ref):
    x = _ln(DN_ALPHA * x_ref[...] + f_ref[...], g_ref[...], b_ref[...])
    gate = jax.nn.sigmoid(_bdot(x, wg_ref[...]) + bg_ref[...])
    o_ref[...] = x + gate * _bdot(p_ref[...], wp_ref[...])


def _ln_ple(x, ffn, p, g, b, w_gate, b_gate, w_p, *, tm):
    t, d = x.shape
    pd = p.shape[1]
    rows = lambda cols: pl.BlockSpec((tm, cols), lambda i: (i, 0))
    fixed = lambda r, c: pl.BlockSpec((r, c), lambda i: (0, 0))
    return pl.pallas_call(
        _ln_ple_kernel,
        grid=(t // tm,),
        in_specs=[rows(d), rows(d), rows(pd), fixed(1, d), fixed(1, d), fixed(d, d), fixed(1, d), fixed(pd, d)],
        out_specs=rows(d),
        out_shape=jax.ShapeDtypeStruct((t, d), F32),
        compiler_params=_params("arbitrary"),
        name="ln_ple",
    )(x, ffn, p, g, b, w_gate, b_gate, w_p)


def _row_tile(rows, want):
    tm = min(rows, want)
    assert rows % tm == 0
    return tm


POOL_TM = 512
MM_TM = 512
QKV_TM = 512
ATTN_TQ = 512
MOE_TM = 1024
MOE_GROUP_TR = 512
ROUTER_TM = 1024
PLE_TM = 512
DECODE_PAGES_PER_STEP = 8


def kernel(x_prompt, x_sample, p_prompt, p_sample, state_pool, cache_k, cache_v, page_table, state_ssm, state_conv,
           ln_g, ln_b, w_pool_in, w_pool_grp, pool_scale, w_pool_out, w_diff_in, diff_lambda_q, diff_lambda_k,
           diff_norm_g, w_diff_out, w_ssd_in, ssd_conv_w, ssd_conv_b, ssd_dt_bias, ssd_a_log, ssd_d, ssd_norm_g,
           w_ssd_out, w_router, b_router, w_exp_gate, w_exp_up, w_exp_down, w_ple, w_ple_gate, b_ple_gate):
    bp, seq, d = x_prompt.shape
    bs, tdec, _ = x_sample.shape
    n_pages = page_table.shape[1]
    past = n_pages * cache_k.shape[2]
    tp = bp * seq
    ts = bs * tdec

    def to_tm(a):
        return jnp.swapaxes(a, 0, 1).reshape((ts,) + a.shape[2:])

    def from_tm(a):
        return jnp.swapaxes(a.reshape((tdec, bs) + a.shape[1:]), 0, 1)

    row = lambda v: v.reshape(1, -1)

    wr = jnp.pad(w_router, ((0, 0), (0, LANES - N_EXPERTS)))
    wr_hi = wr.astype(BF16)
    wr_lo = (wr - wr_hi.astype(F32)).astype(BF16)
    br_col = b_router.reshape(N_EXPERTS, 1)

    def finish(x1, i, p, sparse):
        t = x1.shape[0]
        if sparse:
            tr = MOE_GROUP_TR
            n_tiles = t // tr + N_EGROUPS
            grp = _route_group(x1, wr_hi, wr_lo, br_col, tm=_row_tile(t, ROUTER_TM))[0]
            pos, tile_group, n_used = _dispatch_plan(grp, tr, n_tiles)
            x_sorted = _permute_rows(x1, pos, out_rows=n_tiles * tr, scatter=True)
            y_sorted = _moe_group(x_sorted, tile_group, n_used, wr_hi, wr_lo, br_col,
                                  w_exp_gate[i], w_exp_up[i], w_exp_down[i], tr=tr)
            ffn = _permute_rows(y_sorted, pos, out_rows=t, scatter=False)
        else:
            gate = _router(x1, wr_hi, wr_lo, br_col, tm=_row_tile(t, ROUTER_TM)).T
            ffn = _moe(x1, gate, w_exp_gate[i], w_exp_up[i], w_exp_down[i], tm=_row_tile(t, MOE_TM))
        return _ln_ple(x1, ffn, p, row(ln_g[i, 1]), row(ln_b[i, 1]), w_ple_gate[i].astype(BF16),
                       row(b_ple_gate[i]), w_ple[i].astype(BF16), tm=_row_tile(t, PLE_TM))

    xp = x_prompt.reshape(tp, d)
    xs = to_tm(x_sample)
    pos_p = jnp.arange(seq, dtype=jnp.int32)
    pos_s = jnp.repeat(past + jnp.arange(tdec, dtype=jnp.int32), bs)
    outs = {k: [] for k in ("pool_p", "pool_s", "k_p", "v_p", "k_s", "v_s", "ssm_p", "conv_p", "ssm_s", "conv_s")}

    for i in range(DEPTH):
        kind, j = i % N_MIXERS, i // N_MIXERS
        g1, b1 = row(ln_g[i, 0]), row(ln_b[i, 0])
        if kind == 0:
            wts = (w_pool_in[j].astype(BF16), w_pool_grp[j].astype(BF16), row(pool_scale[j]),
                   w_pool_out[j].astype(BF16), g1, b1)
            x1p, newp = _pool_layer(xp.reshape(bp, seq, d), jnp.zeros((bp, POOL_BUF + 1, d), F32), *wts,
                                    tm=_row_tile(seq, POOL_TM), stride=1, pos0=0)
            pre_s = jnp.swapaxes(state_pool[j], 0, 1).reshape(1, POOL_BUF * bs, d)
            pre_s = jnp.pad(pre_s, ((0, 0), (bs, 0), (0, 0)))
            x1s, news = _pool_layer(xs.reshape(1, ts, d), pre_s, *wts, tm=ts, stride=bs, pos0=past)
            outs["pool_p"].append(newp)
            outs["pool_s"].append(jnp.swapaxes(news.reshape(POOL_BUF, bs, d), 0, 1))
            x1p = x1p.reshape(tp, d)
            x1s = x1s.reshape(ts, d)
        elif kind == 1:
            lam_init = 0.8 - 0.6 * math.exp(-0.3 * i)
            w_in = w_diff_in[j].astype(BF16)
            w_out = w_diff_out[j].astype(BF16)
            lq, lk, ng = diff_lambda_q[j], diff_lambda_k[j], row(diff_norm_g[j])
            cos_p, sin_p = _rope_tables(pos_p)
            q, k, v = _qkv_rope(xp.reshape(bp, seq, d), w_in, cos_p, sin_p, tm=_row_tile(seq, QKV_TM))
            o = _diff_attn_prompt(q, k, v, lq, lk, ng, lam_init, tq=_row_tile(seq, ATTN_TQ))
            x1p = _mm_res_ln(o.reshape(tp, d), w_out, xp, g1, b1, tm=_row_tile(tp, MM_TM))
            outs["k_p"].append(k.reshape(bp, seq, DIFF_HEADS, 2, DIFF_DH))
            outs["v_p"].append(v.reshape(bp, seq, DIFF_HEADS, DIFF_VD))

            cos_s, sin_s = _rope_tables(pos_s)
            q, k, v = _qkv_rope(xs.reshape(1, ts, d), w_in, cos_s, sin_s, tm=ts)
            q, k, v = from_tm(q[0]), from_tm(k[0]), from_tm(v[0])
            q5 = q.reshape(bs, tdec, DIFF_HEADS * 2, DIFF_DH)
            eye = jnp.eye(DIFF_HEADS * 2, dtype=F32)
            qbd = jnp.einsum('bthd,hg->bhtgd', q5, eye)
            qbd = jnp.pad(qbd, ((0, 0), (0, 0), (0, DEC_TPAD - tdec), (0, 0), (0, 0)))
            qbd = qbd.reshape(bs, DEC_ROWS, d)
            kt_pages = jnp.transpose(cache_k, (0, 1, 3, 4, 5, 2)).reshape(-1, d, PAGE_SIZE)
            v_pages = cache_v.reshape(-1, PAGE_SIZE * DIFF_HEADS, DIFF_VD)
            knew = jnp.pad(jnp.swapaxes(k, 1, 2), ((0, 0), (0, 0), (0, PAGE_SIZE - tdec)))
            vnew = jnp.pad(v, ((0, 0), (0, PAGE_SIZE - tdec), (0, 0))).reshape(bs, PAGE_SIZE * DIFF_HEADS, DIFF_VD)
            o = _diff_attn_sample(qbd, kt_pages, v_pages, page_table, knew, vnew, lq, lk, ng, lam_init,
                                  pages_per_step=DECODE_PAGES_PER_STEP, page0=j * cache_k.shape[1])
            o = to_tm(o[:, :tdec])
            x1s = _mm_res_ln(o, w_out, xs, g1, b1, tm=ts)
            outs["k_s"].append(k.reshape(bs, tdec, DIFF_HEADS, 2, DIFF_DH))
            outs["v_s"].append(v.reshape(bs, tdec, DIFF_HEADS, DIFF_VD))
        else:
            w_in = w_ssd_in[j]
            w_z = w_in[:, :SSD_DI].astype(BF16)
            w_xbc = w_in[:, SSD_DI:SSD_DI + SSD_CONV_DIM].astype(BF16)
            w_dt = jnp.pad(w_in[:, SSD_DI + SSD_CONV_DIM:], ((0, 0), (0, LANES - SSD_H))).astype(BF16)
            w_out = w_ssd_out[j].astype(BF16)
            dtb_row = jnp.pad(ssd_dt_bias[j], (0, LANES - SSD_H)).reshape(1, LANES)
            alog_row = jnp.pad(ssd_a_log[j], (0, LANES - SSD_H)).reshape(1, LANES)
            dtb_col = ssd_dt_bias[j].reshape(SSD_H, 1)
            alog_col = ssd_a_log[j].reshape(SSD_H, 1)
            d_exp = jnp.repeat(ssd_d[j], SSD_P).reshape(1, SSD_DI)
            ng = row(ssd_norm_g[j])
            expand = (jnp.arange(SSD_DI)[None, :] // SSD_P == jnp.arange(LANES)[:, None]).astype(BF16)
            shared = (ssd_conv_w[j], row(ssd_conv_b[j]), dtb_row, dtb_col, alog_row, alog_col, d_exp, ng, expand)

            def in_proj(x, tm):
                z = _matmul(x, w_z, tm=tm, tn=1024)
                xbc = _matmul(x, w_xbc, tm=tm, tn=1024)
                dt = _matmul(x, w_dt, tm=tm, tn=LANES)
                return z, xbc, dt

            def hlast_to_state(hl, b):
                return jnp.transpose(hl.reshape(b, SSD_N, SSD_H, SSD_P), (0, 2, 3, 1))

            z, xbc, dt = in_proj(xp, _row_tile(tp, MM_TM))
            dtt = jnp.swapaxes(dt.reshape(bp, seq, LANES)[:, :, :SSD_H], 1, 2)
            y, hl = _ssd_scan(z.reshape(bp, seq, SSD_DI), xbc.reshape(bp, seq, SSD_CONV_DIM),
                              dt.reshape(bp, seq, LANES), dtt, jnp.zeros((bp, CONV_PAD, SSD_CONV_DIM), F32),
                              *shared, None, valid_len=None)
            x1p = _mm_res_ln(y.reshape(tp, SSD_DI), w_out, xp, g1, b1, tm=_row_tile(tp, MM_TM))
            outs["ssm_p"].append(hlast_to_state(hl, bp))
            outs["conv_p"].append(xbc.reshape(bp, seq, SSD_CONV_DIM)[:, seq - (SSD_CONV - 1):])

            z, xbc, dt = in_proj(xs, ts)
            z, xbc, dt = from_tm(z), from_tm(xbc), from_tm(dt)
            padt = ((0, 0), (0, SSD_CHUNK - tdec), (0, 0))
            dtp = jnp.pad(dt, padt)
            cpre = jnp.pad(state_conv[j], ((0, 0), (CONV_PAD - (SSD_CONV - 1), 0), (0, 0)))
            h0t = jnp.transpose(state_ssm[j], (0, 3, 1, 2)).reshape(bs, SSD_N, SSD_DI)
            y, hl = _ssd_scan(jnp.pad(z, padt), jnp.pad(xbc, padt), dtp, jnp.swapaxes(dtp[:, :, :SSD_H], 1, 2),
                              cpre, *shared, h0t, valid_len=tdec)
            x1s = _mm_res_ln(to_tm(y[:, :tdec]), w_out, xs, g1, b1, tm=ts)
            outs["ssm_s"].append(hlast_to_state(hl, bs))
            ext = jnp.concatenate([state_conv[j], xbc], axis=1)
            outs["conv_s"].append(ext[:, -(SSD_CONV - 1):])

        xp = finish(x1p, i, p_prompt[i].reshape(tp, PLE_DIM), sparse=True)
        xs = finish(x1s, i, to_tm(p_sample[i]), sparse=False)

    st = lambda name: jnp.stack(outs[name])
    return (xp.reshape(bp, seq, d), from_tm(xs), st("pool_p"), st("pool_s"), st("k_p"), st("v_p"), st("k_s"),
            st("v_s"), st("ssm_p"), st("conv_p"), st("ssm_s"), st("conv_s"))
```

```python
import functools
import math

import jax
import jax.numpy as jnp
from jax import lax
from jax.experimental import pallas as pl
from jax.experimental.pallas import tpu as pltpu

F32 = jnp.float32
BF16 = jnp.bfloat16

D_MODEL = 1024
DEPTH = 4
N_MIXERS = 3
PAGE_SIZE = 128

POOL_WINDOWS = (2, 4, 8, 16)
POOL_GD = D_MODEL // len(POOL_WINDOWS)
POOL_BUF = max(POOL_WINDOWS) - 1

DIFF_DH = 64
DIFF_VD = 2 * DIFF_DH
DIFF_HEADS = D_MODEL // DIFF_VD
ROPE_THETA = 10000.0

SSD_DI = 2 * D_MODEL
SSD_P = 64
SSD_H = SSD_DI // SSD_P
SSD_G = 4
SSD_N = 128
SSD_CONV = 4
SSD_CONV_DIM = SSD_DI + 2 * SSD_G * SSD_N
SSD_CHUNK = 128

N_EXPERTS = 16
N_EGROUPS = 4
EXP_PER_GROUP = N_EXPERTS // N_EGROUPS
D_FF_EXPERT = D_MODEL // 2

PLE_DIM = 256
DN_ALPHA = (2 * DEPTH) ** 0.25
NORM_EPS = 1e-5

LANES = 128
SUBLANES = 8
VMEM_LIMIT = 56 * 1024 * 1024


def _params(*sem):
    return pltpu.CompilerParams(dimension_semantics=sem, vmem_limit_bytes=VMEM_LIMIT)


def _bdot(a, b):
    return jnp.dot(a.astype(BF16), b.astype(BF16), preferred_element_type=F32)


def _bdot_nt(a, b):
    return lax.dot_general(a.astype(BF16), b.astype(BF16), (((1,), (1,)), ((), ())),
                           preferred_element_type=F32)


def _split3(x):
    hi = x.astype(BF16)
    r = x - hi.astype(F32)
    mid = r.astype(BF16)
    lo = (r - mid.astype(F32)).astype(BF16)
    return hi, mid, lo


def _dot_exact_rhs(x, m):
    hi, mid, lo = _split3(x)
    return (jnp.dot(hi, m, preferred_element_type=F32) + jnp.dot(mid, m, preferred_element_type=F32)
            + jnp.dot(lo, m, preferred_element_type=F32))


def _dot_exact_lhs(m, x):
    hi, mid, lo = _split3(x)
    return (jnp.dot(m, hi, preferred_element_type=F32) + jnp.dot(m, mid, preferred_element_type=F32)
            + jnp.dot(m, lo, preferred_element_type=F32))


def _ln(v, g, b):
    mu = jnp.mean(v, axis=-1, keepdims=True)
    c = v - mu
    var = jnp.mean(c * c, axis=-1, keepdims=True)
    return c * lax.rsqrt(var + NORM_EPS) * g + b


def _silu(v):
    return v * jax.nn.sigmoid(v)


def _softplus(v):
    return jnp.maximum(v, 0.0) + jnp.log(1.0 + jnp.exp(-jnp.abs(v)))


def _pool_kernel(x_ref, pre_ref, win_ref, wgrp_ref, scale_ref, wout_ref, g_ref, b_ref,
                 y_ref, newp_ref, ext_ref, *, tm, stride, pos0, n_tiles):
    i = pl.program_id(1)
    pre_rows = (POOL_BUF + 1) * stride

    @pl.when(i == 0)
    def _():
        ext_ref[0:pre_rows, :] = pre_ref[0]

    @pl.when(i > 0)
    def _():
        ext_ref[0:pre_rows, :] = ext_ref[tm:tm + pre_rows, :]

    x = x_ref[0]
    u = _bdot(x, win_ref[...])
    ext_ref[pre_rows:pre_rows + tm, :] = u
    row = lax.broadcasted_iota(jnp.int32, (tm, 1), 0)
    pos = pos0 + (i * tm + row) // stride
    mixed = []
    for gi, w in enumerate(POOL_WINDOWS):
        c0 = gi * POOL_GD
        ug = u[:, c0:c0 + POOL_GD]
        s = ug
        for k in range(1, w):
            r0 = pre_rows - k * stride
            s = s + ext_ref[r0:r0 + tm, c0:c0 + POOL_GD]
        cnt = jnp.minimum(w, pos + 1).astype(F32)
        pooled = s / cnt - ug
        mixed.append(_bdot(pooled, wgrp_ref[gi]))
    mixed = jnp.concatenate(mixed, axis=1) * scale_ref[...]
    mix = _bdot(mixed, wout_ref[...])
    y_ref[0] = _ln(DN_ALPHA * x + mix, g_ref[...], b_ref[...])

    @pl.when(i == n_tiles - 1)
    def _():
        newp_ref[0] = ext_ref[tm + stride:tm + pre_rows, :]


def _pool_layer(x, prefix, w_in, w_grp, scale, w_out, g, b, *, tm, stride, pos0):
    bsz, rows, d = x.shape
    n_tiles = rows // tm
    pre_rows = (POOL_BUF + 1) * stride
    kern = functools.partial(_pool_kernel, tm=tm, stride=stride, pos0=pos0, n_tiles=n_tiles)
    full = lambda *shape: pl.BlockSpec(shape, lambda bi, i: (0,) * len(shape))
    return pl.pallas_call(
        kern,
        grid=(bsz, n_tiles),
        in_specs=[
            pl.BlockSpec((1, tm, d), lambda bi, i: (bi, i, 0)),
            pl.BlockSpec((1, pre_rows, d), lambda bi, i: (bi, 0, 0)),
            full(d, d), full(len(POOL_WINDOWS), POOL_GD, POOL_GD), full(1, d), full(d, d),
            full(1, d), full(1, d),
        ],
        out_specs=[
            pl.BlockSpec((1, tm, d), lambda bi, i: (bi, i, 0)),
            pl.BlockSpec((1, POOL_BUF * stride, d), lambda bi, i: (bi, 0, 0)),
        ],
        out_shape=[jax.ShapeDtypeStruct((bsz, rows, d), F32),
                   jax.ShapeDtypeStruct((bsz, POOL_BUF * stride, d), F32)],
        scratch_shapes=[pltpu.VMEM((pre_rows + tm, d), F32)],
        compiler_params=_params("arbitrary", "arbitrary"),
        name="pool_layer",
    )(x, prefix, w_in, w_grp, scale, w_out, g, b)


def _mm_kernel(x_ref, w_ref, o_ref):
    o_ref[...] = _bdot(x_ref[...], w_ref[...]).astype(o_ref.dtype)


def _matmul(x, w, *, tm, tn, out_dtype=F32):
    t, k = x.shape
    n = w.shape[1]
    return pl.pallas_call(
        _mm_kernel,
        grid=(n // tn, t // tm),
        in_specs=[pl.BlockSpec((tm, k), lambda j, i: (i, 0)),
                  pl.BlockSpec((k, tn), lambda j, i: (0, j))],
        out_specs=pl.BlockSpec((tm, tn), lambda j, i: (i, j)),
        out_shape=jax.ShapeDtypeStruct((t, n), out_dtype),
        compiler_params=_params("arbitrary", "arbitrary"),
        name="matmul",
    )(x, w)


def _mm_res_ln_kernel(a_ref, w_ref, x_ref, g_ref, b_ref, o_ref):
    mix = _bdot(a_ref[...], w_ref[...])
    o_ref[...] = _ln(DN_ALPHA * x_ref[...] + mix, g_ref[...], b_ref[...])


def _mm_res_ln(a, w, x, g, b, *, tm):
    t, k = a.shape
    d = w.shape[1]
    return pl.pallas_call(
        _mm_res_ln_kernel,
        grid=(t // tm,),
        in_specs=[pl.BlockSpec((tm, k), lambda i: (i, 0)),
                  pl.BlockSpec((k, d), lambda i: (0, 0)),
                  pl.BlockSpec((tm, d), lambda i: (i, 0)),
                  pl.BlockSpec((1, d), lambda i: (0, 0)),
                  pl.BlockSpec((1, d), lambda i: (0, 0))],
        out_specs=pl.BlockSpec((tm, d), lambda i: (i, 0)),
        out_shape=jax.ShapeDtypeStruct((t, d), F32),
        compiler_params=_params("arbitrary"),
        name="mm_res_ln",
    )(a, w, x, g, b)


def _qkv_kernel(x_ref, w_ref, cos_ref, sin_ref, q_ref, k_ref, v_ref, *, tm):
    x = x_ref[0].astype(BF16)
    cos = cos_ref[...]
    sin = sin_ref[...]
    lane = lax.broadcasted_iota(jnp.int32, (tm, LANES), 1)
    first_half = (lane % DIFF_DH) < (DIFF_DH // 2)

    def rope(y):
        outs = []
        for c in range(D_MODEL // LANES):
            yc = y[:, c * LANES:(c + 1) * LANES]
            sw = jnp.where(first_half, pltpu.roll(yc, LANES - DIFF_DH // 2, 1),
                           pltpu.roll(yc, DIFF_DH // 2, 1))
            outs.append(yc * cos + sw * sin)
        return jnp.concatenate(outs, axis=1)

    q_ref[0] = rope(jnp.dot(x, w_ref[:, 0:D_MODEL], preferred_element_type=F32))
    k_ref[0] = rope(jnp.dot(x, w_ref[:, D_MODEL:2 * D_MODEL], preferred_element_type=F32))
    v_ref[0] = jnp.dot(x, w_ref[:, 2 * D_MODEL:3 * D_MODEL], preferred_element_type=F32)


def _rope_tables(pos):
    half = DIFF_DH // 2
    inv = ROPE_THETA ** (-jnp.arange(half, dtype=F32) * 2.0 / DIFF_DH)
    ang = pos.astype(F32)[:, None] * inv[None, :]
    cos, sin = jnp.cos(ang), jnp.sin(ang)
    reps = LANES // DIFF_DH
    cos_t = jnp.tile(jnp.concatenate([cos, cos], axis=1), (1, reps))
    sin_t = jnp.tile(jnp.concatenate([-sin, sin], axis=1), (1, reps))
    return cos_t, sin_t


def _qkv_rope(x, w, cos_t, sin_t, *, tm):
    bsz, rows, d = x.shape
    kern = functools.partial(_qkv_kernel, tm=tm)
    spec = pl.BlockSpec((1, tm, d), lambda bi, i: (bi, i, 0))
    tab = pl.BlockSpec((tm, LANES), lambda bi, i: (i, 0))
    shp = jax.ShapeDtypeStruct((bsz, rows, d), F32)
    return pl.pallas_call(
        kern,
        grid=(bsz, rows // tm),
        in_specs=[spec, pl.BlockSpec((d, 3 * d), lambda bi, i: (0, 0)), tab, tab],
        out_specs=[spec, spec, spec],
        out_shape=[shp, shp, shp],
        compiler_params=_params("arbitrary", "arbitrary"),
        name="qkv_rope",
    )(x, w, cos_t, sin_t)


def _diff_lambda(lq_ref, lk_ref, lam_init):
    dots = jnp.sum(lq_ref[...] * lk_ref[...], axis=-1, keepdims=True)
    e = jnp.exp(dots)
    return e[0:1, :] - e[1:2, :] + lam_init


def _head_norm(o, ng, lam_init):
    ms = jnp.mean(o * o, axis=-1, keepdims=True)
    return o * lax.rsqrt(ms + NORM_EPS) * ng * (1.0 - lam_init)


def _attn_kernel(q_ref, k_ref, v_ref, lq_ref, lk_ref, ng_ref, o_ref, *, tq, lam_init):
    qi = pl.program_id(2)
    q = q_ref[0] * (DIFF_DH ** -0.5)
    q1 = q[:, :DIFF_DH].astype(BF16)
    q2 = q[:, DIFF_DH:].astype(BF16)
    ones = jnp.ones((tq, DIFF_VD), BF16)

    def scores(kb):
        start = pl.multiple_of(kb * tq, tq)
        k = k_ref[0, pl.ds(start, tq), :]
        v1 = jnp.concatenate([v_ref[0, pl.ds(start, tq), :].astype(BF16), ones], axis=1)
        s = jnp.concatenate([_bdot_nt(q1, k[:, :DIFF_DH]), _bdot_nt(q2, k[:, DIFF_DH:])], axis=0)
        return s, v1

    def update(s, v1, m, a):
        mn = jnp.maximum(m, jnp.max(s, axis=1, keepdims=True))
        p = jnp.exp(s - mn).astype(BF16)
        a = jnp.exp(m - mn) * a + jnp.dot(p, v1, preferred_element_type=F32)
        return mn, a

    def body(kb, carry):
        s, v1 = scores(kb)
        return update(s, v1, *carry)

    init = (jnp.full((2 * tq, 1), -jnp.inf, F32), jnp.zeros((2 * tq, 2 * DIFF_VD), F32))
    m, a = lax.fori_loop(0, qi, body, init)
    s, v1 = scores(qi)
    row = lax.broadcasted_iota(jnp.int32, (2 * tq, tq), 0) % tq
    col = lax.broadcasted_iota(jnp.int32, (2 * tq, tq), 1)
    m, a = update(jnp.where(col <= row, s, -jnp.inf), v1, m, a)
    lam = _diff_lambda(lq_ref, lk_ref, lam_init)
    o = (a[:tq, :DIFF_VD] / a[:tq, DIFF_VD:DIFF_VD + 1]
         - lam * (a[tq:, :DIFF_VD] / a[tq:, DIFF_VD:DIFF_VD + 1]))
    o_ref[0] = _head_norm(o, ng_ref[...], lam_init).astype(o_ref.dtype)


def _diff_attn_prompt(q, k, v, lq, lk, ng, lam_init, *, tq):
    bsz, seq, d = q.shape
    kern = functools.partial(_attn_kernel, tq=tq, lam_init=lam_init)
    small = lambda *shape: pl.BlockSpec(shape, lambda bi, h, i: (0,) * len(shape))
    return pl.pallas_call(
        kern,
        grid=(bsz, DIFF_HEADS, seq // tq),
        in_specs=[pl.BlockSpec((1, tq, DIFF_VD), lambda bi, h, i: (bi, i, h)),
                  pl.BlockSpec((1, seq, DIFF_VD), lambda bi, h, i: (bi, 0, h)),
                  pl.BlockSpec((1, seq, DIFF_VD), lambda bi, h, i: (bi, 0, h)),
                  small(2, DIFF_DH), small(2, DIFF_DH), small(1, DIFF_VD)],
        out_specs=pl.BlockSpec((1, tq, DIFF_VD), lambda bi, h, i: (bi, i, h)),
        out_shape=jax.ShapeDtypeStruct((bsz, seq, d), BF16),
        compiler_params=_params("arbitrary", "arbitrary", "arbitrary"),
        name="diff_attn_prompt",
    )(q, k, v, lq, lk, ng)


DEC_TPAD = SUBLANES
DEC_ROWS = DIFF_HEADS * 2 * DEC_TPAD


def _decode_kernel(pt_ref, q_ref, *refs, pages_per_step, n_steps, lam_init):
    g = pages_per_step
    k_refs = refs[:g]
    v_refs = refs[g:2 * g]
    knew_ref, vnew_ref, rep_ref, lq_ref, lk_ref, ng_ref, o_ref, m_ref, l_ref, acc_ref = refs[2 * g:]
    s = pl.program_id(1)
    scale = DIFF_DH ** -0.5
    rep = rep_ref[...]
    row_head = lax.broadcasted_iota(jnp.int32, (DEC_ROWS, PAGE_SIZE * DIFF_HEADS), 0) // (2 * DEC_TPAD)
    col_head = lax.broadcasted_iota(jnp.int32, (DEC_ROWS, PAGE_SIZE * DIFF_HEADS), 1) % DIFF_HEADS
    own_head = row_head == col_head

    @pl.when(s == 0)
    def _():
        m_ref[...] = jnp.full(m_ref.shape, -jnp.inf, F32)
        l_ref[...] = jnp.zeros(l_ref.shape, F32)
        acc_ref[...] = jnp.zeros(acc_ref.shape, F32)

    q = (q_ref[0] * scale).astype(BF16)

    def block(ks, vs, mask):
        sc = jnp.concatenate([_bdot(q, k[0]) for k in ks], axis=1)
        if mask is not None:
            sc = jnp.where(mask, sc, -jnp.inf)
        m_old = m_ref[...]
        mn = jnp.maximum(m_old, jnp.max(sc, axis=1, keepdims=True))
        alpha = jnp.exp(m_old - mn)
        p = jnp.exp(sc - mn)
        pv = None
        for gi in range(len(vs)):
            pe = jnp.dot(p[:, gi * PAGE_SIZE:(gi + 1) * PAGE_SIZE].astype(BF16), rep, preferred_element_type=F32)
            part = _bdot(jnp.where(own_head, pe, 0.0), vs[gi][0])
            pv = part if pv is None else pv + part
        l_ref[...] = alpha * l_ref[...] + jnp.sum(p, axis=1, keepdims=True)
        acc_ref[...] = alpha * acc_ref[...] + pv
        m_ref[...] = mn

    block(k_refs, v_refs, None)

    @pl.when(s == n_steps - 1)
    def _():
        t_row = lax.broadcasted_iota(jnp.int32, (DEC_ROWS, PAGE_SIZE), 0) % DEC_TPAD
        key = lax.broadcasted_iota(jnp.int32, (DEC_ROWS, PAGE_SIZE), 1)
        block([knew_ref], [vnew_ref], key <= t_row)
        lam = _diff_lambda(lq_ref, lk_ref, lam_init)
        ng = ng_ref[...]
        for h in range(DIFF_HEADS):
            r1 = h * 2 * DEC_TPAD
            r2 = r1 + DEC_TPAD
            cols = slice(h * DIFF_VD, (h + 1) * DIFF_VD)
            a1 = acc_ref[r1:r1 + DEC_TPAD, :] / l_ref[r1:r1 + DEC_TPAD, :]
            a2 = acc_ref[r2:r2 + DEC_TPAD, :] / l_ref[r2:r2 + DEC_TPAD, :]
            o_ref[0, :, cols] = _head_norm(a1 - lam * a2, ng, lam_init).astype(o_ref.dtype)


def _diff_attn_sample(qbd, cache_k, cache_v, page_table, knew, vnew, lq, lk, ng, lam_init,
                      *, pages_per_step, page0):
    bsz = qbd.shape[0]
    n_pages = page_table.shape[1]
    g = pages_per_step
    n_steps = n_pages // g
    kern = functools.partial(_decode_kernel, pages_per_step=g, n_steps=n_steps, lam_init=lam_init)
    rep = (jnp.arange(PAGE_SIZE * DIFF_HEADS)[None, :] // DIFF_HEADS == jnp.arange(PAGE_SIZE)[:, None]).astype(BF16)

    def page_spec(gi):
        return pl.BlockSpec((1, D_MODEL, PAGE_SIZE), lambda b, s, pt: (page0 + pt[b, s * g + gi], 0, 0))

    per_b = lambda rows, cols: pl.BlockSpec((1, rows, cols), lambda b, s, pt: (b, 0, 0))
    small = lambda *shape: pl.BlockSpec(shape, lambda b, s, pt: (0,) * len(shape))
    grid_spec = pltpu.PrefetchScalarGridSpec(
        num_scalar_prefetch=1,
        grid=(bsz, n_steps),
        in_specs=([per_b(DEC_ROWS, D_MODEL)] + [page_spec(gi) for gi in range(g)] + [page_spec(gi) for gi in range(g)]
                  + [per_b(D_MODEL, PAGE_SIZE), per_b(D_MODEL, PAGE_SIZE), small(PAGE_SIZE, PAGE_SIZE * DIFF_HEADS),
                     small(2, DIFF_DH), small(2, DIFF_DH), small(1, DIFF_VD)]),
        out_specs=per_b(DEC_TPAD, D_MODEL),
        scratch_shapes=[pltpu.VMEM((DEC_ROWS, 1), F32), pltpu.VMEM((DEC_ROWS, 1), F32),
                        pltpu.VMEM((DEC_ROWS, DIFF_VD), F32)],
    )
    return pl.pallas_call(
        kern,
        grid_spec=grid_spec,
        out_shape=jax.ShapeDtypeStruct((bsz, DEC_TPAD, D_MODEL), BF16),
        compiler_params=_params("arbitrary", "arbitrary"),
        name="diff_attn_sample",
    )(page_table, qbd, *([cache_k] * g), *([cache_v] * g), knew, vnew, rep, lq, lk, ng)


CONV_PAD = SUBLANES


def _ssd_kernel(*refs, q, n_chunks, valid_len, has_h0):
    if has_h0:
        (z_ref, xbc_ref, dt_ref, dtt_ref, cpre_ref, cw_ref, cb_ref, dtb_row_ref, dtb_col_ref,
         alog_row_ref, alog_col_ref, dexp_ref, ng_ref, expand_ref, h0_ref,
         y_ref, hl_ref, s_ref, cext_ref) = refs
    else:
        (z_ref, xbc_ref, dt_ref, dtt_ref, cpre_ref, cw_ref, cb_ref, dtb_row_ref, dtb_col_ref,
         alog_row_ref, alog_col_ref, dexp_ref, ng_ref, expand_ref,
         y_ref, hl_ref, s_ref, cext_ref) = refs
        h0_ref = None
    c = pl.program_id(1)

    @pl.when(c == 0)
    def _():
        if has_h0:
            s_ref[...] = h0_ref[0]
        else:
            s_ref[...] = jnp.zeros(s_ref.shape, F32)
        cext_ref[0:CONV_PAD, :] = cpre_ref[0]

    @pl.when(c > 0)
    def _():
        cext_ref[0:CONV_PAD, :] = cext_ref[q:q + CONV_PAD, :]

    cext_ref[CONV_PAD:CONV_PAD + q, :] = xbc_ref[0]
    xbc = cb_ref[...]
    for kk in range(SSD_CONV):
        r0 = CONV_PAD - (SSD_CONV - 1) + kk
        xbc = xbc + cext_ref[r0:r0 + q, :] * cw_ref[kk:kk + 1, :]
    xbc = _silu(xbc)
    xs = xbc[:, :SSD_DI]
    bm = xbc[:, SSD_DI:SSD_DI + SSD_G * SSD_N]
    cm = xbc[:, SSD_DI + SSD_G * SSD_N:]

    dtc = _softplus(dt_ref[0] + dtb_row_ref[...])
    dtr = _softplus(dtt_ref[0] + dtb_col_ref[...])
    if valid_len is not None:
        t_col = c * q + lax.broadcasted_iota(jnp.int32, (q, LANES), 0)
        t_row = c * q + lax.broadcasted_iota(jnp.int32, (SSD_H, q), 1)
        dtc = jnp.where(t_col < valid_len, dtc, 0.0)
        dtr = jnp.where(t_row < valid_len, dtr, 0.0)
    ii = lax.broadcasted_iota(jnp.int32, (q, q), 0)
    jj = lax.broadcasted_iota(jnp.int32, (q, q), 1)
    causal = jj <= ii
    tri = jnp.where(causal, 1.0, 0.0).astype(BF16)
    tri_t = jnp.where(ii <= jj, 1.0, 0.0).astype(BF16)
    acum_c = _dot_exact_lhs(tri, dtc * (-jnp.exp(alog_row_ref[...])))
    acum_r = _dot_exact_rhs(dtr * (-jnp.exp(alog_col_ref[...])), tri_t)
    total = acum_c[q - 1:q, :]
    expand = expand_ref[...]
    dte_x = _dot_exact_rhs(jnp.exp(total - acum_c) * dtc, expand)
    eac_x = _dot_exact_rhs(jnp.exp(acum_c), expand)
    cdec_x = _dot_exact_rhs(jnp.broadcast_to(jnp.exp(total), (SUBLANES, LANES)), expand)[0:1, :]

    gw = SSD_DI // SSD_G
    hg = SSD_H // SSD_G
    lane = lax.broadcasted_iota(jnp.int32, (q, 2 * SSD_P), 1)
    ys = []
    for g in range(SSD_G):
        gc = slice(g * gw, (g + 1) * gw)
        bg = bm[:, g * SSD_N:(g + 1) * SSD_N]
        cg = cm[:, g * SSD_N:(g + 1) * SSD_N]
        cbm = _bdot_nt(cg, bg)
        xg = xs[:, gc]
        s_old = s_ref[:, gc]
        y_g = _bdot(cg, s_old) * eac_x[:, gc]
        s_ref[:, gc] = s_old * cdec_x[:, gc] + _bdot(bg.T, xg * dte_x[:, gc])
        pairs = []
        for hp in range(hg // 2):
            wts = []
            for hh in range(2):
                h = g * hg + 2 * hp + hh
                seg = acum_c[:, h:h + 1] - acum_r[h:h + 1, :]
                dec = jnp.exp(jnp.where(causal, seg, -jnp.inf))
                wts.append((cbm * dec * dtr[h:h + 1, :]).astype(BF16))
            xp = xg[:, hp * 2 * SSD_P:(hp + 1) * 2 * SSD_P]
            xbd = jnp.concatenate([jnp.where(lane < SSD_P, xp, 0.0), jnp.where(lane >= SSD_P, xp, 0.0)],
                                  axis=0).astype(BF16)
            pairs.append(jnp.dot(jnp.concatenate(wts, axis=1), xbd, preferred_element_type=F32))
        y_g = y_g + jnp.concatenate(pairs, axis=1) + xg * dexp_ref[:, gc]
        zg = z_ref[0, :, gc]
        y_g = y_g * _silu(zg)
        y_g = y_g * lax.rsqrt(jnp.mean(y_g * y_g, axis=-1, keepdims=True) + NORM_EPS)
        ys.append(y_g)
    y_ref[0] = (jnp.concatenate(ys, axis=1) * ng_ref[...]).astype(y_ref.dtype)

    @pl.when(c == n_chunks - 1)
    def _():
        hl_ref[0] = s_ref[...]


def _ssd_scan(z, xbc, dt, dtt, cpre, conv_w, conv_b, dtb_row, dtb_col, alog_row, alog_col, d_exp, ng,
              expand, h0t, *, valid_len):
    bsz, seq, _ = z.shape
    q = SSD_CHUNK
    n_chunks = seq // q
    has_h0 = h0t is not None
    kern = functools.partial(_ssd_kernel, q=q, n_chunks=n_chunks, valid_len=valid_len, has_h0=has_h0)
    full = lambda *shape: pl.BlockSpec(shape, lambda b, c: (0,) * len(shape))
    in_specs = [
        pl.BlockSpec((1, q, SSD_DI), lambda b, c: (b, c, 0)),
        pl.BlockSpec((1, q, SSD_CONV_DIM), lambda b, c: (b, c, 0)),
        pl.BlockSpec((1, q, LANES), lambda b, c: (b, c, 0)),
        pl.BlockSpec((1, SSD_H, q), lambda b, c: (b, 0, c)),
        pl.BlockSpec((1, CONV_PAD, SSD_CONV_DIM), lambda b, c: (b, 0, 0)),
        full(SSD_CONV, SSD_CONV_DIM), full(1, SSD_CONV_DIM),
        full(1, LANES), full(SSD_H, 1), full(1, LANES), full(SSD_H, 1),
        full(1, SSD_DI), full(1, SSD_DI), full(LANES, SSD_DI),
    ]
    args = [z, xbc, dt, dtt, cpre, conv_w, conv_b, dtb_row, dtb_col, alog_row, alog_col, d_exp, ng, expand]
    if has_h0:
        in_specs.append(pl.BlockSpec((1, SSD_N, SSD_DI), lambda b, c: (b, 0, 0)))
        args.append(h0t)
    return pl.pallas_call(
        kern,
        grid=(bsz, n_chunks),
        in_specs=in_specs,
        out_specs=[pl.BlockSpec((1, q, SSD_DI), lambda b, c: (b, c, 0)),
                   pl.BlockSpec((1, SSD_N, SSD_DI), lambda b, c: (b, 0, 0))],
        out_shape=[jax.ShapeDtypeStruct((bsz, seq, SSD_DI), BF16),
                   jax.ShapeDtypeStruct((bsz, SSD_N, SSD_DI), F32)],
        scratch_shapes=[pltpu.VMEM((SSD_N, SSD_DI), F32), pltpu.VMEM((CONV_PAD + q, SSD_CONV_DIM), F32)],
        compiler_params=_params("arbitrary", "arbitrary"),
        name="ssd_scan",
    )(*args)


def _first_max(vals):
    m = vals[0]
    for v in vals[1:]:
        m = jnp.maximum(m, v)
    flags = []
    taken = jnp.zeros_like(m)
    for v in vals:
        f = jnp.where(v == m, 1.0 - taken, 0.0)
        taken = taken + f
        flags.append(f)
    return m, flags


def _sum_rows(rows):
    tot = rows[0]
    for r in rows[1:]:
        tot = tot + r
    return tot


def _router_rows(x, wh_ref, wl_ref, b_ref):
    xh = x.astype(BF16)
    xl = (x - xh.astype(F32)).astype(BF16)
    logits = (jnp.dot(xh, wh_ref[...], preferred_element_type=F32)
              + jnp.dot(xh, wl_ref[...], preferred_element_type=F32)
              + jnp.dot(xl, wh_ref[...], preferred_element_type=F32))
    lt = logits.T
    rows = [lt[e:e + 1, :] for e in range(N_EXPERTS)]
    mx = rows[0]
    for r in rows[1:]:
        mx = jnp.maximum(mx, r)
    ex = [jnp.exp(r - mx) for r in rows]
    den = _sum_rows(ex)
    probs = [r / den for r in ex]
    sel = [probs[e] + b_ref[e:e + 1, :] for e in range(N_EXPERTS)]
    scores, chosen = [], []
    for g in range(N_EGROUPS):
        v = sel[g * EXP_PER_GROUP:(g + 1) * EXP_PER_GROUP]
        m1, f1 = _first_max(v)
        rest = [jnp.where(f > 0.0, -jnp.inf, vi) for f, vi in zip(f1, v)]
        m2, f2 = _first_max(rest)
        scores.append(m1 + m2)
        chosen.extend([a + b for a, b in zip(f1, f2)])
    return probs, chosen, scores


def _router_kernel(x_ref, wh_ref, wl_ref, b_ref, o_ref):
    probs, chosen, scores = _router_rows(x_ref[...], wh_ref, wl_ref, b_ref)
    _, gflag = _first_max(scores)
    gates = [probs[e] * chosen[e] * gflag[e // EXP_PER_GROUP] for e in range(N_EXPERTS)]
    tot = _sum_rows(gates)
    for e in range(N_EXPERTS):
        o_ref[e:e + 1, :] = gates[e] / tot


def _route_group_kernel(x_ref, wh_ref, wl_ref, b_ref, grp_ref, gate_ref, *, tm):
    probs, chosen, scores = _router_rows(x_ref[...], wh_ref, wl_ref, b_ref)
    _, gflag = _first_max(scores)
    grp = _sum_rows([gflag[g] * float(g) for g in range(1, N_EGROUPS)])
    grp_ref[...] = grp.astype(jnp.int32)
    local = [_sum_rows([probs[g * EXP_PER_GROUP + le] * chosen[g * EXP_PER_GROUP + le] * gflag[g]
                        for g in range(N_EGROUPS)]) for le in range(EXP_PER_GROUP)]
    tot = _sum_rows(local)
    rows = jnp.concatenate([r / tot for r in local] + [jnp.zeros((LANES - EXP_PER_GROUP, tm), F32)], axis=0)
    gate_ref[...] = rows.T


def _route_group(x, wr_hi, wr_lo, b_col, *, tm):
    t, d = x.shape
    return pl.pallas_call(
        functools.partial(_route_group_kernel, tm=tm),
        grid=(t // tm,),
        in_specs=[pl.BlockSpec((tm, d), lambda i: (i, 0)),
                  pl.BlockSpec((d, LANES), lambda i: (0, 0)),
                  pl.BlockSpec((d, LANES), lambda i: (0, 0)),
                  pl.BlockSpec((N_EXPERTS, 1), lambda i: (0, 0))],
        out_specs=[pl.BlockSpec((1, tm), lambda i: (0, i)), pl.BlockSpec((tm, LANES), lambda i: (i, 0))],
        out_shape=[jax.ShapeDtypeStruct((1, t), jnp.int32), jax.ShapeDtypeStruct((t, LANES), F32)],
        compiler_params=_params("arbitrary"),
        name="route_group",
    )(x, wr_hi, wr_lo, b_col)


def _moe_sorted_kernel(tg_ref, nu_ref, x_ref, pos_ref, gate_ref, wg_ref, wu_ref, wd_ref, o_ref,
                       xsb_ref, g3_ref, xt_ref, gcol_ref, acc_ref, *, tr):
    sb = pl.program_id(0)
    j = pl.program_id(1)
    e = pl.program_id(2)
    used = j < nu_ref[sb]

    @pl.when(jnp.logical_and(j == 0, e == 0))
    def _():
        xsb_ref[...] = x_ref[...].astype(BF16)
        hi, mid, lo = _split3(gate_ref[...])
        g3_ref[0] = hi
        g3_ref[1] = mid
        g3_ref[2] = lo

    @pl.when(jnp.logical_and(used, e == 0))
    def _():
        n_src = xsb_ref.shape[0]
        row = j * tr + lax.broadcasted_iota(jnp.int32, (tr, n_src), 0)
        pick = jnp.where(pos_ref[0] == row, 1.0, 0.0).astype(BF16)
        xt_ref[...] = jnp.dot(pick, xsb_ref[...], preferred_element_type=F32).astype(BF16)
        gcol_ref[...] = (jnp.dot(pick, g3_ref[0], preferred_element_type=F32)
                         + jnp.dot(pick, g3_ref[1], preferred_element_type=F32)
                         + jnp.dot(pick, g3_ref[2], preferred_element_type=F32))
        acc_ref[...] = jnp.zeros(acc_ref.shape, F32)

    @pl.when(used)
    def _():
        xb = xt_ref[...]
        gates = gcol_ref[...]
        lane = lax.broadcasted_iota(jnp.int32, gates.shape, 1)
        gcol = jnp.sum(jnp.where(lane == e, gates, 0.0), axis=1, keepdims=True)
        hgate = jnp.dot(xb, wg_ref[0], preferred_element_type=F32)
        hup = jnp.dot(xb, wu_ref[0], preferred_element_type=F32)
        acc_ref[...] += _bdot(_silu(hgate) * hup * gcol, wd_ref[0])

    @pl.when(jnp.logical_and(used, e == EXP_PER_GROUP - 1))
    def _():
        o_ref[...] = acc_ref[...].astype(o_ref.dtype)

    @pl.when(jnp.logical_and(jnp.logical_not(used), e == EXP_PER_GROUP - 1))
    def _():
        o_ref[...] = jnp.zeros(o_ref.shape, o_ref.dtype)


def _moe_sorted(x, pos_row, gate4, tile_group, n_used, w_gate, w_up, w_down, *, sb_rows, tr):
    t, d = x.shape
    f = w_gate.shape[2]
    n_sb = t // sb_rows
    cap = sb_rows + N_EGROUPS * tr
    slots = cap // tr

    def w_index(sb, j, e, tg, nu):
        last = sb * slots + jnp.maximum(nu[sb] - 1, 0)
        blk = jnp.where(j < nu[sb], tg[sb * slots + j] * EXP_PER_GROUP + e,
                        tg[last] * EXP_PER_GROUP + EXP_PER_GROUP - 1)
        return (blk, 0, 0)

    grid_spec = pltpu.PrefetchScalarGridSpec(
        num_scalar_prefetch=2,
        grid=(n_sb, slots, EXP_PER_GROUP),
        in_specs=[pl.BlockSpec((sb_rows, d), lambda sb, j, e, tg, nu: (sb, 0)),
                  pl.BlockSpec((1, 1, sb_rows), lambda sb, j, e, tg, nu: (sb, 0, 0)),
                  pl.BlockSpec((sb_rows, LANES), lambda sb, j, e, tg, nu: (sb, 0)),
                  pl.BlockSpec((1, d, f), w_index), pl.BlockSpec((1, d, f), w_index),
                  pl.BlockSpec((1, f, d), w_index)],
        out_specs=pl.BlockSpec((tr, d), lambda sb, j, e, tg, nu: (sb * slots + j, 0)),
        scratch_shapes=[pltpu.VMEM((sb_rows, d), BF16), pltpu.VMEM((3, sb_rows, LANES), BF16),
                        pltpu.VMEM((tr, d), BF16), pltpu.VMEM((tr, LANES), F32), pltpu.VMEM((tr, d), F32)],
    )
    return pl.pallas_call(
        functools.partial(_moe_sorted_kernel, tr=tr),
        grid_spec=grid_spec,
        out_shape=jax.ShapeDtypeStruct((n_sb * cap, d), BF16),
        compiler_params=_params("arbitrary", "arbitrary", "arbitrary"),
        name="moe_sorted_experts",
    )(tile_group, n_used, x, pos_row, gate4, w_gate, w_up, w_down)


def _dispatch_plan(grp, sb_rows, tr):
    n_sb = grp.shape[0] // sb_rows
    slots = (sb_rows + N_EGROUPS * tr) // tr
    oh = (grp.reshape(n_sb, sb_rows, 1) == jnp.arange(N_EGROUPS, dtype=jnp.int32)).astype(jnp.int32)
    csum = jnp.cumsum(oh, axis=1)
    padded = ((csum[:, -1] + tr - 1) // tr) * tr
    gend = jnp.cumsum(padded, axis=1)
    pos = jnp.sum(oh * ((gend - padded)[:, None, :] + csum - 1), axis=2)
    slot_row0 = jnp.arange(slots, dtype=jnp.int32) * tr
    tile_group = jnp.minimum(jnp.sum((slot_row0[None, :, None] >= gend[:, None, :]).astype(jnp.int32), axis=2),
                             N_EGROUPS - 1)
    return pos.astype(jnp.int32), tile_group.reshape(-1).astype(jnp.int32), (gend[:, -1] // tr).astype(jnp.int32)


def _router(x, wr_hi, wr_lo, b_col, *, tm):
    t, d = x.shape
    return pl.pallas_call(
        _router_kernel,
        grid=(t // tm,),
        in_specs=[pl.BlockSpec((tm, d), lambda i: (i, 0)),
                  pl.BlockSpec((d, LANES), lambda i: (0, 0)),
                  pl.BlockSpec((d, LANES), lambda i: (0, 0)),
                  pl.BlockSpec((N_EXPERTS, 1), lambda i: (0, 0))],
        out_specs=pl.BlockSpec((N_EXPERTS, tm), lambda i: (0, i)),
        out_shape=jax.ShapeDtypeStruct((N_EXPERTS, t), F32),
        compiler_params=_params("arbitrary"),
        name="router",
    )(x, wr_hi, wr_lo, b_col)


def _moe_kernel(x_ref, gate_ref, wg_ref, wu_ref, wd_ref, o_ref, xbf_ref):
    e = pl.program_id(1)

    @pl.when(e == 0)
    def _():
        xbf_ref[...] = x_ref[...].astype(BF16)
        o_ref[...] = jnp.zeros(o_ref.shape, F32)

    xb = xbf_ref[...]
    gate = gate_ref[...]
    lane = lax.broadcasted_iota(jnp.int32, gate.shape, 1)
    gcol = jnp.sum(jnp.where(lane == e, gate, 0.0), axis=1, keepdims=True)
    hgate = jnp.dot(xb, wg_ref[0].astype(BF16), preferred_element_type=F32)
    hup = jnp.dot(xb, wu_ref[0].astype(BF16), preferred_element_type=F32)
    o_ref[...] += _bdot(_silu(hgate) * hup * gcol, wd_ref[0])


def _moe(x, gate, w_gate, w_up, w_down, *, tm):
    t, d = x.shape
    f = w_gate.shape[2]
    return pl.pallas_call(
        _moe_kernel,
        grid=(t // tm, N_EXPERTS),
        in_specs=[pl.BlockSpec((tm, d), lambda i, e: (i, 0)),
                  pl.BlockSpec((tm, N_EXPERTS), lambda i, e: (i, 0)),
                  pl.BlockSpec((1, d, f), lambda i, e: (e, 0, 0)),
                  pl.BlockSpec((1, d, f), lambda i, e: (e, 0, 0)),
                  pl.BlockSpec((1, f, d), lambda i, e: (e, 0, 0))],
        out_specs=pl.BlockSpec((tm, d), lambda i, e: (i, 0)),
        out_shape=jax.ShapeDtypeStruct((t, d), F32),
        scratch_shapes=[pltpu.VMEM((tm, d), BF16)],
        compiler_params=_params("arbitrary", "arbitrary"),
        name="moe_experts",
    )(x, gate, w_gate, w_up, w_down)


def _ln_ple_body(x1, ffn, p_ref, g_ref, b_ref, wg_ref, bg_ref, wp_ref, o_ref):
    x = _ln(DN_ALPHA * x1 + ffn, g_ref[...], b_ref[...])
    gate = jax.nn.sigmoid(_bdot(x, wg_ref[...]) + bg_ref[...])
    o_ref[...] = x + gate * _bdot(p_ref[...], wp_ref[...])


def _ln_ple_kernel(x_ref, f_ref, p_ref, *rest):
    _ln_ple_body(x_ref[...], f_ref[...], p_ref, *rest)


def _unsort_ln_ple_kernel(y_ref, pos_ref, x_ref, p_ref, *rest):
    tm, cap = x_ref.shape[0], y_ref.shape[0]
    col = lax.broadcasted_iota(jnp.int32, (tm, cap), 1)
    pick = jnp.where(pos_ref[...] == col, 1.0, 0.0).astype(BF16)
    ffn = jnp.dot(pick, y_ref[...], preferred_element_type=F32)
    _ln_ple_body(x_ref[...], ffn, p_ref, *rest)


def _unsort_ln_ple(y_sorted, pos_col, x, p, g, b, w_gate, b_gate, w_p, *, sb_rows, tm):
    t, d = x.shape
    pd = p.shape[1]
    n_sb = t // sb_rows
    n_sub = sb_rows // tm
    cap = y_sorted.shape[0] // n_sb
    rows = lambda cols: pl.BlockSpec((tm, cols), lambda sb, i: (sb * n_sub + i, 0))
    fixed = lambda r, c: pl.BlockSpec((r, c), lambda sb, i: (0, 0))
    return pl.pallas_call(
        _unsort_ln_ple_kernel,
        grid=(n_sb, n_sub),
        in_specs=[pl.BlockSpec((cap, d), lambda sb, i: (sb, 0)), rows(1), rows(d), rows(pd),
                  fixed(1, d), fixed(1, d), fixed(d, d), fixed(1, d), fixed(pd, d)],
        out_specs=rows(d),
        out_shape=jax.ShapeDtypeStruct((t, d), F32),
        compiler_params=_params("arbitrary", "arbitrary"),
        name="unsort_ln_ple",
    )(y_sorted, pos_col, x, p, g, b, w_gate, b_gate, w_p)


def _ln_ple(x, ffn, p, g, b, w_gate, b_gate, w_p, *, tm):
    t, d = x.shape
    pd = p.shape[1]
    rows = lambda cols: pl.BlockSpec((tm, cols), lambda i: (i, 0))
    fixed = lambda r, c: pl.BlockSpec((r, c), lambda i: (0, 0))
    return pl.pallas_call(
        _ln_ple_kernel,
        grid=(t // tm,),
        in_specs=[rows(d), rows(d), rows(pd), fixed(1, d), fixed(1, d), fixed(d, d), fixed(1, d), fixed(pd, d)],
        out_specs=rows(d),
        out_shape=jax.ShapeDtypeStruct((t, d), F32),
        compiler_params=_params("arbitrary"),
        name="ln_ple",
    )(x, ffn, p, g, b, w_gate, b_gate, w_p)


def _row_tile(rows, want):
    tm = min(rows, want)
    assert rows % tm == 0
    return tm


POOL_TM = 512
MM_TM = 512
QKV_TM = 512
ATTN_TQ = 512
MOE_TM = 1024
MOE_SORT_BLOCK = 2048
MOE_SORT_TR = 256
ROUTER_TM = 1024
PLE_TM = 512
DECODE_PAGES_PER_STEP = 8


def kernel(x_prompt, x_sample, p_prompt, p_sample, state_pool, cache_k, cache_v, page_table, state_ssm, state_conv,
           ln_g, ln_b, w_pool_in, w_pool_grp, pool_scale, w_pool_out, w_diff_in, diff_lambda_q, diff_lambda_k,
           diff_norm_g, w_diff_out, w_ssd_in, ssd_conv_w, ssd_conv_b, ssd_dt_bias, ssd_a_log, ssd_d, ssd_norm_g,
           w_ssd_out, w_router, b_router, w_exp_gate, w_exp_up, w_exp_down, w_ple, w_ple_gate, b_ple_gate):
    bp, seq, d = x_prompt.shape
    bs, tdec, _ = x_sample.shape
    n_pages = page_table.shape[1]
    past = n_pages * cache_k.shape[2]
    tp = bp * seq
    ts = bs * tdec

    def to_tm(a):
        return jnp.swapaxes(a, 0, 1).reshape((ts,) + a.shape[2:])

    def from_tm(a):
        return jnp.swapaxes(a.reshape((tdec, bs) + a.shape[1:]), 0, 1)

    row = lambda v: v.reshape(1, -1)

    wr = jnp.pad(w_router, ((0, 0), (0, LANES - N_EXPERTS)))
    wr_hi = wr.astype(BF16)
    wr_lo = (wr - wr_hi.astype(F32)).astype(BF16)
    br_col = b_router.reshape(N_EXPERTS, 1)

    def finish(x1, i, p, sparse):
        t = x1.shape[0]
        tail = (row(ln_g[i, 1]), row(ln_b[i, 1]), w_ple_gate[i].astype(BF16), row(b_ple_gate[i]),
                w_ple[i].astype(BF16))
        if sparse:
            sb_rows, tr = _row_tile(t, MOE_SORT_BLOCK), MOE_SORT_TR
            grp, gate4 = _route_group(x1, wr_hi, wr_lo, br_col, tm=_row_tile(t, ROUTER_TM))
            pos, tile_group, n_used = _dispatch_plan(grp[0], sb_rows, tr)
            y_sorted = _moe_sorted(x1, pos.reshape(-1, 1, sb_rows), gate4, tile_group, n_used,
                                   w_exp_gate[i].astype(BF16), w_exp_up[i].astype(BF16),
                                   w_exp_down[i].astype(BF16), sb_rows=sb_rows, tr=tr)
            return _unsort_ln_ple(y_sorted, pos.reshape(t, 1), x1, p, *tail, sb_rows=sb_rows,
                                  tm=_row_tile(sb_rows, PLE_TM))
        gate = _router(x1, wr_hi, wr_lo, br_col, tm=_row_tile(t, ROUTER_TM)).T
        ffn = _moe(x1, gate, w_exp_gate[i], w_exp_up[i], w_exp_down[i], tm=_row_tile(t, MOE_TM))
        return _ln_ple(x1, ffn, p, *tail, tm=_row_tile(t, PLE_TM))

    xp = x_prompt.reshape(tp, d)
    xs = to_tm(x_sample)
    pos_p = jnp.arange(seq, dtype=jnp.int32)
    pos_s = jnp.repeat(past + jnp.arange(tdec, dtype=jnp.int32), bs)
    outs = {k: [] for k in ("pool_p", "pool_s", "k_p", "v_p", "k_s", "v_s", "ssm_p", "conv_p", "ssm_s", "conv_s")}

    for i in range(DEPTH):
        kind, j = i % N_MIXERS, i // N_MIXERS
        g1, b1 = row(ln_g[i, 0]), row(ln_b[i, 0])
        if kind == 0:
            wts = (w_pool_in[j].astype(BF16), w_pool_grp[j].astype(BF16), row(pool_scale[j]),
                   w_pool_out[j].astype(BF16), g1, b1)
            x1p, newp = _pool_layer(xp.reshape(bp, seq, d), jnp.zeros((bp, POOL_BUF + 1, d), F32), *wts,
                                    tm=_row_tile(seq, POOL_TM), stride=1, pos0=0)
            pre_s = jnp.swapaxes(state_pool[j], 0, 1).reshape(1, POOL_BUF * bs, d)
            pre_s = jnp.pad(pre_s, ((0, 0), (bs, 0), (0, 0)))
            x1s, news = _pool_layer(xs.reshape(1, ts, d), pre_s, *wts, tm=ts, stride=bs, pos0=past)
            outs["pool_p"].append(newp)
            outs["pool_s"].append(jnp.swapaxes(news.reshape(POOL_BUF, bs, d), 0, 1))
            x1p = x1p.reshape(tp, d)
            x1s = x1s.reshape(ts, d)
        elif kind == 1:
            lam_init = 0.8 - 0.6 * math.exp(-0.3 * i)
            w_in = w_diff_in[j].astype(BF16)
            w_out = w_diff_out[j].astype(BF16)
            lq, lk, ng = diff_lambda_q[j], diff_lambda_k[j], row(diff_norm_g[j])
            cos_p, sin_p = _rope_tables(pos_p)
            q, k, v = _qkv_rope(xp.reshape(bp, seq, d), w_in, cos_p, sin_p, tm=_row_tile(seq, QKV_TM))
            o = _diff_attn_prompt(q, k, v, lq, lk, ng, lam_init, tq=_row_tile(seq, ATTN_TQ))
            x1p = _mm_res_ln(o.reshape(tp, d), w_out, xp, g1, b1, tm=_row_tile(tp, MM_TM))
            outs["k_p"].append(k.reshape(bp, seq, DIFF_HEADS, 2, DIFF_DH))
            outs["v_p"].append(v.reshape(bp, seq, DIFF_HEADS, DIFF_VD))

            cos_s, sin_s = _rope_tables(pos_s)
            q, k, v = _qkv_rope(xs.reshape(1, ts, d), w_in, cos_s, sin_s, tm=ts)
            q, k, v = from_tm(q[0]), from_tm(k[0]), from_tm(v[0])
            q5 = q.reshape(bs, tdec, DIFF_HEADS * 2, DIFF_DH)
            eye = jnp.eye(DIFF_HEADS * 2, dtype=F32)
            qbd = jnp.einsum('bthd,hg->bhtgd', q5, eye)
            qbd = jnp.pad(qbd, ((0, 0), (0, 0), (0, DEC_TPAD - tdec), (0, 0), (0, 0)))
            qbd = qbd.reshape(bs, DEC_ROWS, d)
            kt_pages = jnp.transpose(cache_k, (0, 1, 3, 4, 5, 2)).reshape(-1, d, PAGE_SIZE)
            v_pages = cache_v.reshape(-1, PAGE_SIZE * DIFF_HEADS, DIFF_VD)
            knew = jnp.pad(jnp.swapaxes(k, 1, 2), ((0, 0), (0, 0), (0, PAGE_SIZE - tdec)))
            vnew = jnp.pad(v, ((0, 0), (0, PAGE_SIZE - tdec), (0, 0))).reshape(bs, PAGE_SIZE * DIFF_HEADS, DIFF_VD)
            o = _diff_attn_sample(qbd, kt_pages, v_pages, page_table, knew, vnew, lq, lk, ng, lam_init,
                                  pages_per_step=DECODE_PAGES_PER_STEP, page0=j * cache_k.shape[1])
            o = to_tm(o[:, :tdec])
            x1s = _mm_res_ln(o, w_out, xs, g1, b1, tm=ts)
            outs["k_s"].append(k.reshape(bs, tdec, DIFF_HEADS, 2, DIFF_DH))
            outs["v_s"].append(v.reshape(bs, tdec, DIFF_HEADS, DIFF_VD))
        else:
            w_in = w_ssd_in[j]
            w_z = w_in[:, :SSD_DI].astype(BF16)
            w_xbc = w_in[:, SSD_DI:SSD_DI + SSD_CONV_DIM].astype(BF16)
            w_dt = jnp.pad(w_in[:, SSD_DI + SSD_CONV_DIM:], ((0, 0), (0, LANES - SSD_H))).astype(BF16)
            w_out = w_ssd_out[j].astype(BF16)
            dtb_row = jnp.pad(ssd_dt_bias[j], (0, LANES - SSD_H)).reshape(1, LANES)
            alog_row = jnp.pad(ssd_a_log[j], (0, LANES - SSD_H)).reshape(1, LANES)
            dtb_col = ssd_dt_bias[j].reshape(SSD_H, 1)
            alog_col = ssd_a_log[j].reshape(SSD_H, 1)
            d_exp = jnp.repeat(ssd_d[j], SSD_P).reshape(1, SSD_DI)
            ng = row(ssd_norm_g[j])
            expand = (jnp.arange(SSD_DI)[None, :] // SSD_P == jnp.arange(LANES)[:, None]).astype(BF16)
            shared = (ssd_conv_w[j], row(ssd_conv_b[j]), dtb_row, dtb_col, alog_row, alog_col, d_exp, ng, expand)

            def in_proj(x, tm):
                z = _matmul(x, w_z, tm=tm, tn=1024)
                xbc = _matmul(x, w_xbc, tm=tm, tn=1024)
                dt = _matmul(x, w_dt, tm=tm, tn=LANES)
                return z, xbc, dt

            def hlast_to_state(hl, b):
                return jnp.transpose(hl.reshape(b, SSD_N, SSD_H, SSD_P), (0, 2, 3, 1))

            z, xbc, dt = in_proj(xp, _row_tile(tp, MM_TM))
            dtt = jnp.swapaxes(dt.reshape(bp, seq, LANES)[:, :, :SSD_H], 1, 2)
            y, hl = _ssd_scan(z.reshape(bp, seq, SSD_DI), xbc.reshape(bp, seq, SSD_CONV_DIM),
                              dt.reshape(bp, seq, LANES), dtt, jnp.zeros((bp, CONV_PAD, SSD_CONV_DIM), F32),
                              *shared, None, valid_len=None)
            x1p = _mm_res_ln(y.reshape(tp, SSD_DI), w_out, xp, g1, b1, tm=_row_tile(tp, MM_TM))
            outs["ssm_p"].append(hlast_to_state(hl, bp))
            outs["conv_p"].append(xbc.reshape(bp, seq, SSD_CONV_DIM)[:, seq - (SSD_CONV - 1):])

            z, xbc, dt = in_proj(xs, ts)
            z, xbc, dt = from_tm(z), from_tm(xbc), from_tm(dt)
            padt = ((0, 0), (0, SSD_CHUNK - tdec), (0, 0))
            dtp = jnp.pad(dt, padt)
            cpre = jnp.pad(state_conv[j], ((0, 0), (CONV_PAD - (SSD_CONV - 1), 0), (0, 0)))
            h0t = jnp.transpose(state_ssm[j], (0, 3, 1, 2)).reshape(bs, SSD_N, SSD_DI)
            y, hl = _ssd_scan(jnp.pad(z, padt), jnp.pad(xbc, padt), dtp, jnp.swapaxes(dtp[:, :, :SSD_H], 1, 2),
                              cpre, *shared, h0t, valid_len=tdec)
            x1s = _mm_res_ln(to_tm(y[:, :tdec]), w_out, xs, g1, b1, tm=ts)
            outs["ssm_s"].append(hlast_to_state(hl, bs))
            ext = jnp.concatenate([state_conv[j], xbc], axis=1)
            outs["conv_s"].append(ext[:, -(SSD_CONV - 1):])

        xp = finish(x1p, i, p_prompt[i].reshape(tp, PLE_DIM), sparse=True)
        xs = finish(x1s, i, to_tm(p_sample[i]), sparse=False)

    st = lambda name: jnp.stack(outs[name])
    return (xp.reshape(bp, seq, d), from_tm(xs), st("pool_p"), st("pool_s"), st("k_p"), st("v_p"), st("k_s"),
            st("v_s"), st("ssm_p"), st("conv_p"), st("ssm_s"), st("conv_s"))
```

```python
import functools
import math

import jax
import jax.numpy as jnp
from jax import lax
from jax.experimental import pallas as pl
from jax.experimental.pallas import tpu as pltpu

F32 = jnp.float32
BF16 = jnp.bfloat16

D_MODEL = 1024
DEPTH = 4
N_MIXERS = 3
PAGE_SIZE = 128

POOL_WINDOWS = (2, 4, 8, 16)
POOL_GD = D_MODEL // len(POOL_WINDOWS)
POOL_BUF = max(POOL_WINDOWS) - 1

DIFF_DH = 64
DIFF_VD = 2 * DIFF_DH
DIFF_HEADS = D_MODEL // DIFF_VD
ROPE_THETA = 10000.0

SSD_DI = 2 * D_MODEL
SSD_P = 64
SSD_H = SSD_DI // SSD_P
SSD_G = 4
SSD_N = 128
SSD_CONV = 4
SSD_CONV_DIM = SSD_DI + 2 * SSD_G * SSD_N
SSD_CHUNK = 128

N_EXPERTS = 16
N_EGROUPS = 4
EXP_PER_GROUP = N_EXPERTS // N_EGROUPS
D_FF_EXPERT = D_MODEL // 2

PLE_DIM = 256
DN_ALPHA = (2 * DEPTH) ** 0.25
NORM_EPS = 1e-5

LANES = 128
SUBLANES = 8
VMEM_LIMIT = 56 * 1024 * 1024


def _params(*sem):
    return pltpu.CompilerParams(dimension_semantics=sem, vmem_limit_bytes=VMEM_LIMIT)


def _bdot(a, b):
    return jnp.dot(a.astype(BF16), b.astype(BF16), preferred_element_type=F32)


def _bdot_nt(a, b):
    return lax.dot_general(a.astype(BF16), b.astype(BF16), (((1,), (1,)), ((), ())),
                           preferred_element_type=F32)


def _split3(x):
    hi = x.astype(BF16)
    r = x - hi.astype(F32)
    mid = r.astype(BF16)
    lo = (r - mid.astype(F32)).astype(BF16)
    return hi, mid, lo


def _dot_exact_rhs(x, m):
    hi, mid, lo = _split3(x)
    return (jnp.dot(hi, m, preferred_element_type=F32) + jnp.dot(mid, m, preferred_element_type=F32)
            + jnp.dot(lo, m, preferred_element_type=F32))


def _dot_exact_lhs(m, x):
    hi, mid, lo = _split3(x)
    return (jnp.dot(m, hi, preferred_element_type=F32) + jnp.dot(m, mid, preferred_element_type=F32)
            + jnp.dot(m, lo, preferred_element_type=F32))


def _ln(v, g, b):
    mu = jnp.mean(v, axis=-1, keepdims=True)
    c = v - mu
    var = jnp.mean(c * c, axis=-1, keepdims=True)
    return c * lax.rsqrt(var + NORM_EPS) * g + b


def _silu(v):
    return v * jax.nn.sigmoid(v)


def _softplus(v):
    return jnp.maximum(v, 0.0) + jnp.log(1.0 + jnp.exp(-jnp.abs(v)))


def _pool_kernel(x_ref, pre_ref, win_ref, wgrp_ref, scale_ref, wout_ref, g_ref, b_ref,
                 y_ref, newp_ref, ext_ref, *, tm, stride, pos0, n_tiles):
    i = pl.program_id(1)
    pre_rows = (POOL_BUF + 1) * stride

    @pl.when(i == 0)
    def _():
        ext_ref[0:pre_rows, :] = pre_ref[0]

    @pl.when(i > 0)
    def _():
        ext_ref[0:pre_rows, :] = ext_ref[tm:tm + pre_rows, :]

    x = x_ref[0]
    u = _bdot(x, win_ref[...])
    ext_ref[pre_rows:pre_rows + tm, :] = u
    row = lax.broadcasted_iota(jnp.int32, (tm, 1), 0)
    pos = pos0 + (i * tm + row) // stride
    mixed = []
    for gi, w in enumerate(POOL_WINDOWS):
        c0 = gi * POOL_GD
        ug = u[:, c0:c0 + POOL_GD]
        s = ug
        for k in range(1, w):
            r0 = pre_rows - k * stride
            s = s + ext_ref[r0:r0 + tm, c0:c0 + POOL_GD]
        cnt = jnp.minimum(w, pos + 1).astype(F32)
        pooled = s / cnt - ug
        mixed.append(_bdot(pooled, wgrp_ref[gi]))
    mixed = jnp.concatenate(mixed, axis=1) * scale_ref[...]
    mix = _bdot(mixed, wout_ref[...])
    y_ref[0] = _ln(DN_ALPHA * x + mix, g_ref[...], b_ref[...])

    @pl.when(i == n_tiles - 1)
    def _():
        newp_ref[0] = ext_ref[tm + stride:tm + pre_rows, :]


def _pool_layer(x, prefix, w_in, w_grp, scale, w_out, g, b, *, tm, stride, pos0):
    bsz, rows, d = x.shape
    n_tiles = rows // tm
    pre_rows = (POOL_BUF + 1) * stride
    kern = functools.partial(_pool_kernel, tm=tm, stride=stride, pos0=pos0, n_tiles=n_tiles)
    full = lambda *shape: pl.BlockSpec(shape, lambda bi, i: (0,) * len(shape))
    return pl.pallas_call(
        kern,
        grid=(bsz, n_tiles),
        in_specs=[
            pl.BlockSpec((1, tm, d), lambda bi, i: (bi, i, 0)),
            pl.BlockSpec((1, pre_rows, d), lambda bi, i: (bi, 0, 0)),
            full(d, d), full(len(POOL_WINDOWS), POOL_GD, POOL_GD), full(1, d), full(d, d),
            full(1, d), full(1, d),
        ],
        out_specs=[
            pl.BlockSpec((1, tm, d), lambda bi, i: (bi, i, 0)),
            pl.BlockSpec((1, POOL_BUF * stride, d), lambda bi, i: (bi, 0, 0)),
        ],
        out_shape=[jax.ShapeDtypeStruct((bsz, rows, d), F32),
                   jax.ShapeDtypeStruct((bsz, POOL_BUF * stride, d), F32)],
        scratch_shapes=[pltpu.VMEM((pre_rows + tm, d), F32)],
        compiler_params=_params("arbitrary", "arbitrary"),
        name="pool_layer",
    )(x, prefix, w_in, w_grp, scale, w_out, g, b)


def _mm_kernel(x_ref, w_ref, o_ref):
    o_ref[...] = _bdot(x_ref[...], w_ref[...]).astype(o_ref.dtype)


def _matmul(x, w, *, tm, tn, out_dtype=F32):
    t, k = x.shape
    n = w.shape[1]
    return pl.pallas_call(
        _mm_kernel,
        grid=(n // tn, t // tm),
        in_specs=[pl.BlockSpec((tm, k), lambda j, i: (i, 0)),
                  pl.BlockSpec((k, tn), lambda j, i: (0, j))],
        out_specs=pl.BlockSpec((tm, tn), lambda j, i: (i, j)),
        out_shape=jax.ShapeDtypeStruct((t, n), out_dtype),
        compiler_params=_params("arbitrary", "arbitrary"),
        name="matmul",
    )(x, w)


def _mm_res_ln_kernel(a_ref, w_ref, x_ref, g_ref, b_ref, o_ref):
    mix = _bdot(a_ref[...], w_ref[...])
    o_ref[...] = _ln(DN_ALPHA * x_ref[...] + mix, g_ref[...], b_ref[...])


def _mm_res_ln(a, w, x, g, b, *, tm):
    t, k = a.shape
    d = w.shape[1]
    return pl.pallas_call(
        _mm_res_ln_kernel,
        grid=(t // tm,),
        in_specs=[pl.BlockSpec((tm, k), lambda i: (i, 0)),
                  pl.BlockSpec((k, d), lambda i: (0, 0)),
                  pl.BlockSpec((tm, d), lambda i: (i, 0)),
                  pl.BlockSpec((1, d), lambda i: (0, 0)),
                  pl.BlockSpec((1, d), lambda i: (0, 0))],
        out_specs=pl.BlockSpec((tm, d), lambda i: (i, 0)),
        out_shape=jax.ShapeDtypeStruct((t, d), F32),
        compiler_params=_params("arbitrary"),
        name="mm_res_ln",
    )(a, w, x, g, b)


def _qkv_kernel(x_ref, w_ref, cos_ref, sin_ref, q_ref, k_ref, v_ref, *, tm):
    x = x_ref[0].astype(BF16)
    cos = cos_ref[...]
    sin = sin_ref[...]
    lane = lax.broadcasted_iota(jnp.int32, (tm, LANES), 1)
    first_half = (lane % DIFF_DH) < (DIFF_DH // 2)

    def rope(y):
        outs = []
        for c in range(D_MODEL // LANES):
            yc = y[:, c * LANES:(c + 1) * LANES]
            sw = jnp.where(first_half, pltpu.roll(yc, LANES - DIFF_DH // 2, 1),
                           pltpu.roll(yc, DIFF_DH // 2, 1))
            outs.append(yc * cos + sw * sin)
        return jnp.concatenate(outs, axis=1)

    q_ref[0] = rope(jnp.dot(x, w_ref[:, 0:D_MODEL], preferred_element_type=F32))
    k_ref[0] = rope(jnp.dot(x, w_ref[:, D_MODEL:2 * D_MODEL], preferred_element_type=F32))
    v_ref[0] = jnp.dot(x, w_ref[:, 2 * D_MODEL:3 * D_MODEL], preferred_element_type=F32)


def _rope_tables(pos):
    half = DIFF_DH // 2
    inv = ROPE_THETA ** (-jnp.arange(half, dtype=F32) * 2.0 / DIFF_DH)
    ang = pos.astype(F32)[:, None] * inv[None, :]
    cos, sin = jnp.cos(ang), jnp.sin(ang)
    reps = LANES // DIFF_DH
    cos_t = jnp.tile(jnp.concatenate([cos, cos], axis=1), (1, reps))
    sin_t = jnp.tile(jnp.concatenate([-sin, sin], axis=1), (1, reps))
    return cos_t, sin_t


def _qkv_rope(x, w, cos_t, sin_t, *, tm):
    bsz, rows, d = x.shape
    kern = functools.partial(_qkv_kernel, tm=tm)
    spec = pl.BlockSpec((1, tm, d), lambda bi, i: (bi, i, 0))
    tab = pl.BlockSpec((tm, LANES), lambda bi, i: (i, 0))
    shp = jax.ShapeDtypeStruct((bsz, rows, d), F32)
    return pl.pallas_call(
        kern,
        grid=(bsz, rows // tm),
        in_specs=[spec, pl.BlockSpec((d, 3 * d), lambda bi, i: (0, 0)), tab, tab],
        out_specs=[spec, spec, spec],
        out_shape=[shp, shp, shp],
        compiler_params=_params("arbitrary", "arbitrary"),
        name="qkv_rope",
    )(x, w, cos_t, sin_t)


def _diff_lambda(lq_ref, lk_ref, lam_init):
    dots = jnp.sum(lq_ref[...] * lk_ref[...], axis=-1, keepdims=True)
    e = jnp.exp(dots)
    return e[0:1, :] - e[1:2, :] + lam_init


def _head_norm(o, ng, lam_init):
    ms = jnp.mean(o * o, axis=-1, keepdims=True)
    return o * lax.rsqrt(ms + NORM_EPS) * ng * (1.0 - lam_init)


def _attn_kernel(q_ref, k_ref, v_ref, lq_ref, lk_ref, ng_ref, o_ref, *, tq, lam_init):
    qi = pl.program_id(2)
    q = q_ref[0] * (DIFF_DH ** -0.5)
    q1 = q[:, :DIFF_DH].astype(BF16)
    q2 = q[:, DIFF_DH:].astype(BF16)
    ones = jnp.ones((tq, DIFF_VD), BF16)

    def scores(kb):
        start = pl.multiple_of(kb * tq, tq)
        k = k_ref[0, pl.ds(start, tq), :]
        v1 = jnp.concatenate([v_ref[0, pl.ds(start, tq), :].astype(BF16), ones], axis=1)
        s = jnp.concatenate([_bdot_nt(q1, k[:, :DIFF_DH]), _bdot_nt(q2, k[:, DIFF_DH:])], axis=0)
        return s, v1

    def update(s, v1, m, a):
        mn = jnp.maximum(m, jnp.max(s, axis=1, keepdims=True))
        p = jnp.exp(s - mn).astype(BF16)
        a = jnp.exp(m - mn) * a + jnp.dot(p, v1, preferred_element_type=F32)
        return mn, a

    def body(kb, carry):
        s, v1 = scores(kb)
        return update(s, v1, *carry)

    init = (jnp.full((2 * tq, 1), -jnp.inf, F32), jnp.zeros((2 * tq, 2 * DIFF_VD), F32))
    m, a = lax.fori_loop(0, qi, body, init)
    s, v1 = scores(qi)
    row = lax.broadcasted_iota(jnp.int32, (2 * tq, tq), 0) % tq
    col = lax.broadcasted_iota(jnp.int32, (2 * tq, tq), 1)
    m, a = update(jnp.where(col <= row, s, -jnp.inf), v1, m, a)
    lam = _diff_lambda(lq_ref, lk_ref, lam_init)
    o = (a[:tq, :DIFF_VD] / a[:tq, DIFF_VD:DIFF_VD + 1]
         - lam * (a[tq:, :DIFF_VD] / a[tq:, DIFF_VD:DIFF_VD + 1]))
    o_ref[0] = _head_norm(o, ng_ref[...], lam_init).astype(o_ref.dtype)


def _diff_attn_prompt(q, k, v, lq, lk, ng, lam_init, *, tq):
    bsz, seq, d = q.shape
    kern = functools.partial(_attn_kernel, tq=tq, lam_init=lam_init)
    small = lambda *shape: pl.BlockSpec(shape, lambda bi, h, i: (0,) * len(shape))
    return pl.pallas_call(
        kern,
        grid=(bsz, DIFF_HEADS, seq // tq),
        in_specs=[pl.BlockSpec((1, tq, DIFF_VD), lambda bi, h, i: (bi, i, h)),
                  pl.BlockSpec((1, seq, DIFF_VD), lambda bi, h, i: (bi, 0, h)),
                  pl.BlockSpec((1, seq, DIFF_VD), lambda bi, h, i: (bi, 0, h)),
                  small(2, DIFF_DH), small(2, DIFF_DH), small(1, DIFF_VD)],
        out_specs=pl.BlockSpec((1, tq, DIFF_VD), lambda bi, h, i: (bi, i, h)),
        out_shape=jax.ShapeDtypeStruct((bsz, seq, d), BF16),
        compiler_params=_params("arbitrary", "arbitrary", "arbitrary"),
        name="diff_attn_prompt",
    )(q, k, v, lq, lk, ng)


DEC_TPAD = SUBLANES // 2
DEC_ROWS = DIFF_HEADS * 2 * DEC_TPAD


def _decode_kernel(pt_ref, q_ref, *refs, pages_per_step, n_steps, lam_init):
    g = pages_per_step
    k_refs = refs[:g]
    v_refs = refs[g:2 * g]
    knew_ref, vnew_ref, rep_ref, lq_ref, lk_ref, ng_ref, o_ref, m_ref, l_ref, acc_ref = refs[2 * g:]
    s = pl.program_id(1)
    scale = DIFF_DH ** -0.5
    rep = rep_ref[...]
    row_head = lax.broadcasted_iota(jnp.int32, (DEC_ROWS, PAGE_SIZE * DIFF_HEADS), 0) // (2 * DEC_TPAD)
    col_head = lax.broadcasted_iota(jnp.int32, (DEC_ROWS, PAGE_SIZE * DIFF_HEADS), 1) % DIFF_HEADS
    own_head = row_head == col_head

    @pl.when(s == 0)
    def _():
        m_ref[...] = jnp.full(m_ref.shape, -jnp.inf, F32)
        l_ref[...] = jnp.zeros(l_ref.shape, F32)
        acc_ref[...] = jnp.zeros(acc_ref.shape, F32)

    q = (q_ref[0] * scale).astype(BF16)

    def block(ks, vs, mask):
        sc = jnp.concatenate([_bdot(q, k[0]) for k in ks], axis=1)
        if mask is not None:
            sc = jnp.where(mask, sc, -jnp.inf)
        m_old = m_ref[...]
        mn = jnp.maximum(m_old, jnp.max(sc, axis=1, keepdims=True))
        alpha = jnp.exp(m_old - mn)
        p = jnp.exp(sc - mn)
        pv = None
        for gi in range(len(vs)):
            pe = jnp.dot(p[:, gi * PAGE_SIZE:(gi + 1) * PAGE_SIZE].astype(BF16), rep, preferred_element_type=F32)
            part = _bdot(jnp.where(own_head, pe, 0.0), vs[gi][0])
            pv = part if pv is None else pv + part
        l_ref[...] = alpha * l_ref[...] + jnp.sum(p, axis=1, keepdims=True)
        acc_ref[...] = alpha * acc_ref[...] + pv
        m_ref[...] = mn

    block(k_refs, v_refs, None)

    @pl.when(s == n_steps - 1)
    def _():
        t_row = lax.broadcasted_iota(jnp.int32, (DEC_ROWS, PAGE_SIZE), 0) % DEC_TPAD
        key = lax.broadcasted_iota(jnp.int32, (DEC_ROWS, PAGE_SIZE), 1)
        block([knew_ref], [vnew_ref], key <= t_row)
        lam = _diff_lambda(lq_ref, lk_ref, lam_init)
        ng = ng_ref[...]
        for h in range(DIFF_HEADS):
            r0 = h * SUBLANES
            cols = slice(h * DIFF_VD, (h + 1) * DIFF_VD)
            a = acc_ref[r0:r0 + SUBLANES, :] / l_ref[r0:r0 + SUBLANES, :]
            o = a - lam * pltpu.roll(a, DEC_TPAD, 0)
            o_ref[0, :, cols] = _head_norm(o, ng, lam_init).astype(o_ref.dtype)


def _diff_attn_sample(qbd, cache_k, cache_v, page_table, knew, vnew, lq, lk, ng, lam_init,
                      *, pages_per_step, page0):
    bsz = qbd.shape[0]
    n_pages = page_table.shape[1]
    g = pages_per_step
    n_steps = n_pages // g
    kern = functools.partial(_decode_kernel, pages_per_step=g, n_steps=n_steps, lam_init=lam_init)
    rep = (jnp.arange(PAGE_SIZE * DIFF_HEADS)[None, :] // DIFF_HEADS == jnp.arange(PAGE_SIZE)[:, None]).astype(BF16)

    def page_spec(gi):
        return pl.BlockSpec((1, D_MODEL, PAGE_SIZE), lambda b, s, pt: (page0 + pt[b, s * g + gi], 0, 0))

    per_b = lambda rows, cols: pl.BlockSpec((1, rows, cols), lambda b, s, pt: (b, 0, 0))
    small = lambda *shape: pl.BlockSpec(shape, lambda b, s, pt: (0,) * len(shape))
    grid_spec = pltpu.PrefetchScalarGridSpec(
        num_scalar_prefetch=1,
        grid=(bsz, n_steps),
        in_specs=([per_b(DEC_ROWS, D_MODEL)] + [page_spec(gi) for gi in range(g)] + [page_spec(gi) for gi in range(g)]
                  + [per_b(D_MODEL, PAGE_SIZE), per_b(D_MODEL, PAGE_SIZE), small(PAGE_SIZE, PAGE_SIZE * DIFF_HEADS),
                     small(2, DIFF_DH), small(2, DIFF_DH), small(1, DIFF_VD)]),
        out_specs=per_b(SUBLANES, D_MODEL),
        scratch_shapes=[pltpu.VMEM((DEC_ROWS, 1), F32), pltpu.VMEM((DEC_ROWS, 1), F32),
                        pltpu.VMEM((DEC_ROWS, DIFF_VD), F32)],
    )
    return pl.pallas_call(
        kern,
        grid_spec=grid_spec,
        out_shape=jax.ShapeDtypeStruct((bsz, SUBLANES, D_MODEL), BF16),
        compiler_params=_params("arbitrary", "arbitrary"),
        name="diff_attn_sample",
    )(page_table, qbd, *([cache_k] * g), *([cache_v] * g), knew, vnew, rep, lq, lk, ng)


CONV_PAD = SUBLANES


def _ssd_kernel(*refs, q, n_chunks, valid_len, has_h0):
    if has_h0:
        (z_ref, xbc_ref, dt_ref, dtt_ref, cpre_ref, cw_ref, cb_ref, dtb_row_ref, dtb_col_ref,
         alog_row_ref, alog_col_ref, dexp_ref, ng_ref, expand_ref, h0_ref,
         y_ref, hl_ref, s_ref, cext_ref) = refs
    else:
        (z_ref, xbc_ref, dt_ref, dtt_ref, cpre_ref, cw_ref, cb_ref, dtb_row_ref, dtb_col_ref,
         alog_row_ref, alog_col_ref, dexp_ref, ng_ref, expand_ref,
         y_ref, hl_ref, s_ref, cext_ref) = refs
        h0_ref = None
    c = pl.program_id(1)

    @pl.when(c == 0)
    def _():
        if has_h0:
            s_ref[...] = h0_ref[0]
        else:
            s_ref[...] = jnp.zeros(s_ref.shape, F32)
        cext_ref[0:CONV_PAD, :] = cpre_ref[0]

    @pl.when(c > 0)
    def _():
        cext_ref[0:CONV_PAD, :] = cext_ref[q:q + CONV_PAD, :]

    cext_ref[CONV_PAD:CONV_PAD + q, :] = xbc_ref[0]
    xbc = cb_ref[...]
    for kk in range(SSD_CONV):
        r0 = CONV_PAD - (SSD_CONV - 1) + kk
        xbc = xbc + cext_ref[r0:r0 + q, :] * cw_ref[kk:kk + 1, :]
    xbc = _silu(xbc)
    xs = xbc[:, :SSD_DI]
    bm = xbc[:, SSD_DI:SSD_DI + SSD_G * SSD_N]
    cm = xbc[:, SSD_DI + SSD_G * SSD_N:]

    dtc = _softplus(dt_ref[0] + dtb_row_ref[...])
    dtr = _softplus(dtt_ref[0] + dtb_col_ref[...])
    if valid_len is not None:
        t_col = c * q + lax.broadcasted_iota(jnp.int32, (q, LANES), 0)
        t_row = c * q + lax.broadcasted_iota(jnp.int32, (SSD_H, q), 1)
        dtc = jnp.where(t_col < valid_len, dtc, 0.0)
        dtr = jnp.where(t_row < valid_len, dtr, 0.0)
    ii = lax.broadcasted_iota(jnp.int32, (q, q), 0)
    jj = lax.broadcasted_iota(jnp.int32, (q, q), 1)
    causal = jj <= ii
    tri = jnp.where(causal, 1.0, 0.0).astype(BF16)
    tri_t = jnp.where(ii <= jj, 1.0, 0.0).astype(BF16)
    acum_c = _dot_exact_lhs(tri, dtc * (-jnp.exp(alog_row_ref[...])))
    acum_r = _dot_exact_rhs(dtr * (-jnp.exp(alog_col_ref[...])), tri_t)
    total = acum_c[q - 1:q, :]
    expand = expand_ref[...]
    dte_x = _dot_exact_rhs(jnp.exp(total - acum_c) * dtc, expand)
    eac_x = _dot_exact_rhs(jnp.exp(acum_c), expand)
    cdec_x = _dot_exact_rhs(jnp.broadcast_to(jnp.exp(total), (SUBLANES, LANES)), expand)[0:1, :]

    gw = SSD_DI // SSD_G
    hg = SSD_H // SSD_G
    lane = lax.broadcasted_iota(jnp.int32, (q, 2 * SSD_P), 1)
    ys = []
    for g in range(SSD_G):
        gc = slice(g * gw, (g + 1) * gw)
        bg = bm[:, g * SSD_N:(g + 1) * SSD_N]
        cg = cm[:, g * SSD_N:(g + 1) * SSD_N]
        cbm = _bdot_nt(cg, bg)
        xg = xs[:, gc]
        s_old = s_ref[:, gc]
        y_g = _bdot(cg, s_old) * eac_x[:, gc]
        s_ref[:, gc] = s_old * cdec_x[:, gc] + _bdot(bg.T, xg * dte_x[:, gc])
        pairs = []
        for hp in range(hg // 2):
            wts = []
            for hh in range(2):
                h = g * hg + 2 * hp + hh
                seg = acum_c[:, h:h + 1] - acum_r[h:h + 1, :]
                dec = jnp.exp(jnp.where(causal, seg, -jnp.inf))
                wts.append((cbm * dec * dtr[h:h + 1, :]).astype(BF16))
            xp = xg[:, hp * 2 * SSD_P:(hp + 1) * 2 * SSD_P]
            xbd = jnp.concatenate([jnp.where(lane < SSD_P, xp, 0.0), jnp.where(lane >= SSD_P, xp, 0.0)],
                                  axis=0).astype(BF16)
            pairs.append(jnp.dot(jnp.concatenate(wts, axis=1), xbd, preferred_element_type=F32))
        y_g = y_g + jnp.concatenate(pairs, axis=1) + xg * dexp_ref[:, gc]
        zg = z_ref[0, :, gc]
        y_g = y_g * _silu(zg)
        y_g = y_g * lax.rsqrt(jnp.mean(y_g * y_g, axis=-1, keepdims=True) + NORM_EPS)
        ys.append(y_g)
    y_ref[0] = (jnp.concatenate(ys, axis=1) * ng_ref[...]).astype(y_ref.dtype)

    @pl.when(c == n_chunks - 1)
    def _():
        hl_ref[0] = s_ref[...]


def _ssd_scan(z, xbc, dt, dtt, cpre, conv_w, conv_b, dtb_row, dtb_col, alog_row, alog_col, d_exp, ng,
              expand, h0t, *, valid_len):
    bsz, seq, _ = z.shape
    q = SSD_CHUNK
    n_chunks = seq // q
    has_h0 = h0t is not None
    kern = functools.partial(_ssd_kernel, q=q, n_chunks=n_chunks, valid_len=valid_len, has_h0=has_h0)
    full = lambda *shape: pl.BlockSpec(shape, lambda b, c: (0,) * len(shape))
    in_specs = [
        pl.BlockSpec((1, q, SSD_DI), lambda b, c: (b, c, 0)),
        pl.BlockSpec((1, q, SSD_CONV_DIM), lambda b, c: (b, c, 0)),
        pl.BlockSpec((1, q, LANES), lambda b, c: (b, c, 0)),
        pl.BlockSpec((1, SSD_H, q), lambda b, c: (b, 0, c)),
        pl.BlockSpec((1, CONV_PAD, SSD_CONV_DIM), lambda b, c: (b, 0, 0)),
        full(SSD_CONV, SSD_CONV_DIM), full(1, SSD_CONV_DIM),
        full(1, LANES), full(SSD_H, 1), full(1, LANES), full(SSD_H, 1),
        full(1, SSD_DI), full(1, SSD_DI), full(LANES, SSD_DI),
    ]
    args = [z, xbc, dt, dtt, cpre, conv_w, conv_b, dtb_row, dtb_col, alog_row, alog_col, d_exp, ng, expand]
    if has_h0:
        in_specs.append(pl.BlockSpec((1, SSD_N, SSD_DI), lambda b, c: (b, 0, 0)))
        args.append(h0t)
    return pl.pallas_call(
        kern,
        grid=(bsz, n_chunks),
        in_specs=in_specs,
        out_specs=[pl.BlockSpec((1, q, SSD_DI), lambda b, c: (b, c, 0)),
                   pl.BlockSpec((1, SSD_N, SSD_DI), lambda b, c: (b, 0, 0))],
        out_shape=[jax.ShapeDtypeStruct((bsz, seq, SSD_DI), BF16),
                   jax.ShapeDtypeStruct((bsz, SSD_N, SSD_DI), F32)],
        scratch_shapes=[pltpu.VMEM((SSD_N, SSD_DI), F32), pltpu.VMEM((CONV_PAD + q, SSD_CONV_DIM), F32)],
        compiler_params=_params("arbitrary", "arbitrary"),
        name="ssd_scan",
    )(*args)


def _first_max(vals):
    m = vals[0]
    for v in vals[1:]:
        m = jnp.maximum(m, v)
    flags = []
    taken = jnp.zeros_like(m)
    for v in vals:
        f = jnp.where(v == m, 1.0 - taken, 0.0)
        taken = taken + f
        flags.append(f)
    return m, flags


def _sum_rows(rows):
    tot = rows[0]
    for r in rows[1:]:
        tot = tot + r
    return tot


def _router_rows(x, wh_ref, wl_ref, b_ref):
    xh = x.astype(BF16)
    xl = (x - xh.astype(F32)).astype(BF16)
    logits = (jnp.dot(xh, wh_ref[...], preferred_element_type=F32)
              + jnp.dot(xh, wl_ref[...], preferred_element_type=F32)
              + jnp.dot(xl, wh_ref[...], preferred_element_type=F32))
    lt = logits.T
    rows = [lt[e:e + 1, :] for e in range(N_EXPERTS)]
    mx = rows[0]
    for r in rows[1:]:
        mx = jnp.maximum(mx, r)
    ex = [jnp.exp(r - mx) for r in rows]
    den = _sum_rows(ex)
    probs = [r / den for r in ex]
    sel = [probs[e] + b_ref[e:e + 1, :] for e in range(N_EXPERTS)]
    scores, chosen = [], []
    for g in range(N_EGROUPS):
        v = sel[g * EXP_PER_GROUP:(g + 1) * EXP_PER_GROUP]
        m1, f1 = _first_max(v)
        rest = [jnp.where(f > 0.0, -jnp.inf, vi) for f, vi in zip(f1, v)]
        m2, f2 = _first_max(rest)
        scores.append(m1 + m2)
        chosen.extend([a + b for a, b in zip(f1, f2)])
    return probs, chosen, scores


def _router_kernel(x_ref, wh_ref, wl_ref, b_ref, o_ref):
    probs, chosen, scores = _router_rows(x_ref[...], wh_ref, wl_ref, b_ref)
    _, gflag = _first_max(scores)
    gates = [probs[e] * chosen[e] * gflag[e // EXP_PER_GROUP] for e in range(N_EXPERTS)]
    tot = _sum_rows(gates)
    for e in range(N_EXPERTS):
        o_ref[e:e + 1, :] = gates[e] / tot


def _route_group_kernel(x_ref, wh_ref, wl_ref, b_ref, grp_ref, gate_ref, xbf_ref, *, tm):
    xbf_ref[...] = x_ref[...].astype(BF16)
    probs, chosen, scores = _router_rows(x_ref[...], wh_ref, wl_ref, b_ref)
    _, gflag = _first_max(scores)
    grp = _sum_rows([gflag[g] * float(g) for g in range(1, N_EGROUPS)])
    grp_ref[...] = grp.astype(jnp.int32)
    local = [_sum_rows([probs[g * EXP_PER_GROUP + le] * chosen[g * EXP_PER_GROUP + le] * gflag[g]
                        for g in range(N_EGROUPS)]) for le in range(EXP_PER_GROUP)]
    tot = _sum_rows(local)
    rows = jnp.concatenate([r / tot for r in local] + [jnp.zeros((LANES - EXP_PER_GROUP, tm), F32)], axis=0)
    gate_ref[...] = rows.T


def _route_group(x, wr_hi, wr_lo, b_col, *, tm):
    t, d = x.shape
    return pl.pallas_call(
        functools.partial(_route_group_kernel, tm=tm),
        grid=(t // tm,),
        in_specs=[pl.BlockSpec((tm, d), lambda i: (i, 0)),
                  pl.BlockSpec((d, LANES), lambda i: (0, 0)),
                  pl.BlockSpec((d, LANES), lambda i: (0, 0)),
                  pl.BlockSpec((N_EXPERTS, 1), lambda i: (0, 0))],
        out_specs=[pl.BlockSpec((1, tm), lambda i: (0, i)), pl.BlockSpec((tm, LANES), lambda i: (i, 0)),
                   pl.BlockSpec((tm, d), lambda i: (i, 0))],
        out_shape=[jax.ShapeDtypeStruct((1, t), jnp.int32), jax.ShapeDtypeStruct((t, LANES), F32),
                   jax.ShapeDtypeStruct((t, d), BF16)],
        compiler_params=_params("arbitrary"),
        name="route_group",
    )(x, wr_hi, wr_lo, b_col)


def _moe_sorted_kernel(sw_ref, sslot_ref, se_ref, ns_ref, x_ref, pos_ref, gate_ref, wg_ref, wu_ref, wd_ref, o_ref,
                       g3_ref, xt_ref, gcol_ref, acc_ref, *, tr, steps):
    sb = pl.program_id(0)
    s = pl.program_id(1)
    valid = s < ns_ref[sb]
    slot = sslot_ref[sb * steps + s]
    e = se_ref[sb * steps + s]

    @pl.when(s == 0)
    def _():
        hi, mid, lo = _split3(gate_ref[...])
        g3_ref[0] = hi
        g3_ref[1] = mid
        g3_ref[2] = lo
        o_ref[...] = jnp.zeros(o_ref.shape, o_ref.dtype)

    @pl.when(jnp.logical_and(valid, e == 0))
    def _():
        n_src = x_ref.shape[0]
        row = slot * tr + lax.broadcasted_iota(jnp.int32, (tr, n_src), 0)
        pick = jnp.where(pos_ref[0] == row, 1.0, 0.0).astype(BF16)
        xt_ref[slot] = jnp.dot(pick, x_ref[...], preferred_element_type=F32).astype(BF16)
        gcol_ref[slot] = (jnp.dot(pick, g3_ref[0], preferred_element_type=F32)
                          + jnp.dot(pick, g3_ref[1], preferred_element_type=F32)
                          + jnp.dot(pick, g3_ref[2], preferred_element_type=F32))
        acc_ref[slot] = jnp.zeros(acc_ref.shape[1:], F32)

    @pl.when(valid)
    def _():
        xb = xt_ref[slot]
        gates = gcol_ref[slot]
        lane = lax.broadcasted_iota(jnp.int32, gates.shape, 1)
        gcol = jnp.sum(jnp.where(lane == e, gates, 0.0), axis=1, keepdims=True)
        hgate = jnp.dot(xb, wg_ref[0], preferred_element_type=F32)
        hup = jnp.dot(xb, wu_ref[0], preferred_element_type=F32)
        acc_ref[slot] += _bdot(_silu(hgate) * hup * gcol, wd_ref[0])

    @pl.when(jnp.logical_and(valid, e == EXP_PER_GROUP - 1))
    def _():
        o_ref[pl.ds(pl.multiple_of(slot * tr, tr), tr), :] = acc_ref[slot].astype(o_ref.dtype)


def _moe_sorted(x_bf, pos_row, gate4, plan, w_gate, w_up, w_down, *, sb_rows, tr):
    t, d = x_bf.shape
    f = w_gate.shape[2]
    n_sb = t // sb_rows
    cap = sb_rows + N_EGROUPS * tr
    slots = cap // tr
    steps = slots * EXP_PER_GROUP
    w_index = lambda sb, s, sw, sslot, se, ns: (sw[sb * steps + s], 0, 0)
    per_sb = lambda *shape: pl.BlockSpec(shape, lambda sb, s, sw, sslot, se, ns: (sb,) + (0,) * (len(shape) - 1))
    grid_spec = pltpu.PrefetchScalarGridSpec(
        num_scalar_prefetch=4,
        grid=(n_sb, steps),
        in_specs=[per_sb(sb_rows, d), per_sb(1, 1, sb_rows), per_sb(sb_rows, LANES),
                  pl.BlockSpec((1, d, f), w_index), pl.BlockSpec((1, d, f), w_index),
                  pl.BlockSpec((1, f, d), w_index)],
        out_specs=per_sb(cap, d),
        scratch_shapes=[pltpu.VMEM((3, sb_rows, LANES), BF16), pltpu.VMEM((slots, tr, d), BF16),
                        pltpu.VMEM((slots, tr, LANES), F32), pltpu.VMEM((slots, tr, d), F32)],
    )
    return pl.pallas_call(
        functools.partial(_moe_sorted_kernel, tr=tr, steps=steps),
        grid_spec=grid_spec,
        out_shape=jax.ShapeDtypeStruct((n_sb * cap, d), BF16),
        compiler_params=_params("arbitrary", "arbitrary"),
        name="moe_sorted_experts",
    )(*plan, x_bf, pos_row, gate4, w_gate, w_up, w_down)


def _dispatch_plan(grp, sb_rows, tr):
    n_sb = grp.shape[0] // sb_rows
    slots = (sb_rows + N_EGROUPS * tr) // tr
    steps = slots * EXP_PER_GROUP
    i32 = jnp.int32
    oh = (grp.reshape(n_sb, sb_rows, 1) == jnp.arange(N_EGROUPS, dtype=i32)).astype(i32)
    csum = jnp.cumsum(oh, axis=1)
    padded = ((csum[:, -1] + tr - 1) // tr) * tr
    gend = jnp.cumsum(padded, axis=1)
    pos = jnp.sum(oh * ((gend - padded)[:, None, :] + csum - 1), axis=2)
    slot_row0 = jnp.arange(slots, dtype=i32) * tr
    tile_group = jnp.minimum(jnp.sum((slot_row0[None, :, None] >= gend[:, None, :]).astype(i32), axis=2),
                             N_EGROUPS - 1)
    n_used = gend[:, -1] // tr
    pair_slot = jnp.repeat(jnp.arange(slots, dtype=i32), EXP_PER_GROUP)[None, :]
    pair_e = jnp.tile(jnp.arange(EXP_PER_GROUP, dtype=i32), slots)[None, :]
    pair_w = (jnp.take_along_axis(tile_group, jnp.broadcast_to(pair_slot, (n_sb, steps)), axis=1) * EXP_PER_GROUP
              + pair_e)
    key = jnp.where(pair_slot < n_used[:, None], pair_w * slots + pair_slot, N_EXPERTS * slots + pair_slot)
    order = jnp.argsort(key, axis=1).astype(i32)
    n_steps = n_used * EXP_PER_GROUP
    step_w = jnp.take_along_axis(pair_w, order, axis=1)
    last_w = jnp.take_along_axis(step_w, jnp.maximum(n_steps - 1, 0)[:, None], axis=1)
    step_w = jnp.where(jnp.arange(steps, dtype=i32)[None, :] < n_steps[:, None], step_w, last_w)
    plan = (step_w.reshape(-1).astype(i32), (order // EXP_PER_GROUP).reshape(-1), (order % EXP_PER_GROUP).reshape(-1),
            n_steps.astype(i32))
    return pos.astype(i32), plan


def _router(x, wr_hi, wr_lo, b_col, *, tm):
    t, d = x.shape
    return pl.pallas_call(
        _router_kernel,
        grid=(t // tm,),
        in_specs=[pl.BlockSpec((tm, d), lambda i: (i, 0)),
                  pl.BlockSpec((d, LANES), lambda i: (0, 0)),
                  pl.BlockSpec((d, LANES), lambda i: (0, 0)),
                  pl.BlockSpec((N_EXPERTS, 1), lambda i: (0, 0))],
        out_specs=pl.BlockSpec((N_EXPERTS, tm), lambda i: (0, i)),
        out_shape=jax.ShapeDtypeStruct((N_EXPERTS, t), F32),
        compiler_params=_params("arbitrary"),
        name="router",
    )(x, wr_hi, wr_lo, b_col)


def _moe_kernel(x_ref, gate_ref, wg_ref, wu_ref, wd_ref, o_ref, xbf_ref):
    e = pl.program_id(1)

    @pl.when(e == 0)
    def _():
        xbf_ref[...] = x_ref[...].astype(BF16)
        o_ref[...] = jnp.zeros(o_ref.shape, F32)

    xb = xbf_ref[...]
    gate = gate_ref[...]
    lane = lax.broadcasted_iota(jnp.int32, gate.shape, 1)
    gcol = jnp.sum(jnp.where(lane == e, gate, 0.0), axis=1, keepdims=True)
    hgate = jnp.dot(xb, wg_ref[0].astype(BF16), preferred_element_type=F32)
    hup = jnp.dot(xb, wu_ref[0].astype(BF16), preferred_element_type=F32)
    o_ref[...] += _bdot(_silu(hgate) * hup * gcol, wd_ref[0])


def _moe(x, gate, w_gate, w_up, w_down, *, tm):
    t, d = x.shape
    f = w_gate.shape[2]
    return pl.pallas_call(
        _moe_kernel,
        grid=(t // tm, N_EXPERTS),
        in_specs=[pl.BlockSpec((tm, d), lambda i, e: (i, 0)),
                  pl.BlockSpec((tm, N_EXPERTS), lambda i, e: (i, 0)),
                  pl.BlockSpec((1, d, f), lambda i, e: (e, 0, 0)),
                  pl.BlockSpec((1, d, f), lambda i, e: (e, 0, 0)),
                  pl.BlockSpec((1, f, d), lambda i, e: (e, 0, 0))],
        out_specs=pl.BlockSpec((tm, d), lambda i, e: (i, 0)),
        out_shape=jax.ShapeDtypeStruct((t, d), F32),
        scratch_shapes=[pltpu.VMEM((tm, d), BF16)],
        compiler_params=_params("arbitrary", "arbitrary"),
        name="moe_experts",
    )(x, gate, w_gate, w_up, w_down)


def _ln_ple_body(x1, ffn, p_ref, g_ref, b_ref, wg_ref, bg_ref, wp_ref, o_ref):
    x = _ln(DN_ALPHA * x1 + ffn, g_ref[...], b_ref[...])
    gate = jax.nn.sigmoid(_bdot(x, wg_ref[...]) + bg_ref[...])
    o_ref[...] = x + gate * _bdot(p_ref[...], wp_ref[...])


def _ln_ple_kernel(x_ref, f_ref, p_ref, *rest):
    _ln_ple_body(x_ref[...], f_ref[...], p_ref, *rest)


def _unsort_ln_ple_kernel(y_ref, pos_ref, x_ref, p_ref, *rest):
    tm, cap = x_ref.shape[0], y_ref.shape[0]
    col = lax.broadcasted_iota(jnp.int32, (tm, cap), 1)
    pick = jnp.where(pos_ref[...] == col, 1.0, 0.0).astype(BF16)
    ffn = jnp.dot(pick, y_ref[...], preferred_element_type=F32)
    _ln_ple_body(x_ref[...], ffn, p_ref, *rest)


def _unsort_ln_ple(y_sorted, pos_col, x, p, g, b, w_gate, b_gate, w_p, *, sb_rows, tm):
    t, d = x.shape
    pd = p.shape[1]
    n_sb = t // sb_rows
    n_sub = sb_rows // tm
    cap = y_sorted.shape[0] // n_sb
    rows = lambda cols: pl.BlockSpec((tm, cols), lambda sb, i: (sb * n_sub + i, 0))
    fixed = lambda r, c: pl.BlockSpec((r, c), lambda sb, i: (0, 0))
    return pl.pallas_call(
        _unsort_ln_ple_kernel,
        grid=(n_sb, n_sub),
        in_specs=[pl.BlockSpec((cap, d), lambda sb, i: (sb, 0)), rows(1), rows(d), rows(pd),
                  fixed(1, d), fixed(1, d), fixed(d, d), fixed(1, d), fixed(pd, d)],
        out_specs=rows(d),
        out_shape=jax.ShapeDtypeStruct((t, d), F32),
        compiler_params=_params("arbitrary", "arbitrary"),
        name="unsort_ln_ple",
    )(y_sorted, pos_col, x, p, g, b, w_gate, b_gate, w_p)


def _ln_ple(x, ffn, p, g, b, w_gate, b_gate, w_p, *, tm):
    t, d = x.shape
    pd = p.shape[1]
    rows = lambda cols: pl.BlockSpec((tm, cols), lambda i: (i, 0))
    fixed = lambda r, c: pl.BlockSpec((r, c), lambda i: (0, 0))
    return pl.pallas_call(
        _ln_ple_kernel,
        grid=(t // tm,),
        in_specs=[rows(d), rows(d), rows(pd), fixed(1, d), fixed(1, d), fixed(d, d), fixed(1, d), fixed(pd, d)],
        out_specs=rows(d),
        out_shape=jax.ShapeDtypeStruct((t, d), F32),
        compiler_params=_params("arbitrary"),
        name="ln_ple",
    )(x, ffn, p, g, b, w_gate, b_gate, w_p)


def _row_tile(rows, want):
    tm = min(rows, want)
    assert rows % tm == 0
    return tm


POOL_TM = 512
MM_TM = 512
QKV_TM = 512
ATTN_TQ = 512
MOE_TM = 1024
MOE_SORT_BLOCK = 2048
MOE_SORT_TR = 256
ROUTER_TM = 1024
PLE_TM = 512
DECODE_PAGES_PER_STEP = 8


def kernel(x_prompt, x_sample, p_prompt, p_sample, state_pool, cache_k, cache_v, page_table, state_ssm, state_conv,
           ln_g, ln_b, w_pool_in, w_pool_grp, pool_scale, w_pool_out, w_diff_in, diff_lambda_q, diff_lambda_k,
           diff_norm_g, w_diff_out, w_ssd_in, ssd_conv_w, ssd_conv_b, ssd_dt_bias, ssd_a_log, ssd_d, ssd_norm_g,
           w_ssd_out, w_router, b_router, w_exp_gate, w_exp_up, w_exp_down, w_ple, w_ple_gate, b_ple_gate):
    bp, seq, d = x_prompt.shape
    bs, tdec, _ = x_sample.shape
    n_pages = page_table.shape[1]
    past = n_pages * cache_k.shape[2]
    tp = bp * seq
    ts = bs * tdec

    def to_tm(a):
        return jnp.swapaxes(a, 0, 1).reshape((ts,) + a.shape[2:])

    def from_tm(a):
        return jnp.swapaxes(a.reshape((tdec, bs) + a.shape[1:]), 0, 1)

    row = lambda v: v.reshape(1, -1)

    wr = jnp.pad(w_router, ((0, 0), (0, LANES - N_EXPERTS)))
    wr_hi = wr.astype(BF16)
    wr_lo = (wr - wr_hi.astype(F32)).astype(BF16)
    br_col = b_router.reshape(N_EXPERTS, 1)

    def finish(x1, i, p, sparse):
        t = x1.shape[0]
        tail = (row(ln_g[i, 1]), row(ln_b[i, 1]), w_ple_gate[i].astype(BF16), row(b_ple_gate[i]),
                w_ple[i].astype(BF16))
        if sparse:
            sb_rows, tr = _row_tile(t, MOE_SORT_BLOCK), MOE_SORT_TR
            grp, gate4, x1_bf = _route_group(x1, wr_hi, wr_lo, br_col, tm=_row_tile(t, ROUTER_TM))
            pos, plan = _dispatch_plan(grp[0], sb_rows, tr)
            y_sorted = _moe_sorted(x1_bf, pos.reshape(-1, 1, sb_rows), gate4, plan,
                                   w_exp_gate[i].astype(BF16), w_exp_up[i].astype(BF16),
                                   w_exp_down[i].astype(BF16), sb_rows=sb_rows, tr=tr)
            return _unsort_ln_ple(y_sorted, pos.reshape(t, 1), x1, p, *tail, sb_rows=sb_rows,
                                  tm=_row_tile(sb_rows, PLE_TM))
        gate = _router(x1, wr_hi, wr_lo, br_col, tm=_row_tile(t, ROUTER_TM)).T
        ffn = _moe(x1, gate, w_exp_gate[i], w_exp_up[i], w_exp_down[i], tm=_row_tile(t, MOE_TM))
        return _ln_ple(x1, ffn, p, *tail, tm=_row_tile(t, PLE_TM))

    xp = x_prompt.reshape(tp, d)
    xs = to_tm(x_sample)
    pos_p = jnp.arange(seq, dtype=jnp.int32)
    pos_s = jnp.repeat(past + jnp.arange(tdec, dtype=jnp.int32), bs)
    outs = {k: [] for k in ("pool_p", "pool_s", "k_p", "v_p", "k_s", "v_s", "ssm_p", "conv_p", "ssm_s", "conv_s")}

    for i in range(DEPTH):
        kind, j = i % N_MIXERS, i // N_MIXERS
        g1, b1 = row(ln_g[i, 0]), row(ln_b[i, 0])
        if kind == 0:
            wts = (w_pool_in[j].astype(BF16), w_pool_grp[j].astype(BF16), row(pool_scale[j]),
                   w_pool_out[j].astype(BF16), g1, b1)
            x1p, newp = _pool_layer(xp.reshape(bp, seq, d), jnp.zeros((bp, POOL_BUF + 1, d), F32), *wts,
                                    tm=_row_tile(seq, POOL_TM), stride=1, pos0=0)
            pre_s = jnp.swapaxes(state_pool[j], 0, 1).reshape(1, POOL_BUF * bs, d)
            pre_s = jnp.pad(pre_s, ((0, 0), (bs, 0), (0, 0)))
            x1s, news = _pool_layer(xs.reshape(1, ts, d), pre_s, *wts, tm=ts, stride=bs, pos0=past)
            outs["pool_p"].append(newp)
            outs["pool_s"].append(jnp.swapaxes(news.reshape(POOL_BUF, bs, d), 0, 1))
            x1p = x1p.reshape(tp, d)
            x1s = x1s.reshape(ts, d)
        elif kind == 1:
            lam_init = 0.8 - 0.6 * math.exp(-0.3 * i)
            w_in = w_diff_in[j].astype(BF16)
            w_out = w_diff_out[j].astype(BF16)
            lq, lk, ng = diff_lambda_q[j], diff_lambda_k[j], row(diff_norm_g[j])
            cos_p, sin_p = _rope_tables(pos_p)
            q, k, v = _qkv_rope(xp.reshape(bp, seq, d), w_in, cos_p, sin_p, tm=_row_tile(seq, QKV_TM))
            o = _diff_attn_prompt(q, k, v, lq, lk, ng, lam_init, tq=_row_tile(seq, ATTN_TQ))
            x1p = _mm_res_ln(o.reshape(tp, d), w_out, xp, g1, b1, tm=_row_tile(tp, MM_TM))
            outs["k_p"].append(k.reshape(bp, seq, DIFF_HEADS, 2, DIFF_DH))
            outs["v_p"].append(v.reshape(bp, seq, DIFF_HEADS, DIFF_VD))

            cos_s, sin_s = _rope_tables(pos_s)
            q, k, v = _qkv_rope(xs.reshape(1, ts, d), w_in, cos_s, sin_s, tm=ts)
            q, k, v = from_tm(q[0]), from_tm(k[0]), from_tm(v[0])
            q5 = q.reshape(bs, tdec, DIFF_HEADS * 2, DIFF_DH)
            eye = jnp.eye(DIFF_HEADS * 2, dtype=F32)
            qbd = jnp.einsum('bthd,hg->bhtgd', q5, eye)
            qbd = jnp.pad(qbd, ((0, 0), (0, 0), (0, DEC_TPAD - tdec), (0, 0), (0, 0)))
            qbd = qbd.reshape(bs, DEC_ROWS, d)
            kt_pages = jnp.transpose(cache_k, (0, 1, 3, 4, 5, 2)).reshape(-1, d, PAGE_SIZE)
            v_pages = cache_v.reshape(-1, PAGE_SIZE * DIFF_HEADS, DIFF_VD)
            knew = jnp.pad(jnp.swapaxes(k, 1, 2), ((0, 0), (0, 0), (0, PAGE_SIZE - tdec)))
            vnew = jnp.pad(v, ((0, 0), (0, PAGE_SIZE - tdec), (0, 0))).reshape(bs, PAGE_SIZE * DIFF_HEADS, DIFF_VD)
            o = _diff_attn_sample(qbd, kt_pages, v_pages, page_table, knew, vnew, lq, lk, ng, lam_init,
                                  pages_per_step=DECODE_PAGES_PER_STEP, page0=j * cache_k.shape[1])
            o = to_tm(o[:, :tdec])
            x1s = _mm_res_ln(o, w_out, xs, g1, b1, tm=ts)
            outs["k_s"].append(k.reshape(bs, tdec, DIFF_HEADS, 2, DIFF_DH))
            outs["v_s"].append(v.reshape(bs, tdec, DIFF_HEADS, DIFF_VD))
        else:
            w_in = w_ssd_in[j]
            w_z = w_in[:, :SSD_DI].astype(BF16)
            w_xbc = w_in[:, SSD_DI:SSD_DI + SSD_CONV_DIM].astype(BF16)
            w_dt = jnp.pad(w_in[:, SSD_DI + SSD_CONV_DIM:], ((0, 0), (0, LANES - SSD_H))).astype(BF16)
            w_out = w_ssd_out[j].astype(BF16)
            dtb_row = jnp.pad(ssd_dt_bias[j], (0, LANES - SSD_H)).reshape(1, LANES)
            alog_row = jnp.pad(ssd_a_log[j], (0, LANES - SSD_H)).reshape(1, LANES)
            dtb_col = ssd_dt_bias[j].reshape(SSD_H, 1)
            alog_col = ssd_a_log[j].reshape(SSD_H, 1)
            d_exp = jnp.repeat(ssd_d[j], SSD_P).reshape(1, SSD_DI)
            ng = row(ssd_norm_g[j])
            expand = (jnp.arange(SSD_DI)[None, :] // SSD_P == jnp.arange(LANES)[:, None]).astype(BF16)
            shared = (ssd_conv_w[j], row(ssd_conv_b[j]), dtb_row, dtb_col, alog_row, alog_col, d_exp, ng, expand)

            def in_proj(x, tm):
                z = _matmul(x, w_z, tm=tm, tn=1024)
                xbc = _matmul(x, w_xbc, tm=tm, tn=1024)
                dt = _matmul(x, w_dt, tm=tm, tn=LANES)
                return z, xbc, dt

            def hlast_to_state(hl, b):
                return jnp.transpose(hl.reshape(b, SSD_N, SSD_H, SSD_P), (0, 2, 3, 1))

            z, xbc, dt = in_proj(xp, _row_tile(tp, MM_TM))
            dtt = jnp.swapaxes(dt.reshape(bp, seq, LANES)[:, :, :SSD_H], 1, 2)
            y, hl = _ssd_scan(z.reshape(bp, seq, SSD_DI), xbc.reshape(bp, seq, SSD_CONV_DIM),
                              dt.reshape(bp, seq, LANES), dtt, jnp.zeros((bp, CONV_PAD, SSD_CONV_DIM), F32),
                              *shared, None, valid_len=None)
            x1p = _mm_res_ln(y.reshape(tp, SSD_DI), w_out, xp, g1, b1, tm=_row_tile(tp, MM_TM))
            outs["ssm_p"].append(hlast_to_state(hl, bp))
            outs["conv_p"].append(xbc.reshape(bp, seq, SSD_CONV_DIM)[:, seq - (SSD_CONV - 1):])

            z, xbc, dt = in_proj(xs, ts)
            z, xbc, dt = from_tm(z), from_tm(xbc), from_tm(dt)
            padt = ((0, 0), (0, SSD_CHUNK - tdec), (0, 0))
            dtp = jnp.pad(dt, padt)
            cpre = jnp.pad(state_conv[j], ((0, 0), (CONV_PAD - (SSD_CONV - 1), 0), (0, 0)))
            h0t = jnp.transpose(state_ssm[j], (0, 3, 1, 2)).reshape(bs, SSD_N, SSD_DI)
            y, hl = _ssd_scan(jnp.pad(z, padt), jnp.pad(xbc, padt), dtp, jnp.swapaxes(dtp[:, :, :SSD_H], 1, 2),
                              cpre, *shared, h0t, valid_len=tdec)
            x1s = _mm_res_ln(to_tm(y[:, :tdec]), w_out, xs, g1, b1, tm=ts)
            outs["ssm_s"].append(hlast_to_state(hl, bs))
            ext = jnp.concatenate([state_conv[j], xbc], axis=1)
            outs["conv_s"].append(ext[:, -(SSD_CONV - 1):])

        xp = finish(x1p, i, p_prompt[i].reshape(tp, PLE_DIM), sparse=True)
        xs = finish(x1s, i, to_tm(p_sample[i]), sparse=False)

    st = lambda name: jnp.stack(outs[name])
    return (xp.reshape(bp, seq, d), from_tm(xs), st("pool_p"), st("pool_s"), st("k_p"), st("v_p"), st("k_s"),
            st("v_s"), st("ssm_p"), st("conv_p"), st("ssm_s"), st("conv_s"))
```

```python
import functools
import math

import jax
import jax.numpy as jnp
from jax import lax
from jax.experimental import pallas as pl
from jax.experimental.pallas import tpu as pltpu

F32 = jnp.float32
BF16 = jnp.bfloat16

D_MODEL = 1024
DEPTH = 4
N_MIXERS = 3
PAGE_SIZE = 128

POOL_WINDOWS = (2, 4, 8, 16)
POOL_GD = D_MODEL // len(POOL_WINDOWS)
POOL_BUF = max(POOL_WINDOWS) - 1

DIFF_DH = 64
DIFF_VD = 2 * DIFF_DH
DIFF_HEADS = D_MODEL // DIFF_VD
ROPE_THETA = 10000.0

SSD_DI = 2 * D_MODEL
SSD_P = 64
SSD_H = SSD_DI // SSD_P
SSD_G = 4
SSD_N = 128
SSD_CONV = 4
SSD_CONV_DIM = SSD_DI + 2 * SSD_G * SSD_N
SSD_CHUNK = 128

N_EXPERTS = 16
N_EGROUPS = 4
EXP_PER_GROUP = N_EXPERTS // N_EGROUPS
D_FF_EXPERT = D_MODEL // 2

PLE_DIM = 256
DN_ALPHA = (2 * DEPTH) ** 0.25
NORM_EPS = 1e-5

LANES = 128
SUBLANES = 8
VMEM_LIMIT = 56 * 1024 * 1024


def _params(*sem):
    return pltpu.CompilerParams(dimension_semantics=sem, vmem_limit_bytes=VMEM_LIMIT)


def _bdot(a, b):
    return jnp.dot(a.astype(BF16), b.astype(BF16), preferred_element_type=F32)


def _bdot_nt(a, b):
    return lax.dot_general(a.astype(BF16), b.astype(BF16), (((1,), (1,)), ((), ())),
                           preferred_element_type=F32)


def _split3(x):
    hi = x.astype(BF16)
    r = x - hi.astype(F32)
    mid = r.astype(BF16)
    lo = (r - mid.astype(F32)).astype(BF16)
    return hi, mid, lo


def _dot_exact_rhs(x, m):
    hi, mid, lo = _split3(x)
    return (jnp.dot(hi, m, preferred_element_type=F32) + jnp.dot(mid, m, preferred_element_type=F32)
            + jnp.dot(lo, m, preferred_element_type=F32))


def _dot_exact_lhs(m, x):
    hi, mid, lo = _split3(x)
    return (jnp.dot(m, hi, preferred_element_type=F32) + jnp.dot(m, mid, preferred_element_type=F32)
            + jnp.dot(m, lo, preferred_element_type=F32))


def _ln(v, g, b):
    mu = jnp.mean(v, axis=-1, keepdims=True)
    c = v - mu
    var = jnp.mean(c * c, axis=-1, keepdims=True)
    return c * lax.rsqrt(var + NORM_EPS) * g + b


def _silu(v):
    return v * jax.nn.sigmoid(v)


def _softplus(v):
    return jnp.maximum(v, 0.0) + jnp.log(1.0 + jnp.exp(-jnp.abs(v)))


def _pool_kernel(x_ref, pre_ref, win_ref, wgrp_ref, scale_ref, wout_ref, g_ref, b_ref,
                 y_ref, newp_ref, ext_ref, *, tm, stride, pos0, n_tiles):
    i = pl.program_id(1)
    pre_rows = (POOL_BUF + 1) * stride

    @pl.when(i == 0)
    def _():
        ext_ref[0:pre_rows, :] = pre_ref[0]

    @pl.when(i > 0)
    def _():
        ext_ref[0:pre_rows, :] = ext_ref[tm:tm + pre_rows, :]

    x = x_ref[0]
    u = _bdot(x, win_ref[...])
    ext_ref[pre_rows:pre_rows + tm, :] = u
    row = lax.broadcasted_iota(jnp.int32, (tm, 1), 0)
    pos = pos0 + (i * tm + row) // stride
    mixed = []
    for gi, w in enumerate(POOL_WINDOWS):
        c0 = gi * POOL_GD
        ug = u[:, c0:c0 + POOL_GD]
        s = ug
        for k in range(1, w):
            r0 = pre_rows - k * stride
            s = s + ext_ref[r0:r0 + tm, c0:c0 + POOL_GD]
        cnt = jnp.minimum(w, pos + 1).astype(F32)
        pooled = s / cnt - ug
        mixed.append(_bdot(pooled, wgrp_ref[gi]))
    mixed = jnp.concatenate(mixed, axis=1) * scale_ref[...]
    mix = _bdot(mixed, wout_ref[...])
    y_ref[0] = _ln(DN_ALPHA * x + mix, g_ref[...], b_ref[...])

    @pl.when(i == n_tiles - 1)
    def _():
        newp_ref[0] = ext_ref[tm + stride:tm + pre_rows, :]


def _pool_layer(x, prefix, w_in, w_grp, scale, w_out, g, b, *, tm, stride, pos0):
    bsz, rows, d = x.shape
    n_tiles = rows // tm
    pre_rows = (POOL_BUF + 1) * stride
    kern = functools.partial(_pool_kernel, tm=tm, stride=stride, pos0=pos0, n_tiles=n_tiles)
    full = lambda *shape: pl.BlockSpec(shape, lambda bi, i: (0,) * len(shape))
    return pl.pallas_call(
        kern,
        grid=(bsz, n_tiles),
        in_specs=[
            pl.BlockSpec((1, tm, d), lambda bi, i: (bi, i, 0)),
            pl.BlockSpec((1, pre_rows, d), lambda bi, i: (bi, 0, 0)),
            full(d, d), full(len(POOL_WINDOWS), POOL_GD, POOL_GD), full(1, d), full(d, d),
            full(1, d), full(1, d),
        ],
        out_specs=[
            pl.BlockSpec((1, tm, d), lambda bi, i: (bi, i, 0)),
            pl.BlockSpec((1, POOL_BUF * stride, d), lambda bi, i: (bi, 0, 0)),
        ],
        out_shape=[jax.ShapeDtypeStruct((bsz, rows, d), F32),
                   jax.ShapeDtypeStruct((bsz, POOL_BUF * stride, d), F32)],
        scratch_shapes=[pltpu.VMEM((pre_rows + tm, d), F32)],
        compiler_params=_params("arbitrary", "arbitrary"),
        name="pool_layer",
    )(x, prefix, w_in, w_grp, scale, w_out, g, b)


def _mm_kernel(x_ref, w_ref, o_ref):
    o_ref[...] = _bdot(x_ref[...], w_ref[...]).astype(o_ref.dtype)


def _matmul(x, w, *, tm, tn, out_dtype=F32):
    t, k = x.shape
    n = w.shape[1]
    return pl.pallas_call(
        _mm_kernel,
        grid=(n // tn, t // tm),
        in_specs=[pl.BlockSpec((tm, k), lambda j, i: (i, 0)),
                  pl.BlockSpec((k, tn), lambda j, i: (0, j))],
        out_specs=pl.BlockSpec((tm, tn), lambda j, i: (i, j)),
        out_shape=jax.ShapeDtypeStruct((t, n), out_dtype),
        compiler_params=_params("arbitrary", "arbitrary"),
        name="matmul",
    )(x, w)


def _mm_res_ln_kernel(a_ref, w_ref, x_ref, g_ref, b_ref, o_ref):
    mix = _bdot(a_ref[...], w_ref[...])
    o_ref[...] = _ln(DN_ALPHA * x_ref[...] + mix, g_ref[...], b_ref[...])


def _mm_res_ln(a, w, x, g, b, *, tm):
    t, k = a.shape
    d = w.shape[1]
    return pl.pallas_call(
        _mm_res_ln_kernel,
        grid=(t // tm,),
        in_specs=[pl.BlockSpec((tm, k), lambda i: (i, 0)),
                  pl.BlockSpec((k, d), lambda i: (0, 0)),
                  pl.BlockSpec((tm, d), lambda i: (i, 0)),
                  pl.BlockSpec((1, d), lambda i: (0, 0)),
                  pl.BlockSpec((1, d), lambda i: (0, 0))],
        out_specs=pl.BlockSpec((tm, d), lambda i: (i, 0)),
        out_shape=jax.ShapeDtypeStruct((t, d), F32),
        compiler_params=_params("arbitrary"),
        name="mm_res_ln",
    )(a, w, x, g, b)


def _qkv_kernel(x_ref, w_ref, cos_ref, sin_ref, q_ref, k_ref, v_ref, *, tm):
    x = x_ref[0].astype(BF16)
    cos = cos_ref[...]
    sin = sin_ref[...]
    lane = lax.broadcasted_iota(jnp.int32, (tm, LANES), 1)
    first_half = (lane % DIFF_DH) < (DIFF_DH // 2)

    def rope(y):
        outs = []
        for c in range(D_MODEL // LANES):
            yc = y[:, c * LANES:(c + 1) * LANES]
            sw = jnp.where(first_half, pltpu.roll(yc, LANES - DIFF_DH // 2, 1),
                           pltpu.roll(yc, DIFF_DH // 2, 1))
            outs.append(yc * cos + sw * sin)
        return jnp.concatenate(outs, axis=1)

    q_ref[0] = rope(jnp.dot(x, w_ref[:, 0:D_MODEL], preferred_element_type=F32))
    k_ref[0] = rope(jnp.dot(x, w_ref[:, D_MODEL:2 * D_MODEL], preferred_element_type=F32))
    v_ref[0] = jnp.dot(x, w_ref[:, 2 * D_MODEL:3 * D_MODEL], preferred_element_type=F32)


def _rope_tables(pos):
    half = DIFF_DH // 2
    inv = ROPE_THETA ** (-jnp.arange(half, dtype=F32) * 2.0 / DIFF_DH)
    ang = pos.astype(F32)[:, None] * inv[None, :]
    cos, sin = jnp.cos(ang), jnp.sin(ang)
    reps = LANES // DIFF_DH
    cos_t = jnp.tile(jnp.concatenate([cos, cos], axis=1), (1, reps))
    sin_t = jnp.tile(jnp.concatenate([-sin, sin], axis=1), (1, reps))
    return cos_t, sin_t


def _qkv_rope(x, w, cos_t, sin_t, *, tm):
    bsz, rows, d = x.shape
    kern = functools.partial(_qkv_kernel, tm=tm)
    spec = pl.BlockSpec((1, tm, d), lambda bi, i: (bi, i, 0))
    tab = pl.BlockSpec((tm, LANES), lambda bi, i: (i, 0))
    shp = jax.ShapeDtypeStruct((bsz, rows, d), F32)
    return pl.pallas_call(
        kern,
        grid=(bsz, rows // tm),
        in_specs=[spec, pl.BlockSpec((d, 3 * d), lambda bi, i: (0, 0)), tab, tab],
        out_specs=[spec, spec, spec],
        out_shape=[shp, shp, shp],
        compiler_params=_params("arbitrary", "arbitrary"),
        name="qkv_rope",
    )(x, w, cos_t, sin_t)


def _diff_lambda(lq_ref, lk_ref, lam_init):
    dots = jnp.sum(lq_ref[...] * lk_ref[...], axis=-1, keepdims=True)
    e = jnp.exp(dots)
    return e[0:1, :] - e[1:2, :] + lam_init


def _head_norm(o, ng, lam_init):
    ms = jnp.mean(o * o, axis=-1, keepdims=True)
    return o * lax.rsqrt(ms + NORM_EPS) * ng * (1.0 - lam_init)


def _attn_kernel(q_ref, k_ref, v_ref, lq_ref, lk_ref, ng_ref, o_ref, *, tq, lam_init):
    qi = pl.program_id(2)
    q = q_ref[0] * (DIFF_DH ** -0.5)
    q1 = q[:, :DIFF_DH].astype(BF16)
    q2 = q[:, DIFF_DH:].astype(BF16)
    ones = jnp.ones((tq, DIFF_VD), BF16)

    def scores(kb):
        start = pl.multiple_of(kb * tq, tq)
        k = k_ref[0, pl.ds(start, tq), :]
        v1 = jnp.concatenate([v_ref[0, pl.ds(start, tq), :].astype(BF16), ones], axis=1)
        s = jnp.concatenate([_bdot_nt(q1, k[:, :DIFF_DH]), _bdot_nt(q2, k[:, DIFF_DH:])], axis=0)
        return s, v1

    def update(s, v1, m, a):
        mn = jnp.maximum(m, jnp.max(s, axis=1, keepdims=True))
        p = jnp.exp(s - mn).astype(BF16)
        a = jnp.exp(m - mn) * a + jnp.dot(p, v1, preferred_element_type=F32)
        return mn, a

    def body(kb, carry):
        s, v1 = scores(kb)
        return update(s, v1, *carry)

    init = (jnp.full((2 * tq, 1), -jnp.inf, F32), jnp.zeros((2 * tq, 2 * DIFF_VD), F32))
    m, a = lax.fori_loop(0, qi, body, init)
    s, v1 = scores(qi)
    row = lax.broadcasted_iota(jnp.int32, (2 * tq, tq), 0) % tq
    col = lax.broadcasted_iota(jnp.int32, (2 * tq, tq), 1)
    m, a = update(jnp.where(col <= row, s, -jnp.inf), v1, m, a)
    lam = _diff_lambda(lq_ref, lk_ref, lam_init)
    o = (a[:tq, :DIFF_VD] / a[:tq, DIFF_VD:DIFF_VD + 1]
         - lam * (a[tq:, :DIFF_VD] / a[tq:, DIFF_VD:DIFF_VD + 1]))
    o_ref[0] = _head_norm(o, ng_ref[...], lam_init).astype(o_ref.dtype)


def _diff_attn_prompt(q, k, v, lq, lk, ng, lam_init, *, tq):
    bsz, seq, d = q.shape
    kern = functools.partial(_attn_kernel, tq=tq, lam_init=lam_init)
    small = lambda *shape: pl.BlockSpec(shape, lambda bi, h, i: (0,) * len(shape))
    return pl.pallas_call(
        kern,
        grid=(bsz, DIFF_HEADS, seq // tq),
        in_specs=[pl.BlockSpec((1, tq, DIFF_VD), lambda bi, h, i: (bi, i, h)),
                  pl.BlockSpec((1, seq, DIFF_VD), lambda bi, h, i: (bi, 0, h)),
                  pl.BlockSpec((1, seq, DIFF_VD), lambda bi, h, i: (bi, 0, h)),
                  small(2, DIFF_DH), small(2, DIFF_DH), small(1, DIFF_VD)],
        out_specs=pl.BlockSpec((1, tq, DIFF_VD), lambda bi, h, i: (bi, i, h)),
        out_shape=jax.ShapeDtypeStruct((bsz, seq, d), BF16),
        compiler_params=_params("arbitrary", "arbitrary", "arbitrary"),
        name="diff_attn_prompt",
    )(q, k, v, lq, lk, ng)


DEC_TPAD = SUBLANES // 2
DEC_ROWS = DIFF_HEADS * 2 * DEC_TPAD


def _decode_kernel(pt_ref, q_ref, *refs, pages_per_step, n_steps, lam_init):
    g = pages_per_step
    k_refs = refs[:g]
    v_refs = refs[g:2 * g]
    knew_ref, vnew_ref, rep_ref, lq_ref, lk_ref, ng_ref, o_ref, m_ref, l_ref, acc_ref = refs[2 * g:]
    s = pl.program_id(1)
    scale = DIFF_DH ** -0.5
    rep = rep_ref[...]
    row_head = lax.broadcasted_iota(jnp.int32, (DEC_ROWS, PAGE_SIZE * DIFF_HEADS), 0) // (2 * DEC_TPAD)
    col_head = lax.broadcasted_iota(jnp.int32, (DEC_ROWS, PAGE_SIZE * DIFF_HEADS), 1) % DIFF_HEADS
    own_head = row_head == col_head

    @pl.when(s == 0)
    def _():
        m_ref[...] = jnp.full(m_ref.shape, -jnp.inf, F32)
        l_ref[...] = jnp.zeros(l_ref.shape, F32)
        acc_ref[...] = jnp.zeros(acc_ref.shape, F32)

    q = (q_ref[0] * scale).astype(BF16)

    def block(ks, vs, mask):
        sc = jnp.concatenate([_bdot(q, k[0]) for k in ks], axis=1)
        if mask is not None:
            sc = jnp.where(mask, sc, -jnp.inf)
        m_old = m_ref[...]
        mn = jnp.maximum(m_old, jnp.max(sc, axis=1, keepdims=True))
        alpha = jnp.exp(m_old - mn)
        p = jnp.exp(sc - mn)
        pv = None
        for gi in range(len(vs)):
            pe = jnp.dot(p[:, gi * PAGE_SIZE:(gi + 1) * PAGE_SIZE].astype(BF16), rep, preferred_element_type=F32)
            part = _bdot(jnp.where(own_head, pe, 0.0), vs[gi][0])
            pv = part if pv is None else pv + part
        l_ref[...] = alpha * l_ref[...] + jnp.sum(p, axis=1, keepdims=True)
        acc_ref[...] = alpha * acc_ref[...] + pv
        m_ref[...] = mn

    block(k_refs, v_refs, None)

    @pl.when(s == n_steps - 1)
    def _():
        t_row = lax.broadcasted_iota(jnp.int32, (DEC_ROWS, PAGE_SIZE), 0) % DEC_TPAD
        key = lax.broadcasted_iota(jnp.int32, (DEC_ROWS, PAGE_SIZE), 1)
        block([knew_ref], [vnew_ref], key <= t_row)
        lam = _diff_lambda(lq_ref, lk_ref, lam_init)
        ng = ng_ref[...]
        for h in range(DIFF_HEADS):
            r0 = h * SUBLANES
            cols = slice(h * DIFF_VD, (h + 1) * DIFF_VD)
            a = acc_ref[r0:r0 + SUBLANES, :] / l_ref[r0:r0 + SUBLANES, :]
            o = a - lam * pltpu.roll(a, DEC_TPAD, 0)
            o_ref[0, :, cols] = _head_norm(o, ng, lam_init).astype(o_ref.dtype)


def _diff_attn_sample(qbd, cache_k, cache_v, page_table, knew, vnew, lq, lk, ng, lam_init,
                      *, pages_per_step, page0):
    bsz = qbd.shape[0]
    n_pages = page_table.shape[1]
    g = pages_per_step
    n_steps = n_pages // g
    kern = functools.partial(_decode_kernel, pages_per_step=g, n_steps=n_steps, lam_init=lam_init)
    rep = (jnp.arange(PAGE_SIZE * DIFF_HEADS)[None, :] // DIFF_HEADS == jnp.arange(PAGE_SIZE)[:, None]).astype(BF16)

    def page_spec(gi):
        return pl.BlockSpec((1, D_MODEL, PAGE_SIZE), lambda b, s, pt: (page0 + pt[b, s * g + gi], 0, 0))

    per_b = lambda rows, cols: pl.BlockSpec((1, rows, cols), lambda b, s, pt: (b, 0, 0))
    small = lambda *shape: pl.BlockSpec(shape, lambda b, s, pt: (0,) * len(shape))
    grid_spec = pltpu.PrefetchScalarGridSpec(
        num_scalar_prefetch=1,
        grid=(bsz, n_steps),
        in_specs=([per_b(DEC_ROWS, D_MODEL)] + [page_spec(gi) for gi in range(g)] + [page_spec(gi) for gi in range(g)]
                  + [per_b(D_MODEL, PAGE_SIZE), per_b(D_MODEL, PAGE_SIZE), small(PAGE_SIZE, PAGE_SIZE * DIFF_HEADS),
                     small(2, DIFF_DH), small(2, DIFF_DH), small(1, DIFF_VD)]),
        out_specs=per_b(SUBLANES, D_MODEL),
        scratch_shapes=[pltpu.VMEM((DEC_ROWS, 1), F32), pltpu.VMEM((DEC_ROWS, 1), F32),
                        pltpu.VMEM((DEC_ROWS, DIFF_VD), F32)],
    )
    return pl.pallas_call(
        kern,
        grid_spec=grid_spec,
        out_shape=jax.ShapeDtypeStruct((bsz, SUBLANES, D_MODEL), BF16),
        compiler_params=_params("arbitrary", "arbitrary"),
        name="diff_attn_sample",
    )(page_table, qbd, *([cache_k] * g), *([cache_v] * g), knew, vnew, rep, lq, lk, ng)


CONV_PAD = SUBLANES


def _ssd_kernel(*refs, q, n_chunks, valid_len, has_h0):
    if has_h0:
        (z_ref, xbc_ref, dt_ref, dtt_ref, cpre_ref, cw_ref, cb_ref, dtb_row_ref, dtb_col_ref,
         alog_row_ref, alog_col_ref, dexp_ref, ng_ref, expand_ref, h0_ref,
         y_ref, hl_ref, s_ref, cext_ref) = refs
    else:
        (z_ref, xbc_ref, dt_ref, dtt_ref, cpre_ref, cw_ref, cb_ref, dtb_row_ref, dtb_col_ref,
         alog_row_ref, alog_col_ref, dexp_ref, ng_ref, expand_ref,
         y_ref, hl_ref, s_ref, cext_ref) = refs
        h0_ref = None
    c = pl.program_id(1)

    @pl.when(c == 0)
    def _():
        if has_h0:
            s_ref[...] = h0_ref[0]
        else:
            s_ref[...] = jnp.zeros(s_ref.shape, F32)
        cext_ref[0:CONV_PAD, :] = cpre_ref[0]

    @pl.when(c > 0)
    def _():
        cext_ref[0:CONV_PAD, :] = cext_ref[q:q + CONV_PAD, :]

    cext_ref[CONV_PAD:CONV_PAD + q, :] = xbc_ref[0]
    xbc = cb_ref[...]
    for kk in range(SSD_CONV):
        r0 = CONV_PAD - (SSD_CONV - 1) + kk
        xbc = xbc + cext_ref[r0:r0 + q, :] * cw_ref[kk:kk + 1, :]
    xbc = _silu(xbc)
    xs = xbc[:, :SSD_DI]
    bm = xbc[:, SSD_DI:SSD_DI + SSD_G * SSD_N]
    cm = xbc[:, SSD_DI + SSD_G * SSD_N:]

    dtc = _softplus(dt_ref[0] + dtb_row_ref[...])
    dtr = _softplus(dtt_ref[0] + dtb_col_ref[...])
    if valid_len is not None:
        t_col = c * q + lax.broadcasted_iota(jnp.int32, (q, LANES), 0)
        t_row = c * q + lax.broadcasted_iota(jnp.int32, (SSD_H, q), 1)
        dtc = jnp.where(t_col < valid_len, dtc, 0.0)
        dtr = jnp.where(t_row < valid_len, dtr, 0.0)
    ii = lax.broadcasted_iota(jnp.int32, (q, q), 0)
    jj = lax.broadcasted_iota(jnp.int32, (q, q), 1)
    causal = jj <= ii
    tri = jnp.where(causal, 1.0, 0.0).astype(BF16)
    tri_t = jnp.where(ii <= jj, 1.0, 0.0).astype(BF16)
    acum_c = _dot_exact_lhs(tri, dtc * (-jnp.exp(alog_row_ref[...])))
    acum_r = _dot_exact_rhs(dtr * (-jnp.exp(alog_col_ref[...])), tri_t)
    total = acum_c[q - 1:q, :]
    expand = expand_ref[...]
    dte_x = _dot_exact_rhs(jnp.exp(total - acum_c) * dtc, expand)
    eac_x = _dot_exact_rhs(jnp.exp(acum_c), expand)
    cdec_x = _dot_exact_rhs(jnp.broadcast_to(jnp.exp(total), (SUBLANES, LANES)), expand)[0:1, :]

    gw = SSD_DI // SSD_G
    hg = SSD_H // SSD_G
    lane = lax.broadcasted_iota(jnp.int32, (q, 2 * SSD_P), 1)
    ys = []
    for g in range(SSD_G):
        gc = slice(g * gw, (g + 1) * gw)
        bg = bm[:, g * SSD_N:(g + 1) * SSD_N]
        cg = cm[:, g * SSD_N:(g + 1) * SSD_N]
        cbm = _bdot_nt(cg, bg)
        xg = xs[:, gc]
        s_old = s_ref[:, gc]
        y_g = _bdot(cg, s_old) * eac_x[:, gc]
        s_ref[:, gc] = s_old * cdec_x[:, gc] + _bdot(bg.T, xg * dte_x[:, gc])
        pairs = []
        for hp in range(hg // 2):
            wts = []
            for hh in range(2):
                h = g * hg + 2 * hp + hh
                seg = acum_c[:, h:h + 1] - acum_r[h:h + 1, :]
                dec = jnp.exp(jnp.where(causal, seg, -jnp.inf))
                wts.append((cbm * dec * dtr[h:h + 1, :]).astype(BF16))
            xp = xg[:, hp * 2 * SSD_P:(hp + 1) * 2 * SSD_P]
            xbd = jnp.concatenate([jnp.where(lane < SSD_P, xp, 0.0), jnp.where(lane >= SSD_P, xp, 0.0)],
                                  axis=0).astype(BF16)
            pairs.append(jnp.dot(jnp.concatenate(wts, axis=1), xbd, preferred_element_type=F32))
        y_g = y_g + jnp.concatenate(pairs, axis=1) + xg * dexp_ref[:, gc]
        zg = z_ref[0, :, gc]
        y_g = y_g * _silu(zg)
        y_g = y_g * lax.rsqrt(jnp.mean(y_g * y_g, axis=-1, keepdims=True) + NORM_EPS)
        ys.append(y_g)
    y_ref[0] = (jnp.concatenate(ys, axis=1) * ng_ref[...]).astype(y_ref.dtype)

    @pl.when(c == n_chunks - 1)
    def _():
        hl_ref[0] = s_ref[...]


def _ssd_scan(z, xbc, dt, dtt, cpre, conv_w, conv_b, dtb_row, dtb_col, alog_row, alog_col, d_exp, ng,
              expand, h0t, *, valid_len):
    bsz, seq, _ = z.shape
    q = SSD_CHUNK
    n_chunks = seq // q
    has_h0 = h0t is not None
    kern = functools.partial(_ssd_kernel, q=q, n_chunks=n_chunks, valid_len=valid_len, has_h0=has_h0)
    full = lambda *shape: pl.BlockSpec(shape, lambda b, c: (0,) * len(shape))
    in_specs = [
        pl.BlockSpec((1, q, SSD_DI), lambda b, c: (b, c, 0)),
        pl.BlockSpec((1, q, SSD_CONV_DIM), lambda b, c: (b, c, 0)),
        pl.BlockSpec((1, q, LANES), lambda b, c: (b, c, 0)),
        pl.BlockSpec((1, SSD_H, q), lambda b, c: (b, 0, c)),
        pl.BlockSpec((1, CONV_PAD, SSD_CONV_DIM), lambda b, c: (b, 0, 0)),
        full(SSD_CONV, SSD_CONV_DIM), full(1, SSD_CONV_DIM),
        full(1, LANES), full(SSD_H, 1), full(1, LANES), full(SSD_H, 1),
        full(1, SSD_DI), full(1, SSD_DI), full(LANES, SSD_DI),
    ]
    args = [z, xbc, dt, dtt, cpre, conv_w, conv_b, dtb_row, dtb_col, alog_row, alog_col, d_exp, ng, expand]
    if has_h0:
        in_specs.append(pl.BlockSpec((1, SSD_N, SSD_DI), lambda b, c: (b, 0, 0)))
        args.append(h0t)
    return pl.pallas_call(
        kern,
        grid=(bsz, n_chunks),
        in_specs=in_specs,
        out_specs=[pl.BlockSpec((1, q, SSD_DI), lambda b, c: (b, c, 0)),
                   pl.BlockSpec((1, SSD_N, SSD_DI), lambda b, c: (b, 0, 0))],
        out_shape=[jax.ShapeDtypeStruct((bsz, seq, SSD_DI), BF16),
                   jax.ShapeDtypeStruct((bsz, SSD_N, SSD_DI), F32)],
        scratch_shapes=[pltpu.VMEM((SSD_N, SSD_DI), F32), pltpu.VMEM((CONV_PAD + q, SSD_CONV_DIM), F32)],
        compiler_params=_params("arbitrary", "arbitrary"),
        name="ssd_scan",
    )(*args)


def _first_max(vals):
    m = vals[0]
    for v in vals[1:]:
        m = jnp.maximum(m, v)
    flags = []
    taken = jnp.zeros_like(m)
    for v in vals:
        f = jnp.where(v == m, 1.0 - taken, 0.0)
        taken = taken + f
        flags.append(f)
    return m, flags


def _sum_rows(rows):
    tot = rows[0]
    for r in rows[1:]:
        tot = tot + r
    return tot


def _router_rows(x, wh_ref, wl_ref, b_ref):
    xh = x.astype(BF16)
    xl = (x - xh.astype(F32)).astype(BF16)
    logits = (jnp.dot(xh, wh_ref[...], preferred_element_type=F32)
              + jnp.dot(xh, wl_ref[...], preferred_element_type=F32)
              + jnp.dot(xl, wh_ref[...], preferred_element_type=F32))
    lt = logits.T
    rows = [lt[e:e + 1, :] for e in range(N_EXPERTS)]
    mx = rows[0]
    for r in rows[1:]:
        mx = jnp.maximum(mx, r)
    ex = [jnp.exp(r - mx) for r in rows]
    den = _sum_rows(ex)
    probs = [r / den for r in ex]
    sel = [probs[e] + b_ref[e:e + 1, :] for e in range(N_EXPERTS)]
    scores, chosen = [], []
    for g in range(N_EGROUPS):
        v = sel[g * EXP_PER_GROUP:(g + 1) * EXP_PER_GROUP]
        m1, f1 = _first_max(v)
        rest = [jnp.where(f > 0.0, -jnp.inf, vi) for f, vi in zip(f1, v)]
        m2, f2 = _first_max(rest)
        scores.append(m1 + m2)
        chosen.extend([a + b for a, b in zip(f1, f2)])
    return probs, chosen, scores


def _router_kernel(x_ref, wh_ref, wl_ref, b_ref, o_ref):
    probs, chosen, scores = _router_rows(x_ref[...], wh_ref, wl_ref, b_ref)
    _, gflag = _first_max(scores)
    gates = [probs[e] * chosen[e] * gflag[e // EXP_PER_GROUP] for e in range(N_EXPERTS)]
    tot = _sum_rows(gates)
    for e in range(N_EXPERTS):
        o_ref[e:e + 1, :] = gates[e] / tot


def _route_group_kernel(x_ref, wh_ref, wl_ref, b_ref, grp_ref, gate_ref, xbf_ref, *, tm):
    xbf_ref[...] = x_ref[...].astype(BF16)
    probs, chosen, scores = _router_rows(x_ref[...], wh_ref, wl_ref, b_ref)
    _, gflag = _first_max(scores)
    grp = _sum_rows([gflag[g] * float(g) for g in range(1, N_EGROUPS)])
    grp_ref[...] = grp.astype(jnp.int32)
    local = [_sum_rows([probs[g * EXP_PER_GROUP + le] * chosen[g * EXP_PER_GROUP + le] * gflag[g]
                        for g in range(N_EGROUPS)]) for le in range(EXP_PER_GROUP)]
    tot = _sum_rows(local)
    rows = jnp.concatenate([r / tot for r in local] + [jnp.zeros((LANES - EXP_PER_GROUP, tm), F32)], axis=0)
    gate_ref[...] = rows.T


def _route_group(x, wr_hi, wr_lo, b_col, *, tm):
    t, d = x.shape
    return pl.pallas_call(
        functools.partial(_route_group_kernel, tm=tm),
        grid=(t // tm,),
        in_specs=[pl.BlockSpec((tm, d), lambda i: (i, 0)),
                  pl.BlockSpec((d, LANES), lambda i: (0, 0)),
                  pl.BlockSpec((d, LANES), lambda i: (0, 0)),
                  pl.BlockSpec((N_EXPERTS, 1), lambda i: (0, 0))],
        out_specs=[pl.BlockSpec((1, tm), lambda i: (0, i)), pl.BlockSpec((tm, LANES), lambda i: (i, 0)),
                   pl.BlockSpec((tm, d), lambda i: (i, 0))],
        out_shape=[jax.ShapeDtypeStruct((1, t), jnp.int32), jax.ShapeDtypeStruct((t, LANES), F32),
                   jax.ShapeDtypeStruct((t, d), BF16)],
        compiler_params=_params("arbitrary"),
        name="route_group",
    )(x, wr_hi, wr_lo, b_col)


def _moe_sorted_kernel(sw_ref, sslot_ref, se_ref, ns_ref, x_ref, pos_ref, gate_ref, wgu_ref, wd_ref, o_ref,
                       g3_ref, xt_ref, gt_ref, acc_ref, *, tr, steps):
    sb = pl.program_id(0)
    s = pl.program_id(1)
    valid = s < ns_ref[sb]
    slot = sslot_ref[sb * steps + s]
    e = se_ref[sb * steps + s]

    @pl.when(s == 0)
    def _():
        hi, mid, lo = _split3(gate_ref[...])
        g3_ref[0] = hi
        g3_ref[1] = mid
        g3_ref[2] = lo
        o_ref[...] = jnp.zeros(o_ref.shape, o_ref.dtype)

    @pl.when(jnp.logical_and(valid, e == 0))
    def _():
        n_src = x_ref.shape[0]
        row = slot * tr + lax.broadcasted_iota(jnp.int32, (tr, n_src), 0)
        pick = jnp.where(pos_ref[0] == row, 1.0, 0.0).astype(BF16)
        xt_ref[slot] = jnp.dot(pick, x_ref[...], preferred_element_type=F32).T.astype(BF16)
        gt_ref[slot] = (jnp.dot(pick, g3_ref[0], preferred_element_type=F32)
                        + jnp.dot(pick, g3_ref[1], preferred_element_type=F32)
                        + jnp.dot(pick, g3_ref[2], preferred_element_type=F32)).T
        acc_ref[slot] = jnp.zeros(acc_ref.shape[1:], F32)

    @pl.when(valid)
    def _():
        f = wd_ref.shape[2]
        hgu = jnp.dot(wgu_ref[0], xt_ref[slot], preferred_element_type=F32)
        grow = gt_ref[slot, pl.ds(e, 1), :]
        h = _silu(hgu[:f]) * hgu[f:] * grow
        acc_ref[slot] += jnp.dot(wd_ref[0], h.astype(BF16), preferred_element_type=F32)

    @pl.when(jnp.logical_and(valid, e == EXP_PER_GROUP - 1))
    def _():
        o_ref[pl.ds(pl.multiple_of(slot * tr, tr), tr), :] = acc_ref[slot].T.astype(o_ref.dtype)


def _moe_sorted(x_bf, pos_row, gate4, plan, w_gate_up_t, w_down_t, *, sb_rows, tr):
    t, d = x_bf.shape
    f = w_down_t.shape[2]
    n_sb = t // sb_rows
    cap = sb_rows + N_EGROUPS * tr
    slots = cap // tr
    steps = slots * EXP_PER_GROUP
    w_index = lambda sb, s, sw, sslot, se, ns: (sw[sb * steps + s], 0, 0)
    per_sb = lambda *shape: pl.BlockSpec(shape, lambda sb, s, sw, sslot, se, ns: (sb,) + (0,) * (len(shape) - 1))
    grid_spec = pltpu.PrefetchScalarGridSpec(
        num_scalar_prefetch=4,
        grid=(n_sb, steps),
        in_specs=[per_sb(sb_rows, d), per_sb(1, 1, sb_rows), per_sb(sb_rows, LANES),
                  pl.BlockSpec((1, 2 * f, d), w_index), pl.BlockSpec((1, d, f), w_index)],
        out_specs=per_sb(cap, d),
        scratch_shapes=[pltpu.VMEM((3, sb_rows, LANES), BF16), pltpu.VMEM((slots, d, tr), BF16),
                        pltpu.VMEM((slots, LANES, tr), F32), pltpu.VMEM((slots, d, tr), F32)],
    )
    return pl.pallas_call(
        functools.partial(_moe_sorted_kernel, tr=tr, steps=steps),
        grid_spec=grid_spec,
        out_shape=jax.ShapeDtypeStruct((n_sb * cap, d), BF16),
        compiler_params=_params("arbitrary", "arbitrary"),
        name="moe_sorted_experts",
    )(*plan, x_bf, pos_row, gate4, w_gate_up_t, w_down_t)


def _dispatch_plan(grp, sb_rows, tr):
    n_sb = grp.shape[0] // sb_rows
    slots = (sb_rows + N_EGROUPS * tr) // tr
    steps = slots * EXP_PER_GROUP
    i32 = jnp.int32
    oh = (grp.reshape(n_sb, sb_rows, 1) == jnp.arange(N_EGROUPS, dtype=i32)).astype(i32)
    csum = jnp.cumsum(oh, axis=1)
    padded = ((csum[:, -1] + tr - 1) // tr) * tr
    gend = jnp.cumsum(padded, axis=1)
    pos = jnp.sum(oh * ((gend - padded)[:, None, :] + csum - 1), axis=2)
    slot_row0 = jnp.arange(slots, dtype=i32) * tr
    tile_group = jnp.minimum(jnp.sum((slot_row0[None, :, None] >= gend[:, None, :]).astype(i32), axis=2),
                             N_EGROUPS - 1)
    n_used = gend[:, -1] // tr
    pair_slot = jnp.repeat(jnp.arange(slots, dtype=i32), EXP_PER_GROUP)[None, :]
    pair_e = jnp.tile(jnp.arange(EXP_PER_GROUP, dtype=i32), slots)[None, :]
    pair_w = (jnp.take_along_axis(tile_group, jnp.broadcast_to(pair_slot, (n_sb, steps)), axis=1) * EXP_PER_GROUP
              + pair_e)
    key = jnp.where(pair_slot < n_used[:, None], pair_w * slots + pair_slot, N_EXPERTS * slots + pair_slot)
    order = jnp.argsort(key, axis=1).astype(i32)
    n_steps = n_used * EXP_PER_GROUP
    step_w = jnp.take_along_axis(pair_w, order, axis=1)
    last_w = jnp.take_along_axis(step_w, jnp.maximum(n_steps - 1, 0)[:, None], axis=1)
    step_w = jnp.where(jnp.arange(steps, dtype=i32)[None, :] < n_steps[:, None], step_w, last_w)
    plan = (step_w.reshape(-1).astype(i32), (order // EXP_PER_GROUP).reshape(-1), (order % EXP_PER_GROUP).reshape(-1),
            n_steps.astype(i32))
    return pos.astype(i32), plan


def _router(x, wr_hi, wr_lo, b_col, *, tm):
    t, d = x.shape
    return pl.pallas_call(
        _router_kernel,
        grid=(t // tm,),
        in_specs=[pl.BlockSpec((tm, d), lambda i: (i, 0)),
                  pl.BlockSpec((d, LANES), lambda i: (0, 0)),
                  pl.BlockSpec((d, LANES), lambda i: (0, 0)),
                  pl.BlockSpec((N_EXPERTS, 1), lambda i: (0, 0))],
        out_specs=pl.BlockSpec((N_EXPERTS, tm), lambda i: (0, i)),
        out_shape=jax.ShapeDtypeStruct((N_EXPERTS, t), F32),
        compiler_params=_params("arbitrary"),
        name="router",
    )(x, wr_hi, wr_lo, b_col)


def _moe_kernel(x_ref, gate_ref, wg_ref, wu_ref, wd_ref, o_ref, xbf_ref):
    e = pl.program_id(1)

    @pl.when(e == 0)
    def _():
        xbf_ref[...] = x_ref[...].astype(BF16)
        o_ref[...] = jnp.zeros(o_ref.shape, F32)

    xb = xbf_ref[...]
    gate = gate_ref[...]
    lane = lax.broadcasted_iota(jnp.int32, gate.shape, 1)
    gcol = jnp.sum(jnp.where(lane == e, gate, 0.0), axis=1, keepdims=True)
    hgate = jnp.dot(xb, wg_ref[0, 0].astype(BF16), preferred_element_type=F32)
    hup = jnp.dot(xb, wu_ref[0, 0].astype(BF16), preferred_element_type=F32)
    o_ref[...] += _bdot(_silu(hgate) * hup * gcol, wd_ref[0, 0])


def _moe(x, gate, w_gate, w_up, w_down, *, layer, tm):
    t, d = x.shape
    f = w_gate.shape[3]
    return pl.pallas_call(
        _moe_kernel,
        grid=(t // tm, N_EXPERTS),
        in_specs=[pl.BlockSpec((tm, d), lambda i, e: (i, 0)),
                  pl.BlockSpec((tm, N_EXPERTS), lambda i, e: (i, 0)),
                  pl.BlockSpec((1, 1, d, f), lambda i, e: (layer, e, 0, 0)),
                  pl.BlockSpec((1, 1, d, f), lambda i, e: (layer, e, 0, 0)),
                  pl.BlockSpec((1, 1, f, d), lambda i, e: (layer, e, 0, 0))],
        out_specs=pl.BlockSpec((tm, d), lambda i, e: (i, 0)),
        out_shape=jax.ShapeDtypeStruct((t, d), F32),
        scratch_shapes=[pltpu.VMEM((tm, d), BF16)],
        compiler_params=_params("arbitrary", "arbitrary"),
        name="moe_experts",
    )(x, gate, w_gate, w_up, w_down)


def _ln_ple_body(x1, ffn, p_ref, g_ref, b_ref, wg_ref, bg_ref, wp_ref, o_ref):
    x = _ln(DN_ALPHA * x1 + ffn, g_ref[...], b_ref[...])
    gate = jax.nn.sigmoid(_bdot(x, wg_ref[...]) + bg_ref[...])
    o_ref[...] = x + gate * _bdot(p_ref[...], wp_ref[...])


def _ln_ple_kernel(x_ref, f_ref, p_ref, *rest):
    _ln_ple_body(x_ref[...], f_ref[...], p_ref, *rest)


def _unsort_ln_ple_kernel(y_ref, pos_ref, x_ref, p_ref, *rest):
    tm, cap = x_ref.shape[0], y_ref.shape[0]
    col = lax.broadcasted_iota(jnp.int32, (tm, cap), 1)
    pick = jnp.where(pos_ref[...] == col, 1.0, 0.0).astype(BF16)
    ffn = jnp.dot(pick, y_ref[...], preferred_element_type=F32)
    _ln_ple_body(x_ref[...], ffn, p_ref.at[0], *rest)


def _unsort_ln_ple(y_sorted, pos_col, x, p_all, g, b, w_gate, b_gate, w_p, *, layer, sb_rows, tm):
    t, d = x.shape
    pd = p_all.shape[2]
    n_sb = t // sb_rows
    n_sub = sb_rows // tm
    cap = y_sorted.shape[0] // n_sb
    rows = lambda cols: pl.BlockSpec((tm, cols), lambda sb, i: (sb * n_sub + i, 0))
    fixed = lambda r, c: pl.BlockSpec((r, c), lambda sb, i: (0, 0))
    return pl.pallas_call(
        _unsort_ln_ple_kernel,
        grid=(n_sb, n_sub),
        in_specs=[pl.BlockSpec((cap, d), lambda sb, i: (sb, 0)), rows(1), rows(d),
                  pl.BlockSpec((1, tm, pd), lambda sb, i: (layer, sb * n_sub + i, 0)),
                  fixed(1, d), fixed(1, d), fixed(d, d), fixed(1, d), fixed(pd, d)],
        out_specs=rows(d),
        out_shape=jax.ShapeDtypeStruct((t, d), F32),
        compiler_params=_params("arbitrary", "arbitrary"),
        name="unsort_ln_ple",
    )(y_sorted, pos_col, x, p_all, g, b, w_gate, b_gate, w_p)


def _ln_ple(x, ffn, p, g, b, w_gate, b_gate, w_p, *, tm):
    t, d = x.shape
    pd = p.shape[1]
    rows = lambda cols: pl.BlockSpec((tm, cols), lambda i: (i, 0))
    fixed = lambda r, c: pl.BlockSpec((r, c), lambda i: (0, 0))
    return pl.pallas_call(
        _ln_ple_kernel,
        grid=(t // tm,),
        in_specs=[rows(d), rows(d), rows(pd), fixed(1, d), fixed(1, d), fixed(d, d), fixed(1, d), fixed(pd, d)],
        out_specs=rows(d),
        out_shape=jax.ShapeDtypeStruct((t, d), F32),
        compiler_params=_params("arbitrary"),
        name="ln_ple",
    )(x, ffn, p, g, b, w_gate, b_gate, w_p)


def _row_tile(rows, want):
    tm = min(rows, want)
    assert rows % tm == 0
    return tm


POOL_TM = 512
MM_TM = 512
QKV_TM = 512
ATTN_TQ = 512
MOE_TM = 1024
MOE_SORT_BLOCK = 2048
MOE_SORT_TR = 256
ROUTER_TM = 1024
PLE_TM = 512
DECODE_PAGES_PER_STEP = 8


def kernel(x_prompt, x_sample, p_prompt, p_sample, state_pool, cache_k, cache_v, page_table, state_ssm, state_conv,
           ln_g, ln_b, w_pool_in, w_pool_grp, pool_scale, w_pool_out, w_diff_in, diff_lambda_q, diff_lambda_k,
           diff_norm_g, w_diff_out, w_ssd_in, ssd_conv_w, ssd_conv_b, ssd_dt_bias, ssd_a_log, ssd_d, ssd_norm_g,
           w_ssd_out, w_router, b_router, w_exp_gate, w_exp_up, w_exp_down, w_ple, w_ple_gate, b_ple_gate):
    bp, seq, d = x_prompt.shape
    bs, tdec, _ = x_sample.shape
    n_pages = page_table.shape[1]
    past = n_pages * cache_k.shape[2]
    tp = bp * seq
    ts = bs * tdec

    def to_tm(a):
        return jnp.swapaxes(a, 0, 1).reshape((ts,) + a.shape[2:])

    def from_tm(a):
        return jnp.swapaxes(a.reshape((tdec, bs) + a.shape[1:]), 0, 1)

    row = lambda v: v.reshape(1, -1)

    wr = jnp.pad(w_router, ((0, 0), (0, LANES - N_EXPERTS)))
    wr_hi = wr.astype(BF16)
    wr_lo = (wr - wr_hi.astype(F32)).astype(BF16)
    br_col = b_router.reshape(N_EXPERTS, 1)

    p_prompt_rows = p_prompt.reshape(DEPTH, tp, PLE_DIM)

    def finish(x1, i, sparse):
        t = x1.shape[0]
        tail = (row(ln_g[i, 1]), row(ln_b[i, 1]), w_ple_gate[i].astype(BF16), row(b_ple_gate[i]),
                w_ple[i].astype(BF16))
        if sparse:
            sb_rows, tr = _row_tile(t, MOE_SORT_BLOCK), MOE_SORT_TR
            grp, gate4, x1_bf = _route_group(x1, wr_hi, wr_lo, br_col, tm=_row_tile(t, ROUTER_TM))
            pos, plan = _dispatch_plan(grp[0], sb_rows, tr)
            w_gu_t = jnp.concatenate([jnp.swapaxes(w_exp_gate[i], 1, 2), jnp.swapaxes(w_exp_up[i], 1, 2)],
                                     axis=1).astype(BF16)
            w_d_t = jnp.swapaxes(w_exp_down[i], 1, 2).astype(BF16)
            y_sorted = _moe_sorted(x1_bf, pos.reshape(-1, 1, sb_rows), gate4, plan, w_gu_t, w_d_t,
                                   sb_rows=sb_rows, tr=tr)
            return _unsort_ln_ple(y_sorted, pos.reshape(t, 1), x1, p_prompt_rows, *tail, layer=i,
                                  sb_rows=sb_rows, tm=_row_tile(sb_rows, PLE_TM))
        gate = _router(x1, wr_hi, wr_lo, br_col, tm=_row_tile(t, ROUTER_TM)).T
        ffn = _moe(x1, gate, w_exp_gate, w_exp_up, w_exp_down, layer=i, tm=_row_tile(t, MOE_TM))
        return _ln_ple(x1, ffn, to_tm(p_sample[i]), *tail, tm=_row_tile(t, PLE_TM))

    xp = x_prompt.reshape(tp, d)
    xs = to_tm(x_sample)
    pos_p = jnp.arange(seq, dtype=jnp.int32)
    pos_s = jnp.repeat(past + jnp.arange(tdec, dtype=jnp.int32), bs)
    outs = {k: [] for k in ("pool_p", "pool_s", "k_p", "v_p", "k_s", "v_s", "ssm_p", "conv_p", "ssm_s", "conv_s")}

    for i in range(DEPTH):
        kind, j = i % N_MIXERS, i // N_MIXERS
        g1, b1 = row(ln_g[i, 0]), row(ln_b[i, 0])
        if kind == 0:
            wts = (w_pool_in[j].astype(BF16), w_pool_grp[j].astype(BF16), row(pool_scale[j]),
                   w_pool_out[j].astype(BF16), g1, b1)
            x1p, newp = _pool_layer(xp.reshape(bp, seq, d), jnp.zeros((bp, POOL_BUF + 1, d), F32), *wts,
                                    tm=_row_tile(seq, POOL_TM), stride=1, pos0=0)
            pre_s = jnp.swapaxes(state_pool[j], 0, 1).reshape(1, POOL_BUF * bs, d)
            pre_s = jnp.pad(pre_s, ((0, 0), (bs, 0), (0, 0)))
            x1s, news = _pool_layer(xs.reshape(1, ts, d), pre_s, *wts, tm=ts, stride=bs, pos0=past)
            outs["pool_p"].append(newp)
            outs["pool_s"].append(jnp.swapaxes(news.reshape(POOL_BUF, bs, d), 0, 1))
            x1p = x1p.reshape(tp, d)
            x1s = x1s.reshape(ts, d)
        elif kind == 1:
            lam_init = 0.8 - 0.6 * math.exp(-0.3 * i)
            w_in = w_diff_in[j].astype(BF16)
            w_out = w_diff_out[j].astype(BF16)
            lq, lk, ng = diff_lambda_q[j], diff_lambda_k[j], row(diff_norm_g[j])
            cos_p, sin_p = _rope_tables(pos_p)
            q, k, v = _qkv_rope(xp.reshape(bp, seq, d), w_in, cos_p, sin_p, tm=_row_tile(seq, QKV_TM))
            o = _diff_attn_prompt(q, k, v, lq, lk, ng, lam_init, tq=_row_tile(seq, ATTN_TQ))
            x1p = _mm_res_ln(o.reshape(tp, d), w_out, xp, g1, b1, tm=_row_tile(tp, MM_TM))
            outs["k_p"].append(k.reshape(bp, seq, DIFF_HEADS, 2, DIFF_DH))
            outs["v_p"].append(v.reshape(bp, seq, DIFF_HEADS, DIFF_VD))

            cos_s, sin_s = _rope_tables(pos_s)
            q, k, v = _qkv_rope(xs.reshape(1, ts, d), w_in, cos_s, sin_s, tm=ts)
            q, k, v = from_tm(q[0]), from_tm(k[0]), from_tm(v[0])
            q5 = q.reshape(bs, tdec, DIFF_HEADS * 2, DIFF_DH)
            eye = jnp.eye(DIFF_HEADS * 2, dtype=F32)
            qbd = jnp.einsum('bthd,hg->bhtgd', q5, eye)
            qbd = jnp.pad(qbd, ((0, 0), (0, 0), (0, DEC_TPAD - tdec), (0, 0), (0, 0)))
            qbd = qbd.reshape(bs, DEC_ROWS, d)
            kt_pages = jnp.transpose(cache_k, (0, 1, 3, 4, 5, 2)).reshape(-1, d, PAGE_SIZE)
            v_pages = cache_v.reshape(-1, PAGE_SIZE * DIFF_HEADS, DIFF_VD)
            knew = jnp.pad(jnp.swapaxes(k, 1, 2), ((0, 0), (0, 0), (0, PAGE_SIZE - tdec)))
            vnew = jnp.pad(v, ((0, 0), (0, PAGE_SIZE - tdec), (0, 0))).reshape(bs, PAGE_SIZE * DIFF_HEADS, DIFF_VD)
            o = _diff_attn_sample(qbd, kt_pages, v_pages, page_table, knew, vnew, lq, lk, ng, lam_init,
                                  pages_per_step=DECODE_PAGES_PER_STEP, page0=j * cache_k.shape[1])
            o = to_tm(o[:, :tdec])
            x1s = _mm_res_ln(o, w_out, xs, g1, b1, tm=ts)
            outs["k_s"].append(k.reshape(bs, tdec, DIFF_HEADS, 2, DIFF_DH))
            outs["v_s"].append(v.reshape(bs, tdec, DIFF_HEADS, DIFF_VD))
        else:
            w_in = w_ssd_in[j]
            w_z = w_in[:, :SSD_DI].astype(BF16)
            w_xbc = w_in[:, SSD_DI:SSD_DI + SSD_CONV_DIM].astype(BF16)
            w_dt = jnp.pad(w_in[:, SSD_DI + SSD_CONV_DIM:], ((0, 0), (0, LANES - SSD_H))).astype(BF16)
            w_out = w_ssd_out[j].astype(BF16)
            dtb_row = jnp.pad(ssd_dt_bias[j], (0, LANES - SSD_H)).reshape(1, LANES)
            alog_row = jnp.pad(ssd_a_log[j], (0, LANES - SSD_H)).reshape(1, LANES)
            dtb_col = ssd_dt_bias[j].reshape(SSD_H, 1)
            alog_col = ssd_a_log[j].reshape(SSD_H, 1)
            d_exp = jnp.repeat(ssd_d[j], SSD_P).reshape(1, SSD_DI)
            ng = row(ssd_norm_g[j])
            expand = (jnp.arange(SSD_DI)[None, :] // SSD_P == jnp.arange(LANES)[:, None]).astype(BF16)
            shared = (ssd_conv_w[j], row(ssd_conv_b[j]), dtb_row, dtb_col, alog_row, alog_col, d_exp, ng, expand)

            def in_proj(x, tm):
                z = _matmul(x, w_z, tm=tm, tn=1024)
                xbc = _matmul(x, w_xbc, tm=tm, tn=1024)
                dt = _matmul(x, w_dt, tm=tm, tn=LANES)
                return z, xbc, dt

            def hlast_to_state(hl, b):
                return jnp.transpose(hl.reshape(b, SSD_N, SSD_H, SSD_P), (0, 2, 3, 1))

            z, xbc, dt = in_proj(xp, _row_tile(tp, MM_TM))
            dtt = jnp.swapaxes(dt.reshape(bp, seq, LANES)[:, :, :SSD_H], 1, 2)
            y, hl = _ssd_scan(z.reshape(bp, seq, SSD_DI), xbc.reshape(bp, seq, SSD_CONV_DIM),
                              dt.reshape(bp, seq, LANES), dtt, jnp.zeros((bp, CONV_PAD, SSD_CONV_DIM), F32),
                              *shared, None, valid_len=None)
            x1p = _mm_res_ln(y.reshape(tp, SSD_DI), w_out, xp, g1, b1, tm=_row_tile(tp, MM_TM))
            outs["ssm_p"].append(hlast_to_state(hl, bp))
            outs["conv_p"].append(xbc.reshape(bp, seq, SSD_CONV_DIM)[:, seq - (SSD_CONV - 1):])

            z, xbc, dt = in_proj(xs, ts)
            z, xbc, dt = from_tm(z), from_tm(xbc), from_tm(dt)
            padt = ((0, 0), (0, SSD_CHUNK - tdec), (0, 0))
            dtp = jnp.pad(dt, padt)
            cpre = jnp.pad(state_conv[j], ((0, 0), (CONV_PAD - (SSD_CONV - 1), 0), (0, 0)))
            h0t = jnp.transpose(state_ssm[j], (0, 3, 1, 2)).reshape(bs, SSD_N, SSD_DI)
            y, hl = _ssd_scan(jnp.pad(z, padt), jnp.pad(xbc, padt), dtp, jnp.swapaxes(dtp[:, :, :SSD_H], 1, 2),
                              cpre, *shared, h0t, valid_len=tdec)
            x1s = _mm_res_ln(to_tm(y[:, :tdec]), w_out, xs, g1, b1, tm=ts)
            outs["ssm_s"].append(hlast_to_state(hl, bs))
            ext = jnp.concatenate([state_conv[j], xbc], axis=1)
            outs["conv_s"].append(ext[:, -(SSD_CONV - 1):])

        xp = finish(x1p, i, sparse=True)
        xs = finish(x1s, i, sparse=False)

    st = lambda name: jnp.stack(outs[name])
    return (xp.reshape(bp, seq, d), from_tm(xs), st("pool_p"), st("pool_s"), st("k_p"), st("v_p"), st("k_s"),
            st("v_s"), st("ssm_p"), st("conv_p"), st("ssm_s"), st("conv_s"))
```

```python
import functools
import math

import jax
import jax.numpy as jnp
from jax import lax
from jax.experimental import pallas as pl
from jax.experimental.pallas import tpu as pltpu

F32 = jnp.float32
BF16 = jnp.bfloat16

D_MODEL = 1024
DEPTH = 4
N_MIXERS = 3
PAGE_SIZE = 128

POOL_WINDOWS = (2, 4, 8, 16)
POOL_GD = D_MODEL // len(POOL_WINDOWS)
POOL_BUF = max(POOL_WINDOWS) - 1

DIFF_DH = 64
DIFF_VD = 2 * DIFF_DH
DIFF_HEADS = D_MODEL // DIFF_VD
ROPE_THETA = 10000.0

SSD_DI = 2 * D_MODEL
SSD_P = 64
SSD_H = SSD_DI // SSD_P
SSD_G = 4
SSD_N = 128
SSD_CONV = 4
SSD_CONV_DIM = SSD_DI + 2 * SSD_G * SSD_N
SSD_CHUNK = 128

N_EXPERTS = 16
N_EGROUPS = 4
EXP_PER_GROUP = N_EXPERTS // N_EGROUPS
D_FF_EXPERT = D_MODEL // 2

PLE_DIM = 256
DN_ALPHA = (2 * DEPTH) ** 0.25
NORM_EPS = 1e-5

LANES = 128
SUBLANES = 8
VMEM_LIMIT = 56 * 1024 * 1024


def _params(*sem):
    return pltpu.CompilerParams(dimension_semantics=sem, vmem_limit_bytes=VMEM_LIMIT)


def _bdot(a, b):
    return jnp.dot(a.astype(BF16), b.astype(BF16), preferred_element_type=F32)


def _bdot_nt(a, b):
    return lax.dot_general(a.astype(BF16), b.astype(BF16), (((1,), (1,)), ((), ())),
                           preferred_element_type=F32)


def _split3(x):
    hi = x.astype(BF16)
    r = x - hi.astype(F32)
    mid = r.astype(BF16)
    lo = (r - mid.astype(F32)).astype(BF16)
    return hi, mid, lo


def _dot_exact_rhs(x, m):
    hi, mid, lo = _split3(x)
    return (jnp.dot(hi, m, preferred_element_type=F32) + jnp.dot(mid, m, preferred_element_type=F32)
            + jnp.dot(lo, m, preferred_element_type=F32))


def _dot_exact_lhs(m, x):
    hi, mid, lo = _split3(x)
    return (jnp.dot(m, hi, preferred_element_type=F32) + jnp.dot(m, mid, preferred_element_type=F32)
            + jnp.dot(m, lo, preferred_element_type=F32))


def _ln(v, g, b):
    mu = jnp.mean(v, axis=-1, keepdims=True)
    c = v - mu
    var = jnp.mean(c * c, axis=-1, keepdims=True)
    return c * lax.rsqrt(var + NORM_EPS) * g + b


def _silu(v):
    return v * jax.nn.sigmoid(v)


def _softplus(v):
    return jnp.maximum(v, 0.0) + jnp.log(1.0 + jnp.exp(-jnp.abs(v)))


def _pool_kernel(x_ref, pre_ref, win_ref, wgrp_ref, scale_ref, wout_ref, g_ref, b_ref,
                 y_ref, newp_ref, ext_ref, *, tm, stride, pos0, n_tiles):
    i = pl.program_id(1)
    pre_rows = (POOL_BUF + 1) * stride

    @pl.when(i == 0)
    def _():
        ext_ref[0:pre_rows, :] = pre_ref[0]

    @pl.when(i > 0)
    def _():
        ext_ref[0:pre_rows, :] = ext_ref[tm:tm + pre_rows, :]

    x = x_ref[0]
    u = _bdot(x, win_ref[...])
    ext_ref[pre_rows:pre_rows + tm, :] = u
    row = lax.broadcasted_iota(jnp.int32, (tm, 1), 0)
    pos = pos0 + (i * tm + row) // stride
    mixed = []
    for gi, w in enumerate(POOL_WINDOWS):
        c0 = gi * POOL_GD
        ug = u[:, c0:c0 + POOL_GD]
        s = ug
        for k in range(1, w):
            r0 = pre_rows - k * stride
            s = s + ext_ref[r0:r0 + tm, c0:c0 + POOL_GD]
        cnt = jnp.minimum(w, pos + 1).astype(F32)
        pooled = s / cnt - ug
        mixed.append(_bdot(pooled, wgrp_ref[gi]))
    mixed = jnp.concatenate(mixed, axis=1) * scale_ref[...]
    mix = _bdot(mixed, wout_ref[...])
    y_ref[0] = _ln(DN_ALPHA * x + mix, g_ref[...], b_ref[...])

    @pl.when(i == n_tiles - 1)
    def _():
        newp_ref[0] = ext_ref[tm + stride:tm + pre_rows, :]


def _pool_layer(x, prefix, w_in, w_grp, scale, w_out, g, b, *, tm, stride, pos0):
    bsz, rows, d = x.shape
    n_tiles = rows // tm
    pre_rows = (POOL_BUF + 1) * stride
    kern = functools.partial(_pool_kernel, tm=tm, stride=stride, pos0=pos0, n_tiles=n_tiles)
    full = lambda *shape: pl.BlockSpec(shape, lambda bi, i: (0,) * len(shape))
    return pl.pallas_call(
        kern,
        grid=(bsz, n_tiles),
        in_specs=[
            pl.BlockSpec((1, tm, d), lambda bi, i: (bi, i, 0)),
            pl.BlockSpec((1, pre_rows, d), lambda bi, i: (bi, 0, 0)),
            full(d, d), full(len(POOL_WINDOWS), POOL_GD, POOL_GD), full(1, d), full(d, d),
            full(1, d), full(1, d),
        ],
        out_specs=[
            pl.BlockSpec((1, tm, d), lambda bi, i: (bi, i, 0)),
            pl.BlockSpec((1, POOL_BUF * stride, d), lambda bi, i: (bi, 0, 0)),
        ],
        out_shape=[jax.ShapeDtypeStruct((bsz, rows, d), F32),
                   jax.ShapeDtypeStruct((bsz, POOL_BUF * stride, d), F32)],
        scratch_shapes=[pltpu.VMEM((pre_rows + tm, d), F32)],
        compiler_params=_params("arbitrary", "arbitrary"),
        name="pool_layer",
    )(x, prefix, w_in, w_grp, scale, w_out, g, b)


def _mm_kernel(x_ref, w_ref, o_ref):
    o_ref[...] = _bdot(x_ref[...], w_ref[...]).astype(o_ref.dtype)


def _matmul(x, w, *, tm, tn, out_dtype=F32):
    t, k = x.shape
    n = w.shape[1]
    return pl.pallas_call(
        _mm_kernel,
        grid=(n // tn, t // tm),
        in_specs=[pl.BlockSpec((tm, k), lambda j, i: (i, 0)),
                  pl.BlockSpec((k, tn), lambda j, i: (0, j))],
        out_specs=pl.BlockSpec((tm, tn), lambda j, i: (i, j)),
        out_shape=jax.ShapeDtypeStruct((t, n), out_dtype),
        compiler_params=_params("arbitrary", "arbitrary"),
        name="matmul",
    )(x, w)


def _mm_res_ln_kernel(a_ref, w_ref, x_ref, g_ref, b_ref, o_ref):
    mix = _bdot(a_ref[...], w_ref[...])
    o_ref[...] = _ln(DN_ALPHA * x_ref[...] + mix, g_ref[...], b_ref[...])


def _mm_res_ln(a, w, x, g, b, *, tm):
    t, k = a.shape
    d = w.shape[1]
    return pl.pallas_call(
        _mm_res_ln_kernel,
        grid=(t // tm,),
        in_specs=[pl.BlockSpec((tm, k), lambda i: (i, 0)),
                  pl.BlockSpec((k, d), lambda i: (0, 0)),
                  pl.BlockSpec((tm, d), lambda i: (i, 0)),
                  pl.BlockSpec((1, d), lambda i: (0, 0)),
                  pl.BlockSpec((1, d), lambda i: (0, 0))],
        out_specs=pl.BlockSpec((tm, d), lambda i: (i, 0)),
        out_shape=jax.ShapeDtypeStruct((t, d), F32),
        compiler_params=_params("arbitrary"),
        name="mm_res_ln",
    )(a, w, x, g, b)


def _qkv_kernel(x_ref, w_ref, cos_ref, sin_ref, q_ref, k_ref, v_ref, *, tm):
    x = x_ref[0].astype(BF16)
    cos = cos_ref[...]
    sin = sin_ref[...]
    lane = lax.broadcasted_iota(jnp.int32, (tm, LANES), 1)
    first_half = (lane % DIFF_DH) < (DIFF_DH // 2)

    def rope(y):
        outs = []
        for c in range(D_MODEL // LANES):
            yc = y[:, c * LANES:(c + 1) * LANES]
            sw = jnp.where(first_half, pltpu.roll(yc, LANES - DIFF_DH // 2, 1),
                           pltpu.roll(yc, DIFF_DH // 2, 1))
            outs.append(yc * cos + sw * sin)
        return jnp.concatenate(outs, axis=1)

    q_ref[0] = rope(jnp.dot(x, w_ref[:, 0:D_MODEL], preferred_element_type=F32))
    k_ref[0] = rope(jnp.dot(x, w_ref[:, D_MODEL:2 * D_MODEL], preferred_element_type=F32))
    v_ref[0] = jnp.dot(x, w_ref[:, 2 * D_MODEL:3 * D_MODEL], preferred_element_type=F32)


def _rope_tables(pos):
    half = DIFF_DH // 2
    inv = ROPE_THETA ** (-jnp.arange(half, dtype=F32) * 2.0 / DIFF_DH)
    ang = pos.astype(F32)[:, None] * inv[None, :]
    cos, sin = jnp.cos(ang), jnp.sin(ang)
    reps = LANES // DIFF_DH
    cos_t = jnp.tile(jnp.concatenate([cos, cos], axis=1), (1, reps))
    sin_t = jnp.tile(jnp.concatenate([-sin, sin], axis=1), (1, reps))
    return cos_t, sin_t


def _qkv_rope(x, w, cos_t, sin_t, *, tm):
    bsz, rows, d = x.shape
    kern = functools.partial(_qkv_kernel, tm=tm)
    spec = pl.BlockSpec((1, tm, d), lambda bi, i: (bi, i, 0))
    tab = pl.BlockSpec((tm, LANES), lambda bi, i: (i, 0))
    shp = jax.ShapeDtypeStruct((bsz, rows, d), F32)
    return pl.pallas_call(
        kern,
        grid=(bsz, rows // tm),
        in_specs=[spec, pl.BlockSpec((d, 3 * d), lambda bi, i: (0, 0)), tab, tab],
        out_specs=[spec, spec, spec],
        out_shape=[shp, shp, shp],
        compiler_params=_params("arbitrary", "arbitrary"),
        name="qkv_rope",
    )(x, w, cos_t, sin_t)


def _diff_lambda(lq_ref, lk_ref, lam_init):
    dots = jnp.sum(lq_ref[...] * lk_ref[...], axis=-1, keepdims=True)
    e = jnp.exp(dots)
    return e[0:1, :] - e[1:2, :] + lam_init


def _head_norm(o, ng, lam_init):
    ms = jnp.mean(o * o, axis=-1, keepdims=True)
    return o * lax.rsqrt(ms + NORM_EPS) * ng * (1.0 - lam_init)


def _attn_kernel(q_ref, k_ref, v_ref, lq_ref, lk_ref, ng_ref, o_ref, *, tq, lam_init):
    qi = pl.program_id(2)
    q = q_ref[0] * (DIFF_DH ** -0.5)
    q1 = q[:, :DIFF_DH].astype(BF16)
    q2 = q[:, DIFF_DH:].astype(BF16)
    ones = jnp.ones((tq, DIFF_VD), BF16)

    def scores(kb):
        start = pl.multiple_of(kb * tq, tq)
        k = k_ref[0, pl.ds(start, tq), :]
        v1 = jnp.concatenate([v_ref[0, pl.ds(start, tq), :].astype(BF16), ones], axis=1)
        s = jnp.concatenate([_bdot_nt(q1, k[:, :DIFF_DH]), _bdot_nt(q2, k[:, DIFF_DH:])], axis=0)
        return s, v1

    def update(s, v1, m, a):
        mn = jnp.maximum(m, jnp.max(s, axis=1, keepdims=True))
        p = jnp.exp(s - mn).astype(BF16)
        a = jnp.exp(m - mn) * a + jnp.dot(p, v1, preferred_element_type=F32)
        return mn, a

    def body(kb, carry):
        s, v1 = scores(kb)
        return update(s, v1, *carry)

    init = (jnp.full((2 * tq, 1), -jnp.inf, F32), jnp.zeros((2 * tq, 2 * DIFF_VD), F32))
    m, a = lax.fori_loop(0, qi, body, init)
    s, v1 = scores(qi)
    row = lax.broadcasted_iota(jnp.int32, (2 * tq, tq), 0) % tq
    col = lax.broadcasted_iota(jnp.int32, (2 * tq, tq), 1)
    m, a = update(jnp.where(col <= row, s, -jnp.inf), v1, m, a)
    lam = _diff_lambda(lq_ref, lk_ref, lam_init)
    o = (a[:tq, :DIFF_VD] / a[:tq, DIFF_VD:DIFF_VD + 1]
         - lam * (a[tq:, :DIFF_VD] / a[tq:, DIFF_VD:DIFF_VD + 1]))
    o_ref[0] = _head_norm(o, ng_ref[...], lam_init).astype(o_ref.dtype)


def _diff_attn_prompt(q, k, v, lq, lk, ng, lam_init, *, tq):
    bsz, seq, d = q.shape
    kern = functools.partial(_attn_kernel, tq=tq, lam_init=lam_init)
    small = lambda *shape: pl.BlockSpec(shape, lambda bi, h, i: (0,) * len(shape))
    return pl.pallas_call(
        kern,
        grid=(bsz, DIFF_HEADS, seq // tq),
        in_specs=[pl.BlockSpec((1, tq, DIFF_VD), lambda bi, h, i: (bi, i, h)),
                  pl.BlockSpec((1, seq, DIFF_VD), lambda bi, h, i: (bi, 0, h)),
                  pl.BlockSpec((1, seq, DIFF_VD), lambda bi, h, i: (bi, 0, h)),
                  small(2, DIFF_DH), small(2, DIFF_DH), small(1, DIFF_VD)],
        out_specs=pl.BlockSpec((1, tq, DIFF_VD), lambda bi, h, i: (bi, i, h)),
        out_shape=jax.ShapeDtypeStruct((bsz, seq, d), BF16),
        compiler_params=_params("arbitrary", "arbitrary", "arbitrary"),
        name="diff_attn_prompt",
    )(q, k, v, lq, lk, ng)


DEC_TPAD = SUBLANES // 2
DEC_ROWS = DIFF_HEADS * 2 * DEC_TPAD


def _decode_kernel(pt_ref, q_ref, *refs, pages_per_step, n_steps, lam_init):
    g = pages_per_step
    k_refs = refs[:g]
    v_refs = refs[g:2 * g]
    knew_ref, vnew_ref, rep_ref, lq_ref, lk_ref, ng_ref, o_ref, m_ref, l_ref, acc_ref = refs[2 * g:]
    s = pl.program_id(1)
    scale = DIFF_DH ** -0.5
    rep = rep_ref[...]
    row_head = lax.broadcasted_iota(jnp.int32, (DEC_ROWS, PAGE_SIZE * DIFF_HEADS), 0) // (2 * DEC_TPAD)
    col_head = lax.broadcasted_iota(jnp.int32, (DEC_ROWS, PAGE_SIZE * DIFF_HEADS), 1) % DIFF_HEADS
    own_head = row_head == col_head

    @pl.when(s == 0)
    def _():
        m_ref[...] = jnp.full(m_ref.shape, -jnp.inf, F32)
        l_ref[...] = jnp.zeros(l_ref.shape, F32)
        acc_ref[...] = jnp.zeros(acc_ref.shape, F32)

    q = (q_ref[0] * scale).astype(BF16)

    def block(ks, vs, mask):
        sc = jnp.concatenate([_bdot(q, k[0]) for k in ks], axis=1)
        if mask is not None:
            sc = jnp.where(mask, sc, -jnp.inf)
        m_old = m_ref[...]
        mn = jnp.maximum(m_old, jnp.max(sc, axis=1, keepdims=True))
        alpha = jnp.exp(m_old - mn)
        p = jnp.exp(sc - mn)
        pv = None
        for gi in range(len(vs)):
            pe = jnp.dot(p[:, gi * PAGE_SIZE:(gi + 1) * PAGE_SIZE].astype(BF16), rep, preferred_element_type=F32)
            part = _bdot(jnp.where(own_head, pe, 0.0), vs[gi][0])
            pv = part if pv is None else pv + part
        l_ref[...] = alpha * l_ref[...] + jnp.sum(p, axis=1, keepdims=True)
        acc_ref[...] = alpha * acc_ref[...] + pv
        m_ref[...] = mn

    block(k_refs, v_refs, None)

    @pl.when(s == n_steps - 1)
    def _():
        t_row = lax.broadcasted_iota(jnp.int32, (DEC_ROWS, PAGE_SIZE), 0) % DEC_TPAD
        key = lax.broadcasted_iota(jnp.int32, (DEC_ROWS, PAGE_SIZE), 1)
        block([knew_ref], [vnew_ref], key <= t_row)
        lam = _diff_lambda(lq_ref, lk_ref, lam_init)
        ng = ng_ref[...]
        for h in range(DIFF_HEADS):
            r0 = h * SUBLANES
            cols = slice(h * DIFF_VD, (h + 1) * DIFF_VD)
            a = acc_ref[r0:r0 + SUBLANES, :] / l_ref[r0:r0 + SUBLANES, :]
            o = a - lam * pltpu.roll(a, DEC_TPAD, 0)
            o_ref[0, :, cols] = _head_norm(o, ng, lam_init).astype(o_ref.dtype)


def _diff_attn_sample(qbd, cache_k, cache_v, page_table, knew, vnew, lq, lk, ng, lam_init,
                      *, pages_per_step, page0):
    bsz = qbd.shape[0]
    n_pages = page_table.shape[1]
    g = pages_per_step
    n_steps = n_pages // g
    kern = functools.partial(_decode_kernel, pages_per_step=g, n_steps=n_steps, lam_init=lam_init)
    rep = (jnp.arange(PAGE_SIZE * DIFF_HEADS)[None, :] // DIFF_HEADS == jnp.arange(PAGE_SIZE)[:, None]).astype(BF16)

    def page_spec(gi):
        return pl.BlockSpec((1, D_MODEL, PAGE_SIZE), lambda b, s, pt: (page0 + pt[b, s * g + gi], 0, 0))

    per_b = lambda rows, cols: pl.BlockSpec((1, rows, cols), lambda b, s, pt: (b, 0, 0))
    small = lambda *shape: pl.BlockSpec(shape, lambda b, s, pt: (0,) * len(shape))
    grid_spec = pltpu.PrefetchScalarGridSpec(
        num_scalar_prefetch=1,
        grid=(bsz, n_steps),
        in_specs=([per_b(DEC_ROWS, D_MODEL)] + [page_spec(gi) for gi in range(g)] + [page_spec(gi) for gi in range(g)]
                  + [per_b(D_MODEL, PAGE_SIZE), per_b(D_MODEL, PAGE_SIZE), small(PAGE_SIZE, PAGE_SIZE * DIFF_HEADS),
                     small(2, DIFF_DH), small(2, DIFF_DH), small(1, DIFF_VD)]),
        out_specs=per_b(SUBLANES, D_MODEL),
        scratch_shapes=[pltpu.VMEM((DEC_ROWS, 1), F32), pltpu.VMEM((DEC_ROWS, 1), F32),
                        pltpu.VMEM((DEC_ROWS, DIFF_VD), F32)],
    )
    return pl.pallas_call(
        kern,
        grid_spec=grid_spec,
        out_shape=jax.ShapeDtypeStruct((bsz, SUBLANES, D_MODEL), BF16),
        compiler_params=_params("arbitrary", "arbitrary"),
        name="diff_attn_sample",
    )(page_table, qbd, *([cache_k] * g), *([cache_v] * g), knew, vnew, rep, lq, lk, ng)


CONV_PAD = SUBLANES


def _ssd_kernel(*refs, q, n_chunks, valid_len, has_h0):
    if has_h0:
        (z_ref, xbc_ref, dt_ref, dtt_ref, cpre_ref, cw_ref, cb_ref, dtb_row_ref, dtb_col_ref,
         alog_row_ref, alog_col_ref, dexp_ref, ng_ref, expand_ref, h0_ref,
         y_ref, hl_ref, s_ref, cext_ref) = refs
    else:
        (z_ref, xbc_ref, dt_ref, dtt_ref, cpre_ref, cw_ref, cb_ref, dtb_row_ref, dtb_col_ref,
         alog_row_ref, alog_col_ref, dexp_ref, ng_ref, expand_ref,
         y_ref, hl_ref, s_ref, cext_ref) = refs
        h0_ref = None
    c = pl.program_id(1)

    @pl.when(c == 0)
    def _():
        if has_h0:
            s_ref[...] = h0_ref[0]
        else:
            s_ref[...] = jnp.zeros(s_ref.shape, F32)
        cext_ref[0:CONV_PAD, :] = cpre_ref[0]

    @pl.when(c > 0)
    def _():
        cext_ref[0:CONV_PAD, :] = cext_ref[q:q + CONV_PAD, :]

    cext_ref[CONV_PAD:CONV_PAD + q, :] = xbc_ref[0]
    xbc = cb_ref[...]
    for kk in range(SSD_CONV):
        r0 = CONV_PAD - (SSD_CONV - 1) + kk
        xbc = xbc + cext_ref[r0:r0 + q, :] * cw_ref[kk:kk + 1, :]
    xbc = _silu(xbc)
    xs = xbc[:, :SSD_DI]
    bm = xbc[:, SSD_DI:SSD_DI + SSD_G * SSD_N]
    cm = xbc[:, SSD_DI + SSD_G * SSD_N:]

    dtc = _softplus(dt_ref[0] + dtb_row_ref[...])
    dtr = _softplus(dtt_ref[0] + dtb_col_ref[...])
    if valid_len is not None:
        t_col = c * q + lax.broadcasted_iota(jnp.int32, (q, LANES), 0)
        t_row = c * q + lax.broadcasted_iota(jnp.int32, (SSD_H, q), 1)
        dtc = jnp.where(t_col < valid_len, dtc, 0.0)
        dtr = jnp.where(t_row < valid_len, dtr, 0.0)
    ii = lax.broadcasted_iota(jnp.int32, (q, q), 0)
    jj = lax.broadcasted_iota(jnp.int32, (q, q), 1)
    causal = jj <= ii
    tri = jnp.where(causal, 1.0, 0.0).astype(BF16)
    tri_t = jnp.where(ii <= jj, 1.0, 0.0).astype(BF16)
    acum_c = _dot_exact_lhs(tri, dtc * (-jnp.exp(alog_row_ref[...])))
    acum_r = _dot_exact_rhs(dtr * (-jnp.exp(alog_col_ref[...])), tri_t)
    total = acum_c[q - 1:q, :]
    expand = expand_ref[...]
    dte_x = _dot_exact_rhs(jnp.exp(total - acum_c) * dtc, expand)
    eac_x = _dot_exact_rhs(jnp.exp(acum_c), expand)
    cdec_x = _dot_exact_rhs(jnp.broadcast_to(jnp.exp(total), (SUBLANES, LANES)), expand)[0:1, :]

    gw = SSD_DI // SSD_G
    hg = SSD_H // SSD_G
    lane = lax.broadcasted_iota(jnp.int32, (q, 2 * SSD_P), 1)
    ys = []
    for g in range(SSD_G):
        gc = slice(g * gw, (g + 1) * gw)
        bg = bm[:, g * SSD_N:(g + 1) * SSD_N]
        cg = cm[:, g * SSD_N:(g + 1) * SSD_N]
        cbm = _bdot_nt(cg, bg)
        xg = xs[:, gc]
        s_old = s_ref[:, gc]
        y_g = _bdot(cg, s_old) * eac_x[:, gc]
        s_ref[:, gc] = s_old * cdec_x[:, gc] + _bdot(bg.T, xg * dte_x[:, gc])
        pairs = []
        for hp in range(hg // 2):
            wts = []
            for hh in range(2):
                h = g * hg + 2 * hp + hh
                seg = acum_c[:, h:h + 1] - acum_r[h:h + 1, :]
                dec = jnp.exp(jnp.where(causal, seg, -jnp.inf))
                wts.append((cbm * dec * dtr[h:h + 1, :]).astype(BF16))
            xp = xg[:, hp * 2 * SSD_P:(hp + 1) * 2 * SSD_P]
            xbd = jnp.concatenate([jnp.where(lane < SSD_P, xp, 0.0), jnp.where(lane >= SSD_P, xp, 0.0)],
                                  axis=0).astype(BF16)
            pairs.append(jnp.dot(jnp.concatenate(wts, axis=1), xbd, preferred_element_type=F32))
        y_g = y_g + jnp.concatenate(pairs, axis=1) + xg * dexp_ref[:, gc]
        zg = z_ref[0, :, gc]
        y_g = y_g * _silu(zg)
        y_g = y_g * lax.rsqrt(jnp.mean(y_g * y_g, axis=-1, keepdims=True) + NORM_EPS)
        ys.append(y_g)
    y_ref[0] = (jnp.concatenate(ys, axis=1) * ng_ref[...]).astype(y_ref.dtype)

    @pl.when(c == n_chunks - 1)
    def _():
        hl_ref[0] = s_ref[...]


def _ssd_scan(z, xbc, dt, dtt, cpre, conv_w, conv_b, dtb_row, dtb_col, alog_row, alog_col, d_exp, ng,
              expand, h0t, *, valid_len):
    bsz, seq, _ = z.shape
    q = SSD_CHUNK
    n_chunks = seq // q
    has_h0 = h0t is not None
    kern = functools.partial(_ssd_kernel, q=q, n_chunks=n_chunks, valid_len=valid_len, has_h0=has_h0)
    full = lambda *shape: pl.BlockSpec(shape, lambda b, c: (0,) * len(shape))
    in_specs = [
        pl.BlockSpec((1, q, SSD_DI), lambda b, c: (b, c, 0)),
        pl.BlockSpec((1, q, SSD_CONV_DIM), lambda b, c: (b, c, 0)),
        pl.BlockSpec((1, q, LANES), lambda b, c: (b, c, 0)),
        pl.BlockSpec((1, SSD_H, q), lambda b, c: (b, 0, c)),
        pl.BlockSpec((1, CONV_PAD, SSD_CONV_DIM), lambda b, c: (b, 0, 0)),
        full(SSD_CONV, SSD_CONV_DIM), full(1, SSD_CONV_DIM),
        full(1, LANES), full(SSD_H, 1), full(1, LANES), full(SSD_H, 1),
        full(1, SSD_DI), full(1, SSD_DI), full(LANES, SSD_DI),
    ]
    args = [z, xbc, dt, dtt, cpre, conv_w, conv_b, dtb_row, dtb_col, alog_row, alog_col, d_exp, ng, expand]
    if has_h0:
        in_specs.append(pl.BlockSpec((1, SSD_N, SSD_DI), lambda b, c: (b, 0, 0)))
        args.append(h0t)
    return pl.pallas_call(
        kern,
        grid=(bsz, n_chunks),
        in_specs=in_specs,
        out_specs=[pl.BlockSpec((1, q, SSD_DI), lambda b, c: (b, c, 0)),
                   pl.BlockSpec((1, SSD_N, SSD_DI), lambda b, c: (b, 0, 0))],
        out_shape=[jax.ShapeDtypeStruct((bsz, seq, SSD_DI), BF16),
                   jax.ShapeDtypeStruct((bsz, SSD_N, SSD_DI), F32)],
        scratch_shapes=[pltpu.VMEM((SSD_N, SSD_DI), F32), pltpu.VMEM((CONV_PAD + q, SSD_CONV_DIM), F32)],
        compiler_params=_params("arbitrary", "arbitrary"),
        name="ssd_scan",
    )(*args)


def _first_max(vals):
    m = vals[0]
    for v in vals[1:]:
        m = jnp.maximum(m, v)
    flags = []
    taken = jnp.zeros_like(m)
    for v in vals:
        f = jnp.where(v == m, 1.0 - taken, 0.0)
        taken = taken + f
        flags.append(f)
    return m, flags


def _sum_rows(rows):
    tot = rows[0]
    for r in rows[1:]:
        tot = tot + r
    return tot


def _router_rows(x, wh_ref, wl_ref, b_ref):
    xh = x.astype(BF16)
    xl = (x - xh.astype(F32)).astype(BF16)
    logits = (jnp.dot(xh, wh_ref[...], preferred_element_type=F32)
              + jnp.dot(xh, wl_ref[...], preferred_element_type=F32)
              + jnp.dot(xl, wh_ref[...], preferred_element_type=F32))
    lt = logits.T
    rows = [lt[e:e + 1, :] for e in range(N_EXPERTS)]
    mx = rows[0]
    for r in rows[1:]:
        mx = jnp.maximum(mx, r)
    ex = [jnp.exp(r - mx) for r in rows]
    den = _sum_rows(ex)
    probs = [r / den for r in ex]
    sel = [probs[e] + b_ref[e:e + 1, :] for e in range(N_EXPERTS)]
    scores, chosen = [], []
    for g in range(N_EGROUPS):
        v = sel[g * EXP_PER_GROUP:(g + 1) * EXP_PER_GROUP]
        m1, f1 = _first_max(v)
        rest = [jnp.where(f > 0.0, -jnp.inf, vi) for f, vi in zip(f1, v)]
        m2, f2 = _first_max(rest)
        scores.append(m1 + m2)
        chosen.extend([a + b for a, b in zip(f1, f2)])
    return probs, chosen, scores


def _router_kernel(x_ref, wh_ref, wl_ref, b_ref, o_ref):
    probs, chosen, scores = _router_rows(x_ref[...], wh_ref, wl_ref, b_ref)
    _, gflag = _first_max(scores)
    gates = [probs[e] * chosen[e] * gflag[e // EXP_PER_GROUP] for e in range(N_EXPERTS)]
    tot = _sum_rows(gates)
    for e in range(N_EXPERTS):
        o_ref[e:e + 1, :] = gates[e] / tot


def _route_group_kernel(x_ref, wh_ref, wl_ref, b_ref, grp_ref, gate_ref, xbf_ref, *, tm):
    xbf_ref[...] = x_ref[...].astype(BF16)
    probs, chosen, scores = _router_rows(x_ref[...], wh_ref, wl_ref, b_ref)
    _, gflag = _first_max(scores)
    grp = _sum_rows([gflag[g] * float(g) for g in range(1, N_EGROUPS)])
    grp_ref[...] = grp.astype(jnp.int32)
    local = [_sum_rows([probs[g * EXP_PER_GROUP + le] * chosen[g * EXP_PER_GROUP + le] * gflag[g]
                        for g in range(N_EGROUPS)]) for le in range(EXP_PER_GROUP)]
    tot = _sum_rows(local)
    rows = jnp.concatenate([r / tot for r in local] + [jnp.zeros((LANES - EXP_PER_GROUP, tm), F32)], axis=0)
    gate_ref[...] = rows.T


def _route_group(x, wr_hi, wr_lo, b_col, *, tm):
    t, d = x.shape
    return pl.pallas_call(
        functools.partial(_route_group_kernel, tm=tm),
        grid=(t // tm,),
        in_specs=[pl.BlockSpec((tm, d), lambda i: (i, 0)),
                  pl.BlockSpec((d, LANES), lambda i: (0, 0)),
                  pl.BlockSpec((d, LANES), lambda i: (0, 0)),
                  pl.BlockSpec((N_EXPERTS, 1), lambda i: (0, 0))],
        out_specs=[pl.BlockSpec((1, tm), lambda i: (0, i)), pl.BlockSpec((tm, LANES), lambda i: (i, 0)),
                   pl.BlockSpec((tm, d), lambda i: (i, 0))],
        out_shape=[jax.ShapeDtypeStruct((1, t), jnp.int32), jax.ShapeDtypeStruct((t, LANES), F32),
                   jax.ShapeDtypeStruct((t, d), BF16)],
        compiler_params=_params("arbitrary"),
        name="route_group",
    )(x, wr_hi, wr_lo, b_col)


def _moe_sorted_kernel(sw_ref, sslot_ref, se_ref, ns_ref, x_ref, pos_ref, gate_ref, wg_ref, wu_ref, wd_ref, o_ref,
                       g3_ref, xt_ref, gcol_ref, acc_ref, *, tr, steps):
    sb = pl.program_id(0)
    s = pl.program_id(1)
    valid = s < ns_ref[sb]
    slot = sslot_ref[sb * steps + s]
    e = se_ref[sb * steps + s]

    @pl.when(s == 0)
    def _():
        hi, mid, lo = _split3(gate_ref[...])
        g3_ref[0] = hi
        g3_ref[1] = mid
        g3_ref[2] = lo
        o_ref[...] = jnp.zeros(o_ref.shape, o_ref.dtype)

    @pl.when(jnp.logical_and(valid, e == 0))
    def _():
        n_src = x_ref.shape[0]
        row = slot * tr + lax.broadcasted_iota(jnp.int32, (tr, n_src), 0)
        pick = jnp.where(pos_ref[0] == row, 1.0, 0.0).astype(BF16)
        xt_ref[slot] = jnp.dot(pick, x_ref[...], preferred_element_type=F32).astype(BF16)
        gcol_ref[slot] = (jnp.dot(pick, g3_ref[0], preferred_element_type=F32)
                          + jnp.dot(pick, g3_ref[1], preferred_element_type=F32)
                          + jnp.dot(pick, g3_ref[2], preferred_element_type=F32))
        acc_ref[slot] = jnp.zeros(acc_ref.shape[1:], F32)

    @pl.when(valid)
    def _():
        xb = xt_ref[slot]
        gates = gcol_ref[slot]
        lane = lax.broadcasted_iota(jnp.int32, gates.shape, 1)
        gcol = jnp.sum(jnp.where(lane == e, gates, 0.0), axis=1, keepdims=True)
        hgate = jnp.dot(xb, wg_ref[0], preferred_element_type=F32)
        hup = jnp.dot(xb, wu_ref[0], preferred_element_type=F32)
        acc_ref[slot] += _bdot(_silu(hgate) * hup * gcol, wd_ref[0])

    @pl.when(jnp.logical_and(valid, e == EXP_PER_GROUP - 1))
    def _():
        o_ref[pl.ds(pl.multiple_of(slot * tr, tr), tr), :] = acc_ref[slot].astype(o_ref.dtype)


def _moe_sorted(x_bf, pos_row, gate4, plan, w_gate, w_up, w_down, *, sb_rows, tr):
    t, d = x_bf.shape
    f = w_gate.shape[2]
    n_sb = t // sb_rows
    cap = sb_rows + N_EGROUPS * tr
    slots = cap // tr
    steps = slots * EXP_PER_GROUP
    w_index = lambda sb, s, sw, sslot, se, ns: (sw[sb * steps + s], 0, 0)
    per_sb = lambda *shape: pl.BlockSpec(shape, lambda sb, s, sw, sslot, se, ns: (sb,) + (0,) * (len(shape) - 1))
    grid_spec = pltpu.PrefetchScalarGridSpec(
        num_scalar_prefetch=4,
        grid=(n_sb, steps),
        in_specs=[per_sb(sb_rows, d), per_sb(1, 1, sb_rows), per_sb(sb_rows, LANES),
                  pl.BlockSpec((1, d, f), w_index), pl.BlockSpec((1, d, f), w_index),
                  pl.BlockSpec((1, f, d), w_index)],
        out_specs=per_sb(cap, d),
        scratch_shapes=[pltpu.VMEM((3, sb_rows, LANES), BF16), pltpu.VMEM((slots, tr, d), BF16),
                        pltpu.VMEM((slots, tr, LANES), F32), pltpu.VMEM((slots, tr, d), F32)],
    )
    return pl.pallas_call(
        functools.partial(_moe_sorted_kernel, tr=tr, steps=steps),
        grid_spec=grid_spec,
        out_shape=jax.ShapeDtypeStruct((n_sb * cap, d), BF16),
        compiler_params=_params("arbitrary", "arbitrary"),
        name="moe_sorted_experts",
    )(*plan, x_bf, pos_row, gate4, w_gate, w_up, w_down)


def _dispatch_plan(grp, sb_rows, tr):
    n_sb = grp.shape[0] // sb_rows
    slots = (sb_rows + N_EGROUPS * tr) // tr
    steps = slots * EXP_PER_GROUP
    i32 = jnp.int32
    oh = (grp.reshape(n_sb, sb_rows, 1) == jnp.arange(N_EGROUPS, dtype=i32)).astype(i32)
    csum = jnp.cumsum(oh, axis=1)
    padded = ((csum[:, -1] + tr - 1) // tr) * tr
    gend = jnp.cumsum(padded, axis=1)
    pos = jnp.sum(oh * ((gend - padded)[:, None, :] + csum - 1), axis=2)
    slot_row0 = jnp.arange(slots, dtype=i32) * tr
    tile_group = jnp.minimum(jnp.sum((slot_row0[None, :, None] >= gend[:, None, :]).astype(i32), axis=2),
                             N_EGROUPS - 1)
    n_used = gend[:, -1] // tr
    pair_slot = jnp.repeat(jnp.arange(slots, dtype=i32), EXP_PER_GROUP)[None, :]
    pair_e = jnp.tile(jnp.arange(EXP_PER_GROUP, dtype=i32), slots)[None, :]
    pair_w = (jnp.take_along_axis(tile_group, jnp.broadcast_to(pair_slot, (n_sb, steps)), axis=1) * EXP_PER_GROUP
              + pair_e)
    key = jnp.where(pair_slot < n_used[:, None], pair_w * slots + pair_slot, N_EXPERTS * slots + pair_slot)
    order = jnp.argsort(key, axis=1).astype(i32)
    n_steps = n_used * EXP_PER_GROUP
    step_w = jnp.take_along_axis(pair_w, order, axis=1)
    last_w = jnp.take_along_axis(step_w, jnp.maximum(n_steps - 1, 0)[:, None], axis=1)
    step_w = jnp.where(jnp.arange(steps, dtype=i32)[None, :] < n_steps[:, None], step_w, last_w)
    plan = (step_w.reshape(-1).astype(i32), (order // EXP_PER_GROUP).reshape(-1), (order % EXP_PER_GROUP).reshape(-1),
            n_steps.astype(i32))
    return pos.astype(i32), plan


def _router(x, wr_hi, wr_lo, b_col, *, tm):
    t, d = x.shape
    return pl.pallas_call(
        _router_kernel,
        grid=(t // tm,),
        in_specs=[pl.BlockSpec((tm, d), lambda i: (i, 0)),
                  pl.BlockSpec((d, LANES), lambda i: (0, 0)),
                  pl.BlockSpec((d, LANES), lambda i: (0, 0)),
                  pl.BlockSpec((N_EXPERTS, 1), lambda i: (0, 0))],
        out_specs=pl.BlockSpec((N_EXPERTS, tm), lambda i: (0, i)),
        out_shape=jax.ShapeDtypeStruct((N_EXPERTS, t), F32),
        compiler_params=_params("arbitrary"),
        name="router",
    )(x, wr_hi, wr_lo, b_col)


def _moe_kernel(x_ref, gate_ref, wg_ref, wu_ref, wd_ref, o_ref, xbf_ref):
    e = pl.program_id(1)

    @pl.when(e == 0)
    def _():
        xbf_ref[...] = x_ref[...].astype(BF16)
        o_ref[...] = jnp.zeros(o_ref.shape, F32)

    xb = xbf_ref[...]
    gate = gate_ref[...]
    lane = lax.broadcasted_iota(jnp.int32, gate.shape, 1)
    gcol = jnp.sum(jnp.where(lane == e, gate, 0.0), axis=1, keepdims=True)
    hgate = jnp.dot(xb, wg_ref[0, 0].astype(BF16), preferred_element_type=F32)
    hup = jnp.dot(xb, wu_ref[0, 0].astype(BF16), preferred_element_type=F32)
    o_ref[...] += _bdot(_silu(hgate) * hup * gcol, wd_ref[0, 0])


def _moe(x, gate, w_gate, w_up, w_down, *, layer, tm):
    t, d = x.shape
    f = w_gate.shape[3]
    return pl.pallas_call(
        _moe_kernel,
        grid=(t // tm, N_EXPERTS),
        in_specs=[pl.BlockSpec((tm, d), lambda i, e: (i, 0)),
                  pl.BlockSpec((tm, N_EXPERTS), lambda i, e: (i, 0)),
                  pl.BlockSpec((1, 1, d, f), lambda i, e: (layer, e, 0, 0)),
                  pl.BlockSpec((1, 1, d, f), lambda i, e: (layer, e, 0, 0)),
                  pl.BlockSpec((1, 1, f, d), lambda i, e: (layer, e, 0, 0))],
        out_specs=pl.BlockSpec((tm, d), lambda i, e: (i, 0)),
        out_shape=jax.ShapeDtypeStruct((t, d), F32),
        scratch_shapes=[pltpu.VMEM((tm, d), BF16)],
        compiler_params=_params("arbitrary", "arbitrary"),
        name="moe_experts",
    )(x, gate, w_gate, w_up, w_down)


def _ln_ple_body(x1, ffn, p_ref, g_ref, b_ref, wg_ref, bg_ref, wp_ref, o_ref):
    x = _ln(DN_ALPHA * x1 + ffn, g_ref[...], b_ref[...])
    gate = jax.nn.sigmoid(_bdot(x, wg_ref[...]) + bg_ref[...])
    o_ref[...] = x + gate * _bdot(p_ref[...], wp_ref[...])


def _ln_ple_kernel(x_ref, f_ref, p_ref, *rest):
    _ln_ple_body(x_ref[...], f_ref[...], p_ref, *rest)


def _unsort_ln_ple_kernel(y_ref, pos_ref, x_ref, p_ref, *rest):
    tm, cap = x_ref.shape[0], y_ref.shape[0]
    col = lax.broadcasted_iota(jnp.int32, (tm, cap), 1)
    pick = jnp.where(pos_ref[...] == col, 1.0, 0.0).astype(BF16)
    ffn = jnp.dot(pick, y_ref[...], preferred_element_type=F32)
    _ln_ple_body(x_ref[...], ffn, p_ref.at[0], *rest)


def _unsort_ln_ple(y_sorted, pos_col, x, p_all, g, b, w_gate, b_gate, w_p, *, layer, sb_rows, tm):
    t, d = x.shape
    pd = p_all.shape[2]
    n_sb = t // sb_rows
    n_sub = sb_rows // tm
    cap = y_sorted.shape[0] // n_sb
    rows = lambda cols: pl.BlockSpec((tm, cols), lambda sb, i: (sb * n_sub + i, 0))
    fixed = lambda r, c: pl.BlockSpec((r, c), lambda sb, i: (0, 0))
    return pl.pallas_call(
        _unsort_ln_ple_kernel,
        grid=(n_sb, n_sub),
        in_specs=[pl.BlockSpec((cap, d), lambda sb, i: (sb, 0)), rows(1), rows(d),
                  pl.BlockSpec((1, tm, pd), lambda sb, i: (layer, sb * n_sub + i, 0)),
                  fixed(1, d), fixed(1, d), fixed(d, d), fixed(1, d), fixed(pd, d)],
        out_specs=rows(d),
        out_shape=jax.ShapeDtypeStruct((t, d), F32),
        compiler_params=_params("arbitrary", "arbitrary"),
        name="unsort_ln_ple",
    )(y_sorted, pos_col, x, p_all, g, b, w_gate, b_gate, w_p)


def _ln_ple(x, ffn, p, g, b, w_gate, b_gate, w_p, *, tm):
    t, d = x.shape
    pd = p.shape[1]
    rows = lambda cols: pl.BlockSpec((tm, cols), lambda i: (i, 0))
    fixed = lambda r, c: pl.BlockSpec((r, c), lambda i: (0, 0))
    return pl.pallas_call(
        _ln_ple_kernel,
        grid=(t // tm,),
        in_specs=[rows(d), rows(d), rows(pd), fixed(1, d), fixed(1, d), fixed(d, d), fixed(1, d), fixed(pd, d)],
        out_specs=rows(d),
        out_shape=jax.ShapeDtypeStruct((t, d), F32),
        compiler_params=_params("arbitrary"),
        name="ln_ple",
    )(x, ffn, p, g, b, w_gate, b_gate, w_p)


def _row_tile(rows, want):
    tm = min(rows, want)
    assert rows % tm == 0
    return tm


POOL_TM = 512
MM_TM = 512
QKV_TM = 512
ATTN_TQ = 512
MOE_TM = 1024
MOE_SORT_BLOCK = 2048
MOE_SORT_TR = 256
ROUTER_TM = 1024
PLE_TM = 512
DECODE_PAGES_PER_STEP = 16


def kernel(x_prompt, x_sample, p_prompt, p_sample, state_pool, cache_k, cache_v, page_table, state_ssm, state_conv,
           ln_g, ln_b, w_pool_in, w_pool_grp, pool_scale, w_pool_out, w_diff_in, diff_lambda_q, diff_lambda_k,
           diff_norm_g, w_diff_out, w_ssd_in, ssd_conv_w, ssd_conv_b, ssd_dt_bias, ssd_a_log, ssd_d, ssd_norm_g,
           w_ssd_out, w_router, b_router, w_exp_gate, w_exp_up, w_exp_down, w_ple, w_ple_gate, b_ple_gate):
    bp, seq, d = x_prompt.shape
    bs, tdec, _ = x_sample.shape
    n_pages = page_table.shape[1]
    past = n_pages * cache_k.shape[2]
    tp = bp * seq
    ts = bs * tdec

    def to_tm(a):
        return jnp.swapaxes(a, 0, 1).reshape((ts,) + a.shape[2:])

    def from_tm(a):
        return jnp.swapaxes(a.reshape((tdec, bs) + a.shape[1:]), 0, 1)

    row = lambda v: v.reshape(1, -1)

    wr = jnp.pad(w_router, ((0, 0), (0, LANES - N_EXPERTS)))
    wr_hi = wr.astype(BF16)
    wr_lo = (wr - wr_hi.astype(F32)).astype(BF16)
    br_col = b_router.reshape(N_EXPERTS, 1)

    p_prompt_rows = p_prompt.reshape(DEPTH, tp, PLE_DIM)

    def finish(x1, i, sparse):
        t = x1.shape[0]
        tail = (row(ln_g[i, 1]), row(ln_b[i, 1]), w_ple_gate[i].astype(BF16), row(b_ple_gate[i]),
                w_ple[i].astype(BF16))
        if sparse:
            sb_rows, tr = _row_tile(t, MOE_SORT_BLOCK), MOE_SORT_TR
            grp, gate4, x1_bf = _route_group(x1, wr_hi, wr_lo, br_col, tm=_row_tile(t, ROUTER_TM))
            pos, plan = _dispatch_plan(grp[0], sb_rows, tr)
            y_sorted = _moe_sorted(x1_bf, pos.reshape(-1, 1, sb_rows), gate4, plan,
                                   w_exp_gate[i].astype(BF16), w_exp_up[i].astype(BF16),
                                   w_exp_down[i].astype(BF16), sb_rows=sb_rows, tr=tr)
            return _unsort_ln_ple(y_sorted, pos.reshape(t, 1), x1, p_prompt_rows, *tail, layer=i,
                                  sb_rows=sb_rows, tm=_row_tile(sb_rows, PLE_TM))
        gate = _router(x1, wr_hi, wr_lo, br_col, tm=_row_tile(t, ROUTER_TM)).T
        ffn = _moe(x1, gate, w_exp_gate, w_exp_up, w_exp_down, layer=i, tm=_row_tile(t, MOE_TM))
        return _ln_ple(x1, ffn, to_tm(p_sample[i]), *tail, tm=_row_tile(t, PLE_TM))

    xp = x_prompt.reshape(tp, d)
    xs = to_tm(x_sample)
    pos_p = jnp.arange(seq, dtype=jnp.int32)
    pos_s = jnp.repeat(past + jnp.arange(tdec, dtype=jnp.int32), bs)
    outs = {k: [] for k in ("pool_p", "pool_s", "k_p", "v_p", "k_s", "v_s", "ssm_p", "conv_p", "ssm_s", "conv_s")}

    for i in range(DEPTH):
        kind, j = i % N_MIXERS, i // N_MIXERS
        g1, b1 = row(ln_g[i, 0]), row(ln_b[i, 0])
        if kind == 0:
            wts = (w_pool_in[j].astype(BF16), w_pool_grp[j].astype(BF16), row(pool_scale[j]),
                   w_pool_out[j].astype(BF16), g1, b1)
            x1p, newp = _pool_layer(xp.reshape(bp, seq, d), jnp.zeros((bp, POOL_BUF + 1, d), F32), *wts,
                                    tm=_row_tile(seq, POOL_TM), stride=1, pos0=0)
            pre_s = jnp.swapaxes(state_pool[j], 0, 1).reshape(1, POOL_BUF * bs, d)
            pre_s = jnp.pad(pre_s, ((0, 0), (bs, 0), (0, 0)))
            x1s, news = _pool_layer(xs.reshape(1, ts, d), pre_s, *wts, tm=ts, stride=bs, pos0=past)
            outs["pool_p"].append(newp)
            outs["pool_s"].append(jnp.swapaxes(news.reshape(POOL_BUF, bs, d), 0, 1))
            x1p = x1p.reshape(tp, d)
            x1s = x1s.reshape(ts, d)
        elif kind == 1:
            lam_init = 0.8 - 0.6 * math.exp(-0.3 * i)
            w_in = w_diff_in[j].astype(BF16)
            w_out = w_diff_out[j].astype(BF16)
            lq, lk, ng = diff_lambda_q[j], diff_lambda_k[j], row(diff_norm_g[j])
            cos_p, sin_p = _rope_tables(pos_p)
            q, k, v = _qkv_rope(xp.reshape(bp, seq, d), w_in, cos_p, sin_p, tm=_row_tile(seq, QKV_TM))
            o = _diff_attn_prompt(q, k, v, lq, lk, ng, lam_init, tq=_row_tile(seq, ATTN_TQ))
            x1p = _mm_res_ln(o.reshape(tp, d), w_out, xp, g1, b1, tm=_row_tile(tp, MM_TM))
            outs["k_p"].append(k.reshape(bp, seq, DIFF_HEADS, 2, DIFF_DH))
            outs["v_p"].append(v.reshape(bp, seq, DIFF_HEADS, DIFF_VD))

            cos_s, sin_s = _rope_tables(pos_s)
            q, k, v = _qkv_rope(xs.reshape(1, ts, d), w_in, cos_s, sin_s, tm=ts)
            q, k, v = from_tm(q[0]), from_tm(k[0]), from_tm(v[0])
            q5 = q.reshape(bs, tdec, DIFF_HEADS * 2, DIFF_DH)
            eye = jnp.eye(DIFF_HEADS * 2, dtype=F32)
            qbd = jnp.einsum('bthd,hg->bhtgd', q5, eye)
            qbd = jnp.pad(qbd, ((0, 0), (0, 0), (0, DEC_TPAD - tdec), (0, 0), (0, 0)))
            qbd = qbd.reshape(bs, DEC_ROWS, d)
            kt_pages = jnp.transpose(cache_k, (0, 1, 3, 4, 5, 2)).reshape(-1, d, PAGE_SIZE)
            v_pages = cache_v.reshape(-1, PAGE_SIZE * DIFF_HEADS, DIFF_VD)
            knew = jnp.pad(jnp.swapaxes(k, 1, 2), ((0, 0), (0, 0), (0, PAGE_SIZE - tdec)))
            vnew = jnp.pad(v, ((0, 0), (0, PAGE_SIZE - tdec), (0, 0))).reshape(bs, PAGE_SIZE * DIFF_HEADS, DIFF_VD)
            o = _diff_attn_sample(qbd, kt_pages, v_pages, page_table, knew, vnew, lq, lk, ng, lam_init,
                                  pages_per_step=DECODE_PAGES_PER_STEP, page0=j * cache_k.shape[1])
            o = to_tm(o[:, :tdec])
            x1s = _mm_res_ln(o, w_out, xs, g1, b1, tm=ts)
            outs["k_s"].append(k.reshape(bs, tdec, DIFF_HEADS, 2, DIFF_DH))
            outs["v_s"].append(v.reshape(bs, tdec, DIFF_HEADS, DIFF_VD))
        else:
            w_in = w_ssd_in[j]
            w_z = w_in[:, :SSD_DI].astype(BF16)
            w_xbc = w_in[:, SSD_DI:SSD_DI + SSD_CONV_DIM].astype(BF16)
            w_dt = jnp.pad(w_in[:, SSD_DI + SSD_CONV_DIM:], ((0, 0), (0, LANES - SSD_H))).astype(BF16)
            w_out = w_ssd_out[j].astype(BF16)
            dtb_row = jnp.pad(ssd_dt_bias[j], (0, LANES - SSD_H)).reshape(1, LANES)
            alog_row = jnp.pad(ssd_a_log[j], (0, LANES - SSD_H)).reshape(1, LANES)
            dtb_col = ssd_dt_bias[j].reshape(SSD_H, 1)
            alog_col = ssd_a_log[j].reshape(SSD_H, 1)
            d_exp = jnp.repeat(ssd_d[j], SSD_P).reshape(1, SSD_DI)
            ng = row(ssd_norm_g[j])
            expand = (jnp.arange(SSD_DI)[None, :] // SSD_P == jnp.arange(LANES)[:, None]).astype(BF16)
            shared = (ssd_conv_w[j], row(ssd_conv_b[j]), dtb_row, dtb_col, alog_row, alog_col, d_exp, ng, expand)

            def in_proj(x, tm):
                z = _matmul(x, w_z, tm=tm, tn=1024)
                xbc = _matmul(x, w_xbc, tm=tm, tn=1024)
                dt = _matmul(x, w_dt, tm=tm, tn=LANES)
                return z, xbc, dt

            def hlast_to_state(hl, b):
                return jnp.transpose(hl.reshape(b, SSD_N, SSD_H, SSD_P), (0, 2, 3, 1))

            z, xbc, dt = in_proj(xp, _row_tile(tp, MM_TM))
            dtt = jnp.swapaxes(dt.reshape(bp, seq, LANES)[:, :, :SSD_H], 1, 2)
            y, hl = _ssd_scan(z.reshape(bp, seq, SSD_DI), xbc.reshape(bp, seq, SSD_CONV_DIM),
                              dt.reshape(bp, seq, LANES), dtt, jnp.zeros((bp, CONV_PAD, SSD_CONV_DIM), F32),
                              *shared, None, valid_len=None)
            x1p = _mm_res_ln(y.reshape(tp, SSD_DI), w_out, xp, g1, b1, tm=_row_tile(tp, MM_TM))
            outs["ssm_p"].append(hlast_to_state(hl, bp))
            outs["conv_p"].append(xbc.reshape(bp, seq, SSD_CONV_DIM)[:, seq - (SSD_CONV - 1):])

            z, xbc, dt = in_proj(xs, ts)
            z, xbc, dt = from_tm(z), from_tm(xbc), from_tm(dt)
            padt = ((0, 0), (0, SSD_CHUNK - tdec), (0, 0))
            dtp = jnp.pad(dt, padt)
            cpre = jnp.pad(state_conv[j], ((0, 0), (CONV_PAD - (SSD_CONV - 1), 0), (0, 0)))
            h0t = jnp.transpose(state_ssm[j], (0, 3, 1, 2)).reshape(bs, SSD_N, SSD_DI)
            y, hl = _ssd_scan(jnp.pad(z, padt), jnp.pad(xbc, padt), dtp, jnp.swapaxes(dtp[:, :, :SSD_H], 1, 2),
                              cpre, *shared, h0t, valid_len=tdec)
            x1s = _mm_res_ln(to_tm(y[:, :tdec]), w_out, xs, g1, b1, tm=ts)
            outs["ssm_s"].append(hlast_to_state(hl, bs))
            ext = jnp.concatenate([state_conv[j], xbc], axis=1)
            outs["conv_s"].append(ext[:, -(SSD_CONV - 1):])

        xp = finish(x1p, i, sparse=True)
        xs = finish(x1s, i, sparse=False)

    st = lambda name: jnp.stack(outs[name])
    return (xp.reshape(bp, seq, d), from_tm(xs), st("pool_p"), st("pool_s"), st("k_p"), st("v_p"), st("k_s"),
            st("v_s"), st("ssm_p"), st("conv_p"), st("ssm_s"), st("conv_s"))
```

```python
import functools
import math

import jax
import jax.numpy as jnp
from jax import lax
from jax.experimental import pallas as pl
from jax.experimental.pallas import tpu as pltpu

F32 = jnp.float32
BF16 = jnp.bfloat16

D_MODEL = 1024
DEPTH = 4
N_MIXERS = 3
PAGE_SIZE = 128

POOL_WINDOWS = (2, 4, 8, 16)
POOL_GD = D_MODEL // len(POOL_WINDOWS)
POOL_BUF = max(POOL_WINDOWS) - 1

DIFF_DH = 64
DIFF_VD = 2 * DIFF_DH
DIFF_HEADS = D_MODEL // DIFF_VD
ROPE_THETA = 10000.0

SSD_DI = 2 * D_MODEL
SSD_P = 64
SSD_H = SSD_DI // SSD_P
SSD_G = 4
SSD_N = 128
SSD_CONV = 4
SSD_CONV_DIM = SSD_DI + 2 * SSD_G * SSD_N
SSD_CHUNK = 128

N_EXPERTS = 16
N_EGROUPS = 4
EXP_PER_GROUP = N_EXPERTS // N_EGROUPS
D_FF_EXPERT = D_MODEL // 2

PLE_DIM = 256
DN_ALPHA = (2 * DEPTH) ** 0.25
NORM_EPS = 1e-5

LANES = 128
SUBLANES = 8
VMEM_LIMIT = 56 * 1024 * 1024


def _params(*sem):
    return pltpu.CompilerParams(dimension_semantics=sem, vmem_limit_bytes=VMEM_LIMIT)


def _bdot(a, b):
    return jnp.dot(a.astype(BF16), b.astype(BF16), preferred_element_type=F32)


def _bdot_nt(a, b):
    return lax.dot_general(a.astype(BF16), b.astype(BF16), (((1,), (1,)), ((), ())),
                           preferred_element_type=F32)


def _split3(x):
    hi = x.astype(BF16)
    r = x - hi.astype(F32)
    mid = r.astype(BF16)
    lo = (r - mid.astype(F32)).astype(BF16)
    return hi, mid, lo


def _dot_exact_rhs(x, m):
    hi, mid, lo = _split3(x)
    return (jnp.dot(hi, m, preferred_element_type=F32) + jnp.dot(mid, m, preferred_element_type=F32)
            + jnp.dot(lo, m, preferred_element_type=F32))


def _dot_exact_lhs(m, x):
    hi, mid, lo = _split3(x)
    return (jnp.dot(m, hi, preferred_element_type=F32) + jnp.dot(m, mid, preferred_element_type=F32)
            + jnp.dot(m, lo, preferred_element_type=F32))


def _ln(v, g, b):
    mu = jnp.mean(v, axis=-1, keepdims=True)
    c = v - mu
    var = jnp.mean(c * c, axis=-1, keepdims=True)
    return c * lax.rsqrt(var + NORM_EPS) * g + b


def _silu(v):
    return v * jax.nn.sigmoid(v)


def _softplus(v):
    return jnp.maximum(v, 0.0) + jnp.log(1.0 + jnp.exp(-jnp.abs(v)))


def _pool_kernel(x_ref, pre_ref, win_ref, wgrp_ref, scale_ref, wout_ref, g_ref, b_ref,
                 y_ref, newp_ref, ext_ref, *, tm, stride, pos0, n_tiles):
    i = pl.program_id(1)
    pre_rows = (POOL_BUF + 1) * stride

    @pl.when(i == 0)
    def _():
        ext_ref[0:pre_rows, :] = pre_ref[0]

    @pl.when(i > 0)
    def _():
        ext_ref[0:pre_rows, :] = ext_ref[tm:tm + pre_rows, :]

    x = x_ref[0]
    u = _bdot(x, win_ref[...])
    ext_ref[pre_rows:pre_rows + tm, :] = u
    row = lax.broadcasted_iota(jnp.int32, (tm, 1), 0)
    pos = pos0 + (i * tm + row) // stride
    mixed = []
    for gi, w in enumerate(POOL_WINDOWS):
        c0 = gi * POOL_GD
        ug = u[:, c0:c0 + POOL_GD]
        s = ug
        for k in range(1, w):
            r0 = pre_rows - k * stride
            s = s + ext_ref[r0:r0 + tm, c0:c0 + POOL_GD]
        cnt = jnp.minimum(w, pos + 1).astype(F32)
        pooled = s / cnt - ug
        mixed.append(_bdot(pooled, wgrp_ref[gi]))
    mixed = jnp.concatenate(mixed, axis=1) * scale_ref[...]
    mix = _bdot(mixed, wout_ref[...])
    y_ref[0] = _ln(DN_ALPHA * x + mix, g_ref[...], b_ref[...])

    @pl.when(i == n_tiles - 1)
    def _():
        newp_ref[0] = ext_ref[tm + stride:tm + pre_rows, :]


def _pool_layer(x, prefix, w_in, w_grp, scale, w_out, g, b, *, tm, stride, pos0):
    bsz, rows, d = x.shape
    n_tiles = rows // tm
    pre_rows = (POOL_BUF + 1) * stride
    kern = functools.partial(_pool_kernel, tm=tm, stride=stride, pos0=pos0, n_tiles=n_tiles)
    full = lambda *shape: pl.BlockSpec(shape, lambda bi, i: (0,) * len(shape))
    return pl.pallas_call(
        kern,
        grid=(bsz, n_tiles),
        in_specs=[
            pl.BlockSpec((1, tm, d), lambda bi, i: (bi, i, 0)),
            pl.BlockSpec((1, pre_rows, d), lambda bi, i: (bi, 0, 0)),
            full(d, d), full(len(POOL_WINDOWS), POOL_GD, POOL_GD), full(1, d), full(d, d),
            full(1, d), full(1, d),
        ],
        out_specs=[
            pl.BlockSpec((1, tm, d), lambda bi, i: (bi, i, 0)),
            pl.BlockSpec((1, POOL_BUF * stride, d), lambda bi, i: (bi, 0, 0)),
        ],
        out_shape=[jax.ShapeDtypeStruct((bsz, rows, d), F32),
                   jax.ShapeDtypeStruct((bsz, POOL_BUF * stride, d), F32)],
        scratch_shapes=[pltpu.VMEM((pre_rows + tm, d), F32)],
        compiler_params=_params("arbitrary", "arbitrary"),
        name="pool_layer",
    )(x, prefix, w_in, w_grp, scale, w_out, g, b)


def _mm_kernel(x_ref, w_ref, o_ref):
    o_ref[...] = _bdot(x_ref[...], w_ref[...]).astype(o_ref.dtype)


def _matmul(x, w, *, tm, tn, out_dtype=F32):
    t, k = x.shape
    n = w.shape[1]
    return pl.pallas_call(
        _mm_kernel,
        grid=(n // tn, t // tm),
        in_specs=[pl.BlockSpec((tm, k), lambda j, i: (i, 0)),
                  pl.BlockSpec((k, tn), lambda j, i: (0, j))],
        out_specs=pl.BlockSpec((tm, tn), lambda j, i: (i, j)),
        out_shape=jax.ShapeDtypeStruct((t, n), out_dtype),
        compiler_params=_params("arbitrary", "arbitrary"),
        name="matmul",
    )(x, w)


def _mm_res_ln_kernel(a_ref, w_ref, x_ref, g_ref, b_ref, o_ref):
    mix = _bdot(a_ref[...], w_ref[...])
    o_ref[...] = _ln(DN_ALPHA * x_ref[...] + mix, g_ref[...], b_ref[...])


def _mm_res_ln(a, w, x, g, b, *, tm):
    t, k = a.shape
    d = w.shape[1]
    return pl.pallas_call(
        _mm_res_ln_kernel,
        grid=(t // tm,),
        in_specs=[pl.BlockSpec((tm, k), lambda i: (i, 0)),
                  pl.BlockSpec((k, d), lambda i: (0, 0)),
                  pl.BlockSpec((tm, d), lambda i: (i, 0)),
                  pl.BlockSpec((1, d), lambda i: (0, 0)),
                  pl.BlockSpec((1, d), lambda i: (0, 0))],
        out_specs=pl.BlockSpec((tm, d), lambda i: (i, 0)),
        out_shape=jax.ShapeDtypeStruct((t, d), F32),
        compiler_params=_params("arbitrary"),
        name="mm_res_ln",
    )(a, w, x, g, b)


def _qkv_kernel(x_ref, w_ref, cos_ref, sin_ref, q_ref, k_ref, v_ref, *, tm):
    x = x_ref[0].astype(BF16)
    cos = cos_ref[...]
    sin = sin_ref[...]
    lane = lax.broadcasted_iota(jnp.int32, (tm, LANES), 1)
    first_half = (lane % DIFF_DH) < (DIFF_DH // 2)

    def rope(y):
        outs = []
        for c in range(D_MODEL // LANES):
            yc = y[:, c * LANES:(c + 1) * LANES]
            sw = jnp.where(first_half, pltpu.roll(yc, LANES - DIFF_DH // 2, 1),
                           pltpu.roll(yc, DIFF_DH // 2, 1))
            outs.append(yc * cos + sw * sin)
        return jnp.concatenate(outs, axis=1)

    q_ref[0] = rope(jnp.dot(x, w_ref[:, 0:D_MODEL], preferred_element_type=F32))
    k_ref[0] = rope(jnp.dot(x, w_ref[:, D_MODEL:2 * D_MODEL], preferred_element_type=F32))
    v_ref[0] = jnp.dot(x, w_ref[:, 2 * D_MODEL:3 * D_MODEL], preferred_element_type=F32)


def _rope_tables(pos):
    half = DIFF_DH // 2
    inv = ROPE_THETA ** (-jnp.arange(half, dtype=F32) * 2.0 / DIFF_DH)
    ang = pos.astype(F32)[:, None] * inv[None, :]
    cos, sin = jnp.cos(ang), jnp.sin(ang)
    reps = LANES // DIFF_DH
    cos_t = jnp.tile(jnp.concatenate([cos, cos], axis=1), (1, reps))
    sin_t = jnp.tile(jnp.concatenate([-sin, sin], axis=1), (1, reps))
    return cos_t, sin_t


def _qkv_rope(x, w, cos_t, sin_t, *, tm):
    bsz, rows, d = x.shape
    kern = functools.partial(_qkv_kernel, tm=tm)
    spec = pl.BlockSpec((1, tm, d), lambda bi, i: (bi, i, 0))
    tab = pl.BlockSpec((tm, LANES), lambda bi, i: (i, 0))
    shp = jax.ShapeDtypeStruct((bsz, rows, d), F32)
    return pl.pallas_call(
        kern,
        grid=(bsz, rows // tm),
        in_specs=[spec, pl.BlockSpec((d, 3 * d), lambda bi, i: (0, 0)), tab, tab],
        out_specs=[spec, spec, spec],
        out_shape=[shp, shp, shp],
        compiler_params=_params("arbitrary", "arbitrary"),
        name="qkv_rope",
    )(x, w, cos_t, sin_t)


def _diff_lambda(lq_ref, lk_ref, lam_init):
    dots = jnp.sum(lq_ref[...] * lk_ref[...], axis=-1, keepdims=True)
    e = jnp.exp(dots)
    return e[0:1, :] - e[1:2, :] + lam_init


def _head_norm(o, ng, lam_init):
    ms = jnp.mean(o * o, axis=-1, keepdims=True)
    return o * lax.rsqrt(ms + NORM_EPS) * ng * (1.0 - lam_init)


def _attn_kernel(q_ref, k_ref, v_ref, lq_ref, lk_ref, ng_ref, o_ref, *, tq, lam_init):
    qi = pl.program_id(2)
    q = q_ref[0] * (DIFF_DH ** -0.5)
    q1 = q[:, :DIFF_DH].astype(BF16)
    q2 = q[:, DIFF_DH:].astype(BF16)
    ones = jnp.ones((tq, DIFF_VD), BF16)

    def scores(kb):
        start = pl.multiple_of(kb * tq, tq)
        k = k_ref[0, pl.ds(start, tq), :]
        v1 = jnp.concatenate([v_ref[0, pl.ds(start, tq), :].astype(BF16), ones], axis=1)
        s = jnp.concatenate([_bdot_nt(q1, k[:, :DIFF_DH]), _bdot_nt(q2, k[:, DIFF_DH:])], axis=0)
        return s, v1

    def update(s, v1, m, a):
        mn = jnp.maximum(m, jnp.max(s, axis=1, keepdims=True))
        p = jnp.exp(s - mn).astype(BF16)
        a = jnp.exp(m - mn) * a + jnp.dot(p, v1, preferred_element_type=F32)
        return mn, a

    def body(kb, carry):
        s, v1 = scores(kb)
        return update(s, v1, *carry)

    init = (jnp.full((2 * tq, 1), -jnp.inf, F32), jnp.zeros((2 * tq, 2 * DIFF_VD), F32))
    m, a = lax.fori_loop(0, qi, body, init)
    s, v1 = scores(qi)
    row = lax.broadcasted_iota(jnp.int32, (2 * tq, tq), 0) % tq
    col = lax.broadcasted_iota(jnp.int32, (2 * tq, tq), 1)
    m, a = update(jnp.where(col <= row, s, -jnp.inf), v1, m, a)
    lam = _diff_lambda(lq_ref, lk_ref, lam_init)
    o = (a[:tq, :DIFF_VD] / a[:tq, DIFF_VD:DIFF_VD + 1]
         - lam * (a[tq:, :DIFF_VD] / a[tq:, DIFF_VD:DIFF_VD + 1]))
    o_ref[0] = _head_norm(o, ng_ref[...], lam_init).astype(o_ref.dtype)


def _diff_attn_prompt(q, k, v, lq, lk, ng, lam_init, *, tq):
    bsz, seq, d = q.shape
    kern = functools.partial(_attn_kernel, tq=tq, lam_init=lam_init)
    small = lambda *shape: pl.BlockSpec(shape, lambda bi, h, i: (0,) * len(shape))
    return pl.pallas_call(
        kern,
        grid=(bsz, DIFF_HEADS, seq // tq),
        in_specs=[pl.BlockSpec((1, tq, DIFF_VD), lambda bi, h, i: (bi, i, h)),
                  pl.BlockSpec((1, seq, DIFF_VD), lambda bi, h, i: (bi, 0, h)),
                  pl.BlockSpec((1, seq, DIFF_VD), lambda bi, h, i: (bi, 0, h)),
                  small(2, DIFF_DH), small(2, DIFF_DH), small(1, DIFF_VD)],
        out_specs=pl.BlockSpec((1, tq, DIFF_VD), lambda bi, h, i: (bi, i, h)),
        out_shape=jax.ShapeDtypeStruct((bsz, seq, d), BF16),
        compiler_params=_params("arbitrary", "arbitrary", "arbitrary"),
        name="diff_attn_prompt",
    )(q, k, v, lq, lk, ng)


DEC_TPAD = SUBLANES // 2
DEC_ROWS = DIFF_HEADS * 2 * DEC_TPAD


def _decode_kernel(pt_ref, q_ref, *refs, pages_per_step, n_steps, lam_init):
    g = pages_per_step
    k_refs = refs[:g]
    v_refs = refs[g:2 * g]
    knew_ref, vnew_ref, rep_ref, lq_ref, lk_ref, ng_ref, o_ref, m_ref, l_ref, acc_ref = refs[2 * g:]
    s = pl.program_id(1)
    scale = DIFF_DH ** -0.5
    rep = rep_ref[...]
    row_head = lax.broadcasted_iota(jnp.int32, (DEC_ROWS, PAGE_SIZE * DIFF_HEADS), 0) // (2 * DEC_TPAD)
    col_head = lax.broadcasted_iota(jnp.int32, (DEC_ROWS, PAGE_SIZE * DIFF_HEADS), 1) % DIFF_HEADS
    own_head = row_head == col_head

    @pl.when(s == 0)
    def _():
        m_ref[...] = jnp.full(m_ref.shape, -jnp.inf, F32)
        l_ref[...] = jnp.zeros(l_ref.shape, F32)
        acc_ref[...] = jnp.zeros(acc_ref.shape, F32)

    q = (q_ref[0] * scale).astype(BF16)

    def block(ks, vs, mask):
        sc = jnp.concatenate([_bdot(q, k[0]) for k in ks], axis=1)
        if mask is not None:
            sc = jnp.where(mask, sc, -jnp.inf)
        m_old = m_ref[...]
        mn = jnp.maximum(m_old, jnp.max(sc, axis=1, keepdims=True))
        alpha = jnp.exp(m_old - mn)
        p = jnp.exp(sc - mn)
        pv = None
        for gi in range(len(vs)):
            pe = jnp.dot(p[:, gi * PAGE_SIZE:(gi + 1) * PAGE_SIZE].astype(BF16), rep, preferred_element_type=F32)
            part = _bdot(jnp.where(own_head, pe, 0.0), vs[gi][0])
            pv = part if pv is None else pv + part
        l_ref[...] = alpha * l_ref[...] + jnp.sum(p, axis=1, keepdims=True)
        acc_ref[...] = alpha * acc_ref[...] + pv
        m_ref[...] = mn

    block(k_refs, v_refs, None)

    @pl.when(s == n_steps - 1)
    def _():
        t_row = lax.broadcasted_iota(jnp.int32, (DEC_ROWS, PAGE_SIZE), 0) % DEC_TPAD
        key = lax.broadcasted_iota(jnp.int32, (DEC_ROWS, PAGE_SIZE), 1)
        block([knew_ref], [vnew_ref], key <= t_row)
        lam = _diff_lambda(lq_ref, lk_ref, lam_init)
        ng = ng_ref[...]
        for h in range(DIFF_HEADS):
            r0 = h * SUBLANES
            cols = slice(h * DIFF_VD, (h + 1) * DIFF_VD)
            a = acc_ref[r0:r0 + SUBLANES, :] / l_ref[r0:r0 + SUBLANES, :]
            o = a - lam * pltpu.roll(a, DEC_TPAD, 0)
            o_ref[0, :, cols] = _head_norm(o, ng, lam_init).astype(o_ref.dtype)


def _diff_attn_sample(qbd, cache_k, cache_v, page_table, knew, vnew, lq, lk, ng, lam_init,
                      *, pages_per_step, page0):
    bsz = qbd.shape[0]
    n_pages = page_table.shape[1]
    g = pages_per_step
    n_steps = n_pages // g
    kern = functools.partial(_decode_kernel, pages_per_step=g, n_steps=n_steps, lam_init=lam_init)
    rep = (jnp.arange(PAGE_SIZE * DIFF_HEADS)[None, :] // DIFF_HEADS == jnp.arange(PAGE_SIZE)[:, None]).astype(BF16)

    def page_spec(gi):
        return pl.BlockSpec((1, D_MODEL, PAGE_SIZE), lambda b, s, pt: (page0 + pt[b, s * g + gi], 0, 0))

    per_b = lambda rows, cols: pl.BlockSpec((1, rows, cols), lambda b, s, pt: (b, 0, 0))
    small = lambda *shape: pl.BlockSpec(shape, lambda b, s, pt: (0,) * len(shape))
    grid_spec = pltpu.PrefetchScalarGridSpec(
        num_scalar_prefetch=1,
        grid=(bsz, n_steps),
        in_specs=([per_b(DEC_ROWS, D_MODEL)] + [page_spec(gi) for gi in range(g)] + [page_spec(gi) for gi in range(g)]
                  + [per_b(D_MODEL, PAGE_SIZE), per_b(D_MODEL, PAGE_SIZE), small(PAGE_SIZE, PAGE_SIZE * DIFF_HEADS),
                     small(2, DIFF_DH), small(2, DIFF_DH), small(1, DIFF_VD)]),
        out_specs=per_b(SUBLANES, D_MODEL),
        scratch_shapes=[pltpu.VMEM((DEC_ROWS, 1), F32), pltpu.VMEM((DEC_ROWS, 1), F32),
                        pltpu.VMEM((DEC_ROWS, DIFF_VD), F32)],
    )
    return pl.pallas_call(
        kern,
        grid_spec=grid_spec,
        out_shape=jax.ShapeDtypeStruct((bsz, SUBLANES, D_MODEL), BF16),
        compiler_params=_params("arbitrary", "arbitrary"),
        name="diff_attn_sample",
    )(page_table, qbd, *([cache_k] * g), *([cache_v] * g), knew, vnew, rep, lq, lk, ng)


CONV_PAD = SUBLANES


def _ssd_kernel(*refs, q, n_chunks, valid_len, has_h0):
    if has_h0:
        (z_ref, xbc_ref, dt_ref, dtt_ref, cpre_ref, cw_ref, cb_ref, dtb_row_ref, dtb_col_ref,
         alog_row_ref, alog_col_ref, dexp_ref, ng_ref, expand_ref, h0_ref,
         y_ref, hl_ref, s_ref, cext_ref) = refs
    else:
        (z_ref, xbc_ref, dt_ref, dtt_ref, cpre_ref, cw_ref, cb_ref, dtb_row_ref, dtb_col_ref,
         alog_row_ref, alog_col_ref, dexp_ref, ng_ref, expand_ref,
         y_ref, hl_ref, s_ref, cext_ref) = refs
        h0_ref = None
    c = pl.program_id(1)

    @pl.when(c == 0)
    def _():
        if has_h0:
            s_ref[...] = h0_ref[0]
        else:
            s_ref[...] = jnp.zeros(s_ref.shape, F32)
        cext_ref[0:CONV_PAD, :] = cpre_ref[0]

    @pl.when(c > 0)
    def _():
        cext_ref[0:CONV_PAD, :] = cext_ref[q:q + CONV_PAD, :]

    cext_ref[CONV_PAD:CONV_PAD + q, :] = xbc_ref[0]
    xbc = cb_ref[...]
    for kk in range(SSD_CONV):
        r0 = CONV_PAD - (SSD_CONV - 1) + kk
        xbc = xbc + cext_ref[r0:r0 + q, :] * cw_ref[kk:kk + 1, :]
    xbc = _silu(xbc)
    xs = xbc[:, :SSD_DI]
    bm = xbc[:, SSD_DI:SSD_DI + SSD_G * SSD_N]
    cm = xbc[:, SSD_DI + SSD_G * SSD_N:]

    dtc = _softplus(dt_ref[0] + dtb_row_ref[...])
    dtr = _softplus(dtt_ref[0] + dtb_col_ref[...])
    if valid_len is not None:
        t_col = c * q + lax.broadcasted_iota(jnp.int32, (q, LANES), 0)
        t_row = c * q + lax.broadcasted_iota(jnp.int32, (SSD_H, q), 1)
        dtc = jnp.where(t_col < valid_len, dtc, 0.0)
        dtr = jnp.where(t_row < valid_len, dtr, 0.0)
    ii = lax.broadcasted_iota(jnp.int32, (q, q), 0)
    jj = lax.broadcasted_iota(jnp.int32, (q, q), 1)
    causal = jj <= ii
    tri = jnp.where(causal, 1.0, 0.0).astype(BF16)
    tri_t = jnp.where(ii <= jj, 1.0, 0.0).astype(BF16)
    acum_c = _dot_exact_lhs(tri, dtc * (-jnp.exp(alog_row_ref[...])))
    acum_r = _dot_exact_rhs(dtr * (-jnp.exp(alog_col_ref[...])), tri_t)
    total = acum_c[q - 1:q, :]
    expand = expand_ref[...]
    dte_x = _dot_exact_rhs(jnp.exp(total - acum_c) * dtc, expand)
    eac_x = _dot_exact_rhs(jnp.exp(acum_c), expand)
    cdec_x = _dot_exact_rhs(jnp.broadcast_to(jnp.exp(total), (SUBLANES, LANES)), expand)[0:1, :]

    gw = SSD_DI // SSD_G
    hg = SSD_H // SSD_G
    lane = lax.broadcasted_iota(jnp.int32, (q, 2 * SSD_P), 1)
    ys = []
    for g in range(SSD_G):
        gc = slice(g * gw, (g + 1) * gw)
        bg = bm[:, g * SSD_N:(g + 1) * SSD_N]
        cg = cm[:, g * SSD_N:(g + 1) * SSD_N]
        cbm = _bdot_nt(cg, bg)
        xg = xs[:, gc]
        s_old = s_ref[:, gc]
        y_g = _bdot(cg, s_old) * eac_x[:, gc]
        s_ref[:, gc] = s_old * cdec_x[:, gc] + _bdot(bg.T, xg * dte_x[:, gc])
        pairs = []
        for hp in range(hg // 2):
            wts = []
            for hh in range(2):
                h = g * hg + 2 * hp + hh
                seg = acum_c[:, h:h + 1] - acum_r[h:h + 1, :]
                dec = jnp.exp(jnp.where(causal, seg, -jnp.inf))
                wts.append((cbm * dec * dtr[h:h + 1, :]).astype(BF16))
            xp = xg[:, hp * 2 * SSD_P:(hp + 1) * 2 * SSD_P]
            xbd = jnp.concatenate([jnp.where(lane < SSD_P, xp, 0.0), jnp.where(lane >= SSD_P, xp, 0.0)],
                                  axis=0).astype(BF16)
            pairs.append(jnp.dot(jnp.concatenate(wts, axis=1), xbd, preferred_element_type=F32))
        y_g = y_g + jnp.concatenate(pairs, axis=1) + xg * dexp_ref[:, gc]
        zg = z_ref[0, :, gc]
        y_g = y_g * _silu(zg)
        y_g = y_g * lax.rsqrt(jnp.mean(y_g * y_g, axis=-1, keepdims=True) + NORM_EPS)
        ys.append(y_g)
    y_ref[0] = (jnp.concatenate(ys, axis=1) * ng_ref[...]).astype(y_ref.dtype)

    @pl.when(c == n_chunks - 1)
    def _():
        hl_ref[0] = s_ref[...]


def _ssd_scan(z, xbc, dt, dtt, cpre, conv_w, conv_b, dtb_row, dtb_col, alog_row, alog_col, d_exp, ng,
              expand, h0t, *, valid_len):
    bsz, seq, _ = z.shape
    q = SSD_CHUNK
    n_chunks = seq // q
    has_h0 = h0t is not None
    kern = functools.partial(_ssd_kernel, q=q, n_chunks=n_chunks, valid_len=valid_len, has_h0=has_h0)
    full = lambda *shape: pl.BlockSpec(shape, lambda b, c: (0,) * len(shape))
    in_specs = [
        pl.BlockSpec((1, q, SSD_DI), lambda b, c: (b, c, 0)),
        pl.BlockSpec((1, q, SSD_CONV_DIM), lambda b, c: (b, c, 0)),
        pl.BlockSpec((1, q, LANES), lambda b, c: (b, c, 0)),
        pl.BlockSpec((1, SSD_H, q), lambda b, c: (b, 0, c)),
        pl.BlockSpec((1, CONV_PAD, SSD_CONV_DIM), lambda b, c: (b, 0, 0)),
        full(SSD_CONV, SSD_CONV_DIM), full(1, SSD_CONV_DIM),
        full(1, LANES), full(SSD_H, 1), full(1, LANES), full(SSD_H, 1),
        full(1, SSD_DI), full(1, SSD_DI), full(LANES, SSD_DI),
    ]
    args = [z, xbc, dt, dtt, cpre, conv_w, conv_b, dtb_row, dtb_col, alog_row, alog_col, d_exp, ng, expand]
    if has_h0:
        in_specs.append(pl.BlockSpec((1, SSD_N, SSD_DI), lambda b, c: (b, 0, 0)))
        args.append(h0t)
    return pl.pallas_call(
        kern,
        grid=(bsz, n_chunks),
        in_specs=in_specs,
        out_specs=[pl.BlockSpec((1, q, SSD_DI), lambda b, c: (b, c, 0)),
                   pl.BlockSpec((1, SSD_N, SSD_DI), lambda b, c: (b, 0, 0))],
        out_shape=[jax.ShapeDtypeStruct((bsz, seq, SSD_DI), BF16),
                   jax.ShapeDtypeStruct((bsz, SSD_N, SSD_DI), F32)],
        scratch_shapes=[pltpu.VMEM((SSD_N, SSD_DI), F32), pltpu.VMEM((CONV_PAD + q, SSD_CONV_DIM), F32)],
        compiler_params=_params("arbitrary", "arbitrary"),
        name="ssd_scan",
    )(*args)


def _first_max(vals):
    m = vals[0]
    for v in vals[1:]:
        m = jnp.maximum(m, v)
    flags = []
    taken = jnp.zeros_like(m)
    for v in vals:
        f = jnp.where(v == m, 1.0 - taken, 0.0)
        taken = taken + f
        flags.append(f)
    return m, flags


def _sum_rows(rows):
    tot = rows[0]
    for r in rows[1:]:
        tot = tot + r
    return tot


def _router_rows(x, wh_ref, wl_ref, b_ref):
    xh = x.astype(BF16)
    xl = (x - xh.astype(F32)).astype(BF16)
    logits = (jnp.dot(xh, wh_ref[...], preferred_element_type=F32)
              + jnp.dot(xh, wl_ref[...], preferred_element_type=F32)
              + jnp.dot(xl, wh_ref[...], preferred_element_type=F32))
    lt = logits.T
    rows = [lt[e:e + 1, :] for e in range(N_EXPERTS)]
    mx = rows[0]
    for r in rows[1:]:
        mx = jnp.maximum(mx, r)
    ex = [jnp.exp(r - mx) for r in rows]
    den = _sum_rows(ex)
    probs = [r / den for r in ex]
    sel = [probs[e] + b_ref[e:e + 1, :] for e in range(N_EXPERTS)]
    scores, chosen = [], []
    for g in range(N_EGROUPS):
        v = sel[g * EXP_PER_GROUP:(g + 1) * EXP_PER_GROUP]
        m1, f1 = _first_max(v)
        rest = [jnp.where(f > 0.0, -jnp.inf, vi) for f, vi in zip(f1, v)]
        m2, f2 = _first_max(rest)
        scores.append(m1 + m2)
        chosen.extend([a + b for a, b in zip(f1, f2)])
    return probs, chosen, scores


def _router_kernel(x_ref, wh_ref, wl_ref, b_ref, o_ref):
    probs, chosen, scores = _router_rows(x_ref[...], wh_ref, wl_ref, b_ref)
    _, gflag = _first_max(scores)
    gates = [probs[e] * chosen[e] * gflag[e // EXP_PER_GROUP] for e in range(N_EXPERTS)]
    tot = _sum_rows(gates)
    for e in range(N_EXPERTS):
        o_ref[e:e + 1, :] = gates[e] / tot


def _route_group_kernel(x_ref, wh_ref, wl_ref, b_ref, grp_ref, gate_ref, xbf_ref, *, tm):
    xbf_ref[...] = x_ref[...].astype(BF16)
    probs, chosen, scores = _router_rows(x_ref[...], wh_ref, wl_ref, b_ref)
    _, gflag = _first_max(scores)
    grp = _sum_rows([gflag[g] * float(g) for g in range(1, N_EGROUPS)])
    grp_ref[...] = grp.astype(jnp.int32)
    local = [_sum_rows([probs[g * EXP_PER_GROUP + le] * chosen[g * EXP_PER_GROUP + le] * gflag[g]
                        for g in range(N_EGROUPS)]) for le in range(EXP_PER_GROUP)]
    tot = _sum_rows(local)
    rows = jnp.concatenate([r / tot for r in local] + [jnp.zeros((LANES - EXP_PER_GROUP, tm), F32)], axis=0)
    gate_ref[...] = rows.T


def _route_group(x, wr_hi, wr_lo, b_col, *, tm):
    t, d = x.shape
    return pl.pallas_call(
        functools.partial(_route_group_kernel, tm=tm),
        grid=(t // tm,),
        in_specs=[pl.BlockSpec((tm, d), lambda i: (i, 0)),
                  pl.BlockSpec((d, LANES), lambda i: (0, 0)),
                  pl.BlockSpec((d, LANES), lambda i: (0, 0)),
                  pl.BlockSpec((N_EXPERTS, 1), lambda i: (0, 0))],
        out_specs=[pl.BlockSpec((1, tm), lambda i: (0, i)), pl.BlockSpec((tm, LANES), lambda i: (i, 0)),
                   pl.BlockSpec((tm, d), lambda i: (i, 0))],
        out_shape=[jax.ShapeDtypeStruct((1, t), jnp.int32), jax.ShapeDtypeStruct((t, LANES), F32),
                   jax.ShapeDtypeStruct((t, d), BF16)],
        compiler_params=_params("arbitrary"),
        name="route_group",
    )(x, wr_hi, wr_lo, b_col)


def _moe_sorted_kernel(sw_ref, sslot_ref, se_ref, ns_ref, x_ref, pos_ref, gate_ref, wg_ref, wu_ref, wd_ref, o_ref,
                       g3_ref, xt_ref, gcol_ref, acc_ref, *, tr, steps):
    sb = pl.program_id(0)
    s = pl.program_id(1)
    valid = s < ns_ref[sb]
    slot = sslot_ref[sb * steps + s]
    e = se_ref[sb * steps + s]

    @pl.when(s == 0)
    def _():
        hi, mid, lo = _split3(gate_ref[...])
        g3_ref[0] = hi
        g3_ref[1] = mid
        g3_ref[2] = lo
        o_ref[...] = jnp.zeros(o_ref.shape, o_ref.dtype)

    @pl.when(jnp.logical_and(valid, e == 0))
    def _():
        n_src = x_ref.shape[0]
        row = slot * tr + lax.broadcasted_iota(jnp.int32, (tr, n_src), 0)
        pick = jnp.where(pos_ref[0] == row, 1.0, 0.0).astype(BF16)
        xt_ref[slot] = jnp.dot(pick, x_ref[...], preferred_element_type=F32).astype(BF16)
        gcol_ref[slot] = (jnp.dot(pick, g3_ref[0], preferred_element_type=F32)
                          + jnp.dot(pick, g3_ref[1], preferred_element_type=F32)
                          + jnp.dot(pick, g3_ref[2], preferred_element_type=F32))
        acc_ref[slot] = jnp.zeros(acc_ref.shape[1:], F32)

    @pl.when(valid)
    def _():
        xb = xt_ref[slot]
        gates = gcol_ref[slot]
        lane = lax.broadcasted_iota(jnp.int32, gates.shape, 1)
        gcol = jnp.sum(jnp.where(lane == e, gates, 0.0), axis=1, keepdims=True)
        hgate = jnp.dot(xb, wg_ref[0], preferred_element_type=F32)
        hup = jnp.dot(xb, wu_ref[0], preferred_element_type=F32)
        acc_ref[slot] += _bdot(_silu(hgate) * hup * gcol, wd_ref[0])

    @pl.when(jnp.logical_and(valid, e == EXP_PER_GROUP - 1))
    def _():
        o_ref[pl.ds(pl.multiple_of(slot * tr, tr), tr), :] = acc_ref[slot].astype(o_ref.dtype)


def _moe_sorted(x_bf, pos_row, gate4, plan, w_gate, w_up, w_down, *, sb_rows, tr):
    t, d = x_bf.shape
    f = w_gate.shape[2]
    n_sb = t // sb_rows
    cap = sb_rows + N_EGROUPS * tr
    slots = cap // tr
    steps = slots * EXP_PER_GROUP
    w_index = lambda sb, s, sw, sslot, se, ns: (sw[sb * steps + s], 0, 0)
    per_sb = lambda *shape: pl.BlockSpec(shape, lambda sb, s, sw, sslot, se, ns: (sb,) + (0,) * (len(shape) - 1))
    grid_spec = pltpu.PrefetchScalarGridSpec(
        num_scalar_prefetch=4,
        grid=(n_sb, steps),
        in_specs=[per_sb(sb_rows, d), per_sb(1, 1, sb_rows), per_sb(sb_rows, LANES),
                  pl.BlockSpec((1, d, f), w_index), pl.BlockSpec((1, d, f), w_index),
                  pl.BlockSpec((1, f, d), w_index)],
        out_specs=per_sb(cap, d),
        scratch_shapes=[pltpu.VMEM((3, sb_rows, LANES), BF16), pltpu.VMEM((slots, tr, d), BF16),
                        pltpu.VMEM((slots, tr, LANES), F32), pltpu.VMEM((slots, tr, d), F32)],
    )
    return pl.pallas_call(
        functools.partial(_moe_sorted_kernel, tr=tr, steps=steps),
        grid_spec=grid_spec,
        out_shape=jax.ShapeDtypeStruct((n_sb * cap, d), BF16),
        compiler_params=_params("arbitrary", "arbitrary"),
        name="moe_sorted_experts",
    )(*plan, x_bf, pos_row, gate4, w_gate, w_up, w_down)


def _dispatch_plan(grp, sb_rows, tr):
    n_sb = grp.shape[0] // sb_rows
    slots = (sb_rows + N_EGROUPS * tr) // tr
    steps = slots * EXP_PER_GROUP
    i32 = jnp.int32
    oh = (grp.reshape(n_sb, sb_rows, 1) == jnp.arange(N_EGROUPS, dtype=i32)).astype(i32)
    csum = jnp.cumsum(oh, axis=1)
    padded = ((csum[:, -1] + tr - 1) // tr) * tr
    gend = jnp.cumsum(padded, axis=1)
    pos = jnp.sum(oh * ((gend - padded)[:, None, :] + csum - 1), axis=2)
    slot_row0 = jnp.arange(slots, dtype=i32) * tr
    tile_group = jnp.minimum(jnp.sum((slot_row0[None, :, None] >= gend[:, None, :]).astype(i32), axis=2),
                             N_EGROUPS - 1)
    n_used = gend[:, -1] // tr
    pair_slot = jnp.repeat(jnp.arange(slots, dtype=i32), EXP_PER_GROUP)[None, :]
    pair_e = jnp.tile(jnp.arange(EXP_PER_GROUP, dtype=i32), slots)[None, :]
    pair_w = (jnp.take_along_axis(tile_group, jnp.broadcast_to(pair_slot, (n_sb, steps)), axis=1) * EXP_PER_GROUP
              + pair_e)
    key = jnp.where(pair_slot < n_used[:, None], pair_w * slots + pair_slot, N_EXPERTS * slots + pair_slot)
    order = jnp.argsort(key, axis=1).astype(i32)
    n_steps = n_used * EXP_PER_GROUP
    step_w = jnp.take_along_axis(pair_w, order, axis=1)
    last_w = jnp.take_along_axis(step_w, jnp.maximum(n_steps - 1, 0)[:, None], axis=1)
    step_w = jnp.where(jnp.arange(steps, dtype=i32)[None, :] < n_steps[:, None], step_w, last_w)
    plan = (step_w.reshape(-1).astype(i32), (order // EXP_PER_GROUP).reshape(-1), (order % EXP_PER_GROUP).reshape(-1),
            n_steps.astype(i32))
    return pos.astype(i32), plan


def _router(x, wr_hi, wr_lo, b_col, *, tm):
    t, d = x.shape
    return pl.pallas_call(
        _router_kernel,
        grid=(t // tm,),
        in_specs=[pl.BlockSpec((tm, d), lambda i: (i, 0)),
                  pl.BlockSpec((d, LANES), lambda i: (0, 0)),
                  pl.BlockSpec((d, LANES), lambda i: (0, 0)),
                  pl.BlockSpec((N_EXPERTS, 1), lambda i: (0, 0))],
        out_specs=pl.BlockSpec((N_EXPERTS, tm), lambda i: (0, i)),
        out_shape=jax.ShapeDtypeStruct((N_EXPERTS, t), F32),
        compiler_params=_params("arbitrary"),
        name="router",
    )(x, wr_hi, wr_lo, b_col)


def _moe_kernel(x_ref, gate_ref, wg_ref, wu_ref, wd_ref, o_ref, xbf_ref):
    e = pl.program_id(1)

    @pl.when(e == 0)
    def _():
        xbf_ref[...] = x_ref[...].astype(BF16)
        o_ref[...] = jnp.zeros(o_ref.shape, F32)

    xb = xbf_ref[...]
    gate = gate_ref[...]
    lane = lax.broadcasted_iota(jnp.int32, gate.shape, 1)
    gcol = jnp.sum(jnp.where(lane == e, gate, 0.0), axis=1, keepdims=True)
    hgate = jnp.dot(xb, wg_ref[0, 0].astype(BF16), preferred_element_type=F32)
    hup = jnp.dot(xb, wu_ref[0, 0].astype(BF16), preferred_element_type=F32)
    o_ref[...] += _bdot(_silu(hgate) * hup * gcol, wd_ref[0, 0])


def _moe(x, gate, w_gate, w_up, w_down, *, layer, tm):
    t, d = x.shape
    f = w_gate.shape[3]
    return pl.pallas_call(
        _moe_kernel,
        grid=(t // tm, N_EXPERTS),
        in_specs=[pl.BlockSpec((tm, d), lambda i, e: (i, 0)),
                  pl.BlockSpec((tm, N_EXPERTS), lambda i, e: (i, 0)),
                  pl.BlockSpec((1, 1, d, f), lambda i, e: (layer, e, 0, 0)),
                  pl.BlockSpec((1, 1, d, f), lambda i, e: (layer, e, 0, 0)),
                  pl.BlockSpec((1, 1, f, d), lambda i, e: (layer, e, 0, 0))],
        out_specs=pl.BlockSpec((tm, d), lambda i, e: (i, 0)),
        out_shape=jax.ShapeDtypeStruct((t, d), F32),
        scratch_shapes=[pltpu.VMEM((tm, d), BF16)],
        compiler_params=_params("arbitrary", "arbitrary"),
        name="moe_experts",
    )(x, gate, w_gate, w_up, w_down)


def _ln_ple_body(x1, ffn, p_ref, g_ref, b_ref, wg_ref, bg_ref, wp_ref, o_ref):
    x = _ln(DN_ALPHA * x1 + ffn, g_ref[...], b_ref[...])
    gate = jax.nn.sigmoid(_bdot(x, wg_ref[...]) + bg_ref[...])
    o_ref[...] = x + gate * _bdot(p_ref[...], wp_ref[...])


def _ln_ple_kernel(x_ref, f_ref, p_ref, *rest):
    _ln_ple_body(x_ref[...], f_ref[...], p_ref, *rest)


def _unsort_ln_ple_kernel(y_ref, pos_ref, x_ref, p_ref, *rest):
    tm, cap = x_ref.shape[0], y_ref.shape[0]
    col = lax.broadcasted_iota(jnp.int32, (tm, cap), 1)
    pick = jnp.where(pos_ref[...] == col, 1.0, 0.0).astype(BF16)
    ffn = jnp.dot(pick, y_ref[...], preferred_element_type=F32)
    _ln_ple_body(x_ref[...], ffn, p_ref, *rest)


def _unsort_ln_ple(y_sorted, pos_col, x, p, g, b, w_gate, b_gate, w_p, *, sb_rows, tm):
    t, d = x.shape
    pd = p.shape[1]
    n_sb = t // sb_rows
    n_sub = sb_rows // tm
    cap = y_sorted.shape[0] // n_sb
    rows = lambda cols: pl.BlockSpec((tm, cols), lambda sb, i: (sb * n_sub + i, 0))
    fixed = lambda r, c: pl.BlockSpec((r, c), lambda sb, i: (0, 0))
    return pl.pallas_call(
        _unsort_ln_ple_kernel,
        grid=(n_sb, n_sub),
        in_specs=[pl.BlockSpec((cap, d), lambda sb, i: (sb, 0)), rows(1), rows(d), rows(pd),
                  fixed(1, d), fixed(1, d), fixed(d, d), fixed(1, d), fixed(pd, d)],
        out_specs=rows(d),
        out_shape=jax.ShapeDtypeStruct((t, d), F32),
        compiler_params=_params("arbitrary", "arbitrary"),
        name="unsort_ln_ple",
    )(y_sorted, pos_col, x, p, g, b, w_gate, b_gate, w_p)


def _ln_ple(x, ffn, p, g, b, w_gate, b_gate, w_p, *, tm):
    t, d = x.shape
    pd = p.shape[1]
    rows = lambda cols: pl.BlockSpec((tm, cols), lambda i: (i, 0))
    fixed = lambda r, c: pl.BlockSpec((r, c), lambda i: (0, 0))
    return pl.pallas_call(
        _ln_ple_kernel,
        grid=(t // tm,),
        in_specs=[rows(d), rows(d), rows(pd), fixed(1, d), fixed(1, d), fixed(d, d), fixed(1, d), fixed(pd, d)],
        out_specs=rows(d),
        out_shape=jax.ShapeDtypeStruct((t, d), F32),
        compiler_params=_params("arbitrary"),
        name="ln_ple",
    )(x, ffn, p, g, b, w_gate, b_gate, w_p)


def _row_tile(rows, want):
    tm = min(rows, want)
    assert rows % tm == 0
    return tm


POOL_TM = 512
MM_TM = 512
QKV_TM = 512
ATTN_TQ = 512
MOE_TM = 1024
MOE_SORT_BLOCK = 2048
MOE_SORT_TR = 256
ROUTER_TM = 1024
PLE_TM = 512
DECODE_PAGES_PER_STEP = 16


def kernel(x_prompt, x_sample, p_prompt, p_sample, state_pool, cache_k, cache_v, page_table, state_ssm, state_conv,
           ln_g, ln_b, w_pool_in, w_pool_grp, pool_scale, w_pool_out, w_diff_in, diff_lambda_q, diff_lambda_k,
           diff_norm_g, w_diff_out, w_ssd_in, ssd_conv_w, ssd_conv_b, ssd_dt_bias, ssd_a_log, ssd_d, ssd_norm_g,
           w_ssd_out, w_router, b_router, w_exp_gate, w_exp_up, w_exp_down, w_ple, w_ple_gate, b_ple_gate):
    bp, seq, d = x_prompt.shape
    bs, tdec, _ = x_sample.shape
    n_pages = page_table.shape[1]
    past = n_pages * cache_k.shape[2]
    tp = bp * seq
    ts = bs * tdec

    def to_tm(a):
        return jnp.swapaxes(a, 0, 1).reshape((ts,) + a.shape[2:])

    def from_tm(a):
        return jnp.swapaxes(a.reshape((tdec, bs) + a.shape[1:]), 0, 1)

    row = lambda v: v.reshape(1, -1)

    wr = jnp.pad(w_router, ((0, 0), (0, LANES - N_EXPERTS)))
    wr_hi = wr.astype(BF16)
    wr_lo = (wr - wr_hi.astype(F32)).astype(BF16)
    br_col = b_router.reshape(N_EXPERTS, 1)

    def finish(x1, i, p, sparse):
        t = x1.shape[0]
        tail = (row(ln_g[i, 1]), row(ln_b[i, 1]), w_ple_gate[i].astype(BF16), row(b_ple_gate[i]),
                w_ple[i].astype(BF16))
        if sparse:
            sb_rows, tr = _row_tile(t, MOE_SORT_BLOCK), MOE_SORT_TR
            grp, gate4, x1_bf = _route_group(x1, wr_hi, wr_lo, br_col, tm=_row_tile(t, ROUTER_TM))
            pos, plan = _dispatch_plan(grp[0], sb_rows, tr)
            y_sorted = _moe_sorted(x1_bf, pos.reshape(-1, 1, sb_rows), gate4, plan,
                                   w_exp_gate[i].astype(BF16), w_exp_up[i].astype(BF16),
                                   w_exp_down[i].astype(BF16), sb_rows=sb_rows, tr=tr)
            return _unsort_ln_ple(y_sorted, pos.reshape(t, 1), x1, p, *tail, sb_rows=sb_rows,
                                  tm=_row_tile(sb_rows, PLE_TM))
        gate = _router(x1, wr_hi, wr_lo, br_col, tm=_row_tile(t, ROUTER_TM)).T
        ffn = _moe(x1, gate, w_exp_gate, w_exp_up, w_exp_down, layer=i, tm=_row_tile(t, MOE_TM))
        return _ln_ple(x1, ffn, p, *tail, tm=_row_tile(t, PLE_TM))

    xp = x_prompt.reshape(tp, d)
    xs = to_tm(x_sample)
    pos_p = jnp.arange(seq, dtype=jnp.int32)
    pos_s = jnp.repeat(past + jnp.arange(tdec, dtype=jnp.int32), bs)
    outs = {k: [] for k in ("pool_p", "pool_s", "k_p", "v_p", "k_s", "v_s", "ssm_p", "conv_p", "ssm_s", "conv_s")}

    for i in range(DEPTH):
        kind, j = i % N_MIXERS, i // N_MIXERS
        g1, b1 = row(ln_g[i, 0]), row(ln_b[i, 0])
        if kind == 0:
            wts = (w_pool_in[j].astype(BF16), w_pool_grp[j].astype(BF16), row(pool_scale[j]),
                   w_pool_out[j].astype(BF16), g1, b1)
            x1p, newp = _pool_layer(xp.reshape(bp, seq, d), jnp.zeros((bp, POOL_BUF + 1, d), F32), *wts,
                                    tm=_row_tile(seq, POOL_TM), stride=1, pos0=0)
            pre_s = jnp.swapaxes(state_pool[j], 0, 1).reshape(1, POOL_BUF * bs, d)
            pre_s = jnp.pad(pre_s, ((0, 0), (bs, 0), (0, 0)))
            x1s, news = _pool_layer(xs.reshape(1, ts, d), pre_s, *wts, tm=ts, stride=bs, pos0=past)
            outs["pool_p"].append(newp)
            outs["pool_s"].append(jnp.swapaxes(news.reshape(POOL_BUF, bs, d), 0, 1))
            x1p = x1p.reshape(tp, d)
            x1s = x1s.reshape(ts, d)
        elif kind == 1:
            lam_init = 0.8 - 0.6 * math.exp(-0.3 * i)
            w_in = w_diff_in[j].astype(BF16)
            w_out = w_diff_out[j].astype(BF16)
            lq, lk, ng = diff_lambda_q[j], diff_lambda_k[j], row(diff_norm_g[j])
            cos_p, sin_p = _rope_tables(pos_p)
            q, k, v = _qkv_rope(xp.reshape(bp, seq, d), w_in, cos_p, sin_p, tm=_row_tile(seq, QKV_TM))
            o = _diff_attn_prompt(q, k, v, lq, lk, ng, lam_init, tq=_row_tile(seq, ATTN_TQ))
            x1p = _mm_res_ln(o.reshape(tp, d), w_out, xp, g1, b1, tm=_row_tile(tp, MM_TM))
            outs["k_p"].append(k.reshape(bp, seq, DIFF_HEADS, 2, DIFF_DH))
            outs["v_p"].append(v.reshape(bp, seq, DIFF_HEADS, DIFF_VD))

            cos_s, sin_s = _rope_tables(pos_s)
            q, k, v = _qkv_rope(xs.reshape(1, ts, d), w_in, cos_s, sin_s, tm=ts)
            q, k, v = from_tm(q[0]), from_tm(k[0]), from_tm(v[0])
            q5 = q.reshape(bs, tdec, DIFF_HEADS * 2, DIFF_DH)
            eye = jnp.eye(DIFF_HEADS * 2, dtype=F32)
            qbd = jnp.einsum('bthd,hg->bhtgd', q5, eye)
            qbd = jnp.pad(qbd, ((0, 0), (0, 0), (0, DEC_TPAD - tdec), (0, 0), (0, 0)))
            qbd = qbd.reshape(bs, DEC_ROWS, d)
            kt_pages = jnp.transpose(cache_k, (0, 1, 3, 4, 5, 2)).reshape(-1, d, PAGE_SIZE)
            v_pages = cache_v.reshape(-1, PAGE_SIZE * DIFF_HEADS, DIFF_VD)
            knew = jnp.pad(jnp.swapaxes(k, 1, 2), ((0, 0), (0, 0), (0, PAGE_SIZE - tdec)))
            vnew = jnp.pad(v, ((0, 0), (0, PAGE_SIZE - tdec), (0, 0))).reshape(bs, PAGE_SIZE * DIFF_HEADS, DIFF_VD)
            o = _diff_attn_sample(qbd, kt_pages, v_pages, page_table, knew, vnew, lq, lk, ng, lam_init,
                                  pages_per_step=DECODE_PAGES_PER_STEP, page0=j * cache_k.shape[1])
            o = to_tm(o[:, :tdec])
            x1s = _mm_res_ln(o, w_out, xs, g1, b1, tm=ts)
            outs["k_s"].append(k.reshape(bs, tdec, DIFF_HEADS, 2, DIFF_DH))
            outs["v_s"].append(v.reshape(bs, tdec, DIFF_HEADS, DIFF_VD))
        else:
            w_in = w_ssd_in[j]
            w_z = w_in[:, :SSD_DI].astype(BF16)
            w_xbc = w_in[:, SSD_DI:SSD_DI + SSD_CONV_DIM].astype(BF16)
            w_dt = jnp.pad(w_in[:, SSD_DI + SSD_CONV_DIM:], ((0, 0), (0, LANES - SSD_H))).astype(BF16)
            w_out = w_ssd_out[j].astype(BF16)
            dtb_row = jnp.pad(ssd_dt_bias[j], (0, LANES - SSD_H)).reshape(1, LANES)
            alog_row = jnp.pad(ssd_a_log[j], (0, LANES - SSD_H)).reshape(1, LANES)
            dtb_col = ssd_dt_bias[j].reshape(SSD_H, 1)
            alog_col = ssd_a_log[j].reshape(SSD_H, 1)
            d_exp = jnp.repeat(ssd_d[j], SSD_P).reshape(1, SSD_DI)
            ng = row(ssd_norm_g[j])
            expand = (jnp.arange(SSD_DI)[None, :] // SSD_P == jnp.arange(LANES)[:, None]).astype(BF16)
            shared = (ssd_conv_w[j], row(ssd_conv_b[j]), dtb_row, dtb_col, alog_row, alog_col, d_exp, ng, expand)

            def in_proj(x, tm):
                z = _matmul(x, w_z, tm=tm, tn=1024)
                xbc = _matmul(x, w_xbc, tm=tm, tn=1024)
                dt = _matmul(x, w_dt, tm=tm, tn=LANES)
                return z, xbc, dt

            def hlast_to_state(hl, b):
                return jnp.transpose(hl.reshape(b, SSD_N, SSD_H, SSD_P), (0, 2, 3, 1))

            z, xbc, dt = in_proj(xp, _row_tile(tp, MM_TM))
            dtt = jnp.swapaxes(dt.reshape(bp, seq, LANES)[:, :, :SSD_H], 1, 2)
            y, hl = _ssd_scan(z.reshape(bp, seq, SSD_DI), xbc.reshape(bp, seq, SSD_CONV_DIM),
                              dt.reshape(bp, seq, LANES), dtt, jnp.zeros((bp, CONV_PAD, SSD_CONV_DIM), F32),
                              *shared, None, valid_len=None)
            x1p = _mm_res_ln(y.reshape(tp, SSD_DI), w_out, xp, g1, b1, tm=_row_tile(tp, MM_TM))
            outs["ssm_p"].append(hlast_to_state(hl, bp))
            outs["conv_p"].append(xbc.reshape(bp, seq, SSD_CONV_DIM)[:, seq - (SSD_CONV - 1):])

            z, xbc, dt = in_proj(xs, ts)
            z, xbc, dt = from_tm(z), from_tm(xbc), from_tm(dt)
            padt = ((0, 0), (0, SSD_CHUNK - tdec), (0, 0))
            dtp = jnp.pad(dt, padt)
            cpre = jnp.pad(state_conv[j], ((0, 0), (CONV_PAD - (SSD_CONV - 1), 0), (0, 0)))
            h0t = jnp.transpose(state_ssm[j], (0, 3, 1, 2)).reshape(bs, SSD_N, SSD_DI)
            y, hl = _ssd_scan(jnp.pad(z, padt), jnp.pad(xbc, padt), dtp, jnp.swapaxes(dtp[:, :, :SSD_H], 1, 2),
                              cpre, *shared, h0t, valid_len=tdec)
            x1s = _mm_res_ln(to_tm(y[:, :tdec]), w_out, xs, g1, b1, tm=ts)
            outs["ssm_s"].append(hlast_to_state(hl, bs))
            ext = jnp.concatenate([state_conv[j], xbc], axis=1)
            outs["conv_s"].append(ext[:, -(SSD_CONV - 1):])

        xp = finish(x1p, i, p_prompt[i].reshape(tp, PLE_DIM), sparse=True)
        xs = finish(x1s, i, to_tm(p_sample[i]), sparse=False)

    st = lambda name: jnp.stack(outs[name])
    return (xp.reshape(bp, seq, d), from_tm(xs), st("pool_p"), st("pool_s"), st("k_p"), st("v_p"), st("k_s"),
            st("v_s"), st("ssm_p"), st("conv_p"), st("ssm_s"), st("conv_s"))
```
